```python
import jax, jax.numpy as jnp
from jax import lax
import numpy as np

D_MODEL = 1024
BATCH = 1
SEQ = 16384
DEPTH = 1
DEC_BATCH = 2
DEC_SEQ = 16384
PAST_LEN = 128

MLA_HEADS = 8
MLA_NOPE = 64
MLA_ROPE = 32
MLA_V = 64
MLA_Q_LORA = 384
MLA_KV_LORA = 256
ROPE_THETA = 10000.0
Q_BLOCK = 128
GLA_HEADS = 4
GLA_DK = 64
GLA_DV = 128
GLA_GATE_RANK = 16
GLA_GATE_NORM = 16.0
GLA_CHUNK = 16
N_EXPERTS = 32
TOP_K = 4
D_FF = 1024
SWIGLU_LIMIT = 7.0
SWIGLU_ALPHA = 1.702
MOE_BLOCK = 128
RMS_EPS = 1e-6

MLA_WIDTH = MLA_HEADS * MLA_V
GLA_WIDTH = GLA_HEADS * GLA_DV
MIX_WIDTH = MLA_WIDTH + GLA_WIDTH
GLA_KDIM = GLA_HEADS * GLA_DK
IN_SIZES = (MLA_Q_LORA, MLA_KV_LORA, MLA_ROPE, GLA_KDIM, GLA_KDIM, GLA_WIDTH, 2 * GLA_GATE_RANK, GLA_WIDTH)
IN_COLS = MLA_Q_LORA + MLA_KV_LORA + MLA_ROPE + 2 * GLA_KDIM + GLA_WIDTH + 2 * GLA_GATE_RANK + GLA_WIDTH

kernel_name = "hymba_mla_gla_moe_encoder"

F32 = jnp.float32


def rms_norm(x, gain):
    xf = x.astype(F32)
    y = xf * lax.rsqrt(jnp.mean(xf * xf, axis=-1, keepdims=True) + RMS_EPS)
    return (y * gain.astype(F32)).astype(x.dtype)


def rope_tables(seq_len):
    inv_freq = jnp.power(ROPE_THETA, -jnp.arange(0, MLA_ROPE, 2, dtype=F32) / MLA_ROPE)
    ang = jnp.arange(seq_len, dtype=F32)[:, None] * inv_freq[None, :]
    return jnp.cos(ang), jnp.sin(ang)


def apply_rope(x, cos, sin):
    xf = x.astype(F32)
    x1, x2 = jnp.split(xf, 2, axis=-1)
    return jnp.concatenate([x1 * cos - x2 * sin, x1 * sin + x2 * cos], axis=-1).astype(x.dtype)


def mla_attention(q_nope, q_rope, k_nope, k_rope, v):
    B, S = q_nope.shape[:2]
    nb = S // Q_BLOCK
    qn_blocks = jnp.moveaxis(q_nope.reshape(B, nb, Q_BLOCK, MLA_HEADS, MLA_NOPE), 1, 0)
    qr_blocks = jnp.moveaxis(q_rope.reshape(B, nb, Q_BLOCK, MLA_HEADS, MLA_ROPE), 1, 0)

    def one_block(args):
        qn_b, qr_b = args
        s = (jnp.einsum('bqhd,bkhd->bhqk', qn_b, k_nope, preferred_element_type=F32)
             + jnp.einsum('bqhr,bkr->bhqk', qr_b, k_rope, preferred_element_type=F32))
        p = jax.nn.softmax(s, axis=-1)
        return jnp.einsum('bhqk,bkhd->bqhd', p.astype(v.dtype), v)

    out = lax.map(one_block, (qn_blocks, qr_blocks))
    return jnp.moveaxis(out, 0, 1).reshape(B, S, MLA_WIDTH)


def gla_chunked(q, k, v, log_a, strict):
    B, S, H, DK = q.shape
    DV = v.shape[-1]
    C = GLA_CHUNK
    nC = S // C
    qf = q.astype(F32).reshape(B, nC, C, H, DK)
    kf = k.astype(F32).reshape(B, nC, C, H, DK)
    vf = v.astype(F32).reshape(B, nC, C, H, DV)
    b = jnp.cumsum(log_a.reshape(B, nC, C, H, DK), axis=2)
    b_last = b[:, :, -1]
    diff = b[:, :, :, None] - b[:, :, None, :]
    mask = jnp.tril(jnp.ones((C, C), dtype=bool), k=-1 if strict else 0)
    decay = jnp.where(mask[:, :, None, None], jnp.exp(jnp.minimum(diff, 0.0)), 0.0)
    scores = jnp.einsum('bnihd,bnjhd,bnijhd->bnhij', qf, kf, decay)
    o_intra = jnp.einsum('bnhij,bnjhv->bnihv', scores, vf)
    k_to_end = kf * jnp.exp(b_last[:, :, None] - b)
    d_state = jnp.einsum('bnjhd,bnjhv->bnhdv', k_to_end, vf)

    def step(state, inp):
        log_decay, ds = inp
        return jnp.exp(log_decay)[..., None] * state + ds, state

    init = jnp.zeros((B, H, DK, DV), F32)
    _, s_before = lax.scan(step, init, (jnp.moveaxis(b_last, 1, 0), jnp.moveaxis(d_state, 1, 0)))
    s_before = jnp.moveaxis(s_before, 0, 1)
    o_inter = jnp.einsum('bnihd,bnhdv->bnihv', qf * jnp.exp(b), s_before)
    return (o_intra + o_inter).reshape(B, S, H, DV)


def moe_ffn(h, router_w, router_b, w_gu, b_gu, w_dn, b_dn):
    T = h.shape[0]
    logits = (h @ router_w + router_b).astype(F32)
    top_v, top_i = lax.top_k(logits, TOP_K)
    gates = jax.nn.softmax(top_v, axis=-1)
    n_assign = T * TOP_K
    e_flat = top_i.reshape(-1).astype(jnp.int32)
    tok_flat = jnp.arange(n_assign, dtype=jnp.int32) // TOP_K
    g_flat = gates.reshape(-1)
    order = jnp.argsort(e_flat)
    e_sorted = e_flat[order]
    counts = jnp.zeros((N_EXPERTS,), jnp.int32).at[e_flat].add(1)
    starts = jnp.cumsum(counts) - counts
    padded = ((counts + MOE_BLOCK - 1) // MOE_BLOCK) * MOE_BLOCK
    pends = jnp.cumsum(padded)
    pstarts = pends - padded
    dest = pstarts[e_sorted] + (jnp.arange(n_assign, dtype=jnp.int32) - starts[e_sorted])
    n_rows = n_assign + N_EXPERTS * MOE_BLOCK
    n_blocks = n_rows // MOE_BLOCK
    row_tok = jnp.full((n_rows,), T, jnp.int32).at[dest].set(tok_flat[order])
    row_gate = jnp.zeros((n_rows,), F32).at[dest].set(g_flat[order])
    block_expert = jnp.minimum(
        jnp.searchsorted(pends, jnp.arange(n_blocks, dtype=jnp.int32) * MOE_BLOCK, side='right'),
        N_EXPERTS - 1).astype(jnp.int32)
    h_pad = jnp.concatenate([h, jnp.zeros((1, h.shape[1]), h.dtype)], axis=0)
    xs = h_pad[row_tok].reshape(n_blocks, MOE_BLOCK, h.shape[1])

    def expert_block(args):
        xb, e = args
        gu = xb @ w_gu[e] + b_gu[e]
        glu = jnp.minimum(gu[:, ::2], SWIGLU_LIMIT)
        lin = jnp.clip(gu[:, 1::2], -SWIGLU_LIMIT, SWIGLU_LIMIT)
        act = glu * jax.nn.sigmoid(SWIGLU_ALPHA * glu) * (lin + 1.0)
        return act @ w_dn[e] + b_dn[e]

    ys = lax.map(expert_block, (xs, block_expert)).reshape(n_rows, h.shape[1])
    ys = ys * row_gate.astype(ys.dtype)[:, None]
    return jax.ops.segment_sum(ys, row_tok, num_segments=T + 1)[:T]


def encoder_layer(x, attn_norm, w_in, mla_q_norm, mla_w_uq, mla_kv_norm, mla_w_ukv,
                  gla_w_gate_fwd, gla_b_gate_fwd, gla_w_gate_bwd, gla_b_gate_bwd, gla_out_norm,
                  w_out, ffn_norm, router_w, router_b, w_gu, b_gu, w_dn, b_dn):
    B, S, D = x.shape
    h = rms_norm(x, attn_norm)
    proj = h @ w_in
    split_points = list(np.cumsum(IN_SIZES)[:-1])
    c_q, c_kv, k_r, g_q, g_k, g_v, g_lr, g_r = jnp.split(proj, split_points, axis=-1)

    q = (rms_norm(c_q, mla_q_norm) @ mla_w_uq).reshape(B, S, MLA_HEADS, MLA_NOPE + MLA_ROPE)
    kv = (rms_norm(c_kv, mla_kv_norm) @ mla_w_ukv).reshape(B, S, MLA_HEADS, MLA_NOPE + MLA_V)
    q_nope, q_rope = q[..., :MLA_NOPE], q[..., MLA_NOPE:]
    k_nope, v = kv[..., :MLA_NOPE], kv[..., MLA_NOPE:]
    cos, sin = rope_tables(S)
    q_rope = apply_rope(q_rope, cos[None, :, None, :], sin[None, :, None, :])
    k_rope = apply_rope(k_r, cos[None], sin[None])
    scale = (MLA_NOPE + MLA_ROPE) ** -0.5
    mla_out = mla_attention(q_nope * scale, q_rope * scale, k_nope, k_rope, v)

    gq = g_q.reshape(B, S, GLA_HEADS, GLA_DK) * (GLA_DK ** -0.5)
    gk = g_k.reshape(B, S, GLA_HEADS, GLA_DK)
    gv = g_v.reshape(B, S, GLA_HEADS, GLA_DV)
    lr_f, lr_b = jnp.split(g_lr, 2, axis=-1)
    log_a_f = (jax.nn.log_sigmoid((lr_f @ gla_w_gate_fwd + gla_b_gate_fwd).astype(F32))
               / GLA_GATE_NORM).reshape(B, S, GLA_HEADS, GLA_DK)
    log_a_b = (jax.nn.log_sigmoid((lr_b @ gla_w_gate_bwd + gla_b_gate_bwd).astype(F32))
               / GLA_GATE_NORM).reshape(B, S, GLA_HEADS, GLA_DK)
    o_fwd = gla_chunked(gq, gk, gv, log_a_f, strict=False)
    o_bwd = jnp.flip(gla_chunked(jnp.flip(gq, 1), jnp.flip(gk, 1), jnp.flip(gv, 1),
                                 jnp.flip(log_a_b, 1), strict=True), 1)
    o = rms_norm(o_fwd + o_bwd, gla_out_norm) * jax.nn.silu(
        g_r.astype(F32)).reshape(B, S, GLA_HEADS, GLA_DV)
    gla_out = o.reshape(B, S, GLA_WIDTH).astype(x.dtype)

    x = x + jnp.concatenate([mla_out, gla_out], axis=-1) @ w_out

    h2 = rms_norm(x, ffn_norm).reshape(B * S, D)
    x = x + moe_ffn(h2, router_w, router_b, w_gu, b_gu, w_dn, b_dn).reshape(B, S, D)
    return x


def encoder_trunk(x, layer_params, final_norm):
    for l in range(DEPTH):
        x = encoder_layer(x, *[p[l] for p in layer_params])
    return rms_norm(x, final_norm)


def setup_inputs(seed: int = 0) -> dict:
    key = jax.random.key(seed)
    ks = jax.random.split(key, 24)
    L = DEPTH

    def nrm(k, shape, scale):
        return jax.random.normal(k, shape, F32) * scale

    def gain(k, n):
        return 1.0 + 0.02 * jax.random.normal(k, (L, n), F32)

    return {
        "x_prompt": jax.random.normal(ks[0], (BATCH, SEQ, D_MODEL), F32),
        "x_sample": jax.random.normal(ks[1], (DEC_BATCH, DEC_SEQ, D_MODEL), F32),
        "attn_norm": gain(ks[2], D_MODEL),
        "w_in": nrm(ks[3], (L, D_MODEL, IN_COLS), D_MODEL ** -0.5),
        "mla_q_norm": gain(ks[4], MLA_Q_LORA),
        "mla_w_uq": nrm(ks[5], (L, MLA_Q_LORA, MLA_HEADS * (MLA_NOPE + MLA_ROPE)), MLA_Q_LORA ** -0.5),
        "mla_kv_norm": gain(ks[6], MLA_KV_LORA),
        "mla_w_ukv": nrm(ks[7], (L, MLA_KV_LORA, MLA_HEADS * (MLA_NOPE + MLA_V)), MLA_KV_LORA ** -0.5),
        "gla_w_gate_fwd": nrm(ks[8], (L, GLA_GATE_RANK, GLA_KDIM), GLA_GATE_RANK ** -0.5),
        "gla_b_gate_fwd": nrm(ks[9], (L, GLA_KDIM), 0.1),
        "gla_w_gate_bwd": nrm(ks[10], (L, GLA_GATE_RANK, GLA_KDIM), GLA_GATE_RANK ** -0.5),
        "gla_b_gate_bwd": nrm(ks[11], (L, GLA_KDIM), 0.1),
        "gla_out_norm": gain(ks[12], GLA_DV),
        "w_out": nrm(ks[13], (L, MIX_WIDTH, D_MODEL), MIX_WIDTH ** -0.5),
        "ffn_norm": gain(ks[14], D_MODEL),
        "router_w": nrm(ks[15], (L, D_MODEL, N_EXPERTS), D_MODEL ** -0.5),
        "router_b": nrm(ks[16], (L, N_EXPERTS), 0.01),
        "expert_w_gate_up": nrm(ks[17], (L, N_EXPERTS, D_MODEL, 2 * D_FF), D_MODEL ** -0.5),
        "expert_b_gate_up": nrm(ks[18], (L, N_EXPERTS, 2 * D_FF), 0.02),
        "expert_w_down": nrm(ks[19], (L, N_EXPERTS, D_FF, D_MODEL), D_FF ** -0.5),
        "expert_b_down": nrm(ks[20], (L, N_EXPERTS, D_MODEL), 0.02),
        "final_norm": 1.0 + 0.02 * jax.random.normal(ks[21], (D_MODEL,), F32),
    }


def reference(x_prompt, x_sample, attn_norm, w_in, mla_q_norm, mla_w_uq, mla_kv_norm, mla_w_ukv,
              gla_w_gate_fwd, gla_b_gate_fwd, gla_w_gate_bwd, gla_b_gate_bwd, gla_out_norm,
              w_out, ffn_norm, router_w, router_b, expert_w_gate_up, expert_b_gate_up,
              expert_w_down, expert_b_down, final_norm):
    layer_params = (attn_norm, w_in, mla_q_norm, mla_w_uq, mla_kv_norm, mla_w_ukv,
                    gla_w_gate_fwd, gla_b_gate_fwd, gla_w_gate_bwd, gla_b_gate_bwd, gla_out_norm,
                    w_out, ffn_norm, router_w, router_b, expert_w_gate_up, expert_b_gate_up,
                    expert_w_down, expert_b_down)
    y_prompt = encoder_trunk(x_prompt, layer_params, final_norm)
    y_sample = encoder_trunk(x_sample, layer_params, final_norm)
    return (y_prompt, y_sample)
```

```python
import functools

import jax
import jax.numpy as jnp
import numpy as np
from jax import lax
from jax.experimental import pallas as pl
from jax.experimental.pallas import tpu as pltpu

F32 = jnp.float32
BF16 = jnp.bfloat16

D_MODEL = 1024
MLA_HEADS = 8
MLA_NOPE = 64
MLA_ROPE = 32
MLA_V = 64
MLA_Q_LORA = 384
MLA_KV_LORA = 256
ROPE_THETA = 10000.0
GLA_HEADS = 4
GLA_DK = 64
GLA_DV = 128
GLA_GATE_RANK = 16
GLA_GATE_NORM = 16.0
N_EXPERTS = 32
TOP_K = 4
D_FF = 1024
SWIGLU_LIMIT = 7.0
SWIGLU_ALPHA = 1.702
MOE_BLOCK = 128
RMS_EPS = 1e-6

MLA_WIDTH = MLA_HEADS * MLA_V
GLA_WIDTH = GLA_HEADS * GLA_DV
GLA_KDIM = GLA_HEADS * GLA_DK
HALF_ROPE = MLA_ROPE // 2
QK_PAD = 128
LANES = 128
V_ONES_ROWS = 16

OFF_CQ = 0
OFF_CKV = OFF_CQ + MLA_Q_LORA
OFF_KR = OFF_CKV + MLA_KV_LORA
OFF_GQ = OFF_KR + LANES
OFF_GK = OFF_GQ + GLA_KDIM
OFF_GV = OFF_GK + GLA_KDIM
OFF_LR = OFF_GV + GLA_WIDTH
OFF_GR = OFF_LR + LANES
PROJ_COLS = OFF_GR + GLA_WIDTH

GLA_TILE = 256
GLA_LEVELS = 8
VMEM_LIMIT = 56 * 1024 * 1024

NT_DIMS = (((1,), (1,)), ((), ()))
TN_DIMS = (((0,), (0,)), ((), ()))


def _cparams(sem):
    return pltpu.CompilerParams(dimension_semantics=sem, vmem_limit_bytes=VMEM_LIMIT)


def _rms(x, gain):
    return x * lax.rsqrt(jnp.mean(x * x, axis=-1, keepdims=True) + RMS_EPS) * gain


def _split_bf16(x):
    hi = x.astype(BF16)
    lo = (x - hi.astype(F32)).astype(BF16)
    return hi, lo


def _pre_kernel(x_ref, an_ref, win_ref, qn_ref, wuqT_ref, kvn_ref, wk_ref, wvT_ref,
                wgf_ref, bgf_ref, wgb_ref, bgb_ref, rc_ref, rs1_ref, rs2_ref, cosT_ref, sinT_ref,
                qT_ref, k_ref, vT_ref, gq_ref, gk_ref, gv_ref, laf_ref, lab_ref, gr_ref):
    x = x_ref[...]
    h = _rms(x, an_ref[...]).astype(BF16)
    proj = jnp.dot(h, win_ref[...], preferred_element_type=F32)

    cqn = _rms(proj[:, OFF_CQ:OFF_CQ + MLA_Q_LORA], qn_ref[...]).astype(BF16)
    ckvn = _rms(proj[:, OFF_CKV:OFF_CKV + MLA_KV_LORA], kvn_ref[...]).astype(BF16)

    scale = (MLA_NOPE + MLA_ROPE) ** -0.5
    qT = lax.dot_general(wuqT_ref[...], cqn, NT_DIMS, preferred_element_type=F32) * scale
    n_nope = MLA_HEADS * MLA_NOPE
    n_half = MLA_HEADS * HALF_ROPE
    x1 = qT[n_nope:n_nope + n_half]
    x2 = qT[n_nope + n_half:n_nope + 2 * n_half]
    c = cosT_ref[...]
    s = sinT_ref[...]
    x1r = x1 * c - x2 * s
    x2r = x1 * s + x2 * c
    zpad = jnp.zeros((QK_PAD - MLA_NOPE - MLA_ROPE, qT.shape[1]), BF16)
    for hd in range(MLA_HEADS):
        qT_ref[hd, 0:MLA_NOPE, :] = qT[hd * MLA_NOPE:(hd + 1) * MLA_NOPE].astype(BF16)
        qT_ref[hd, MLA_NOPE:MLA_NOPE + HALF_ROPE, :] = x1r[hd * HALF_ROPE:(hd + 1) * HALF_ROPE].astype(BF16)
        qT_ref[hd, MLA_NOPE + HALF_ROPE:MLA_NOPE + MLA_ROPE, :] = (
            x2r[hd * HALF_ROPE:(hd + 1) * HALF_ROPE].astype(BF16))
        qT_ref[hd, MLA_NOPE + MLA_ROPE:QK_PAD, :] = zpad

    kr = proj[:, OFF_KR:OFF_KR + LANES]
    kr = (kr * rc_ref[...]
          + pltpu.roll(kr, LANES - HALF_ROPE, axis=1) * rs1_ref[...]
          + pltpu.roll(kr, HALF_ROPE, axis=1) * rs2_ref[...])
    kfull = jnp.dot(ckvn, wk_ref[...], preferred_element_type=F32)
    for hd in range(MLA_HEADS):
        k_ref[hd] = (kfull[:, hd * QK_PAD:(hd + 1) * QK_PAD] + kr).astype(BF16)

    vT = lax.dot_general(wvT_ref[...], ckvn, NT_DIMS, preferred_element_type=F32)
    for hd in range(MLA_HEADS):
        vT_ref[hd] = vT[hd * MLA_V:(hd + 1) * MLA_V].astype(BF16)

    gq_ref[...] = proj[:, OFF_GQ:OFF_GQ + GLA_KDIM] * (GLA_DK ** -0.5)
    gk_ref[...] = proj[:, OFF_GK:OFF_GK + GLA_KDIM]
    gv_ref[...] = proj[:, OFF_GV:OFF_GV + GLA_WIDTH]
    gr_ref[...] = proj[:, OFF_GR:OFF_GR + GLA_WIDTH]
    lr = proj[:, OFF_LR:OFF_LR + LANES].astype(BF16)

    def log_decay(w_ref, b_ref):
        z = jnp.dot(lr, w_ref[...], preferred_element_type=F32) + b_ref[...]
        return (jnp.minimum(z, 0.0) - jnp.log1p(jnp.exp(-jnp.abs(z)))) * (1.0 / GLA_GATE_NORM)

    laf_ref[...] = log_decay(wgf_ref, bgf_ref)
    lab_ref[...] = log_decay(wgb_ref, bgb_ref)


def _pre_call(x, prm, rope, tm):
    B, S, D = x.shape
    nS = S // tm
    H = MLA_HEADS

    def full(a):
        nd = a.ndim
        return pl.BlockSpec(a.shape, lambda b, i, _nd=nd: (0,) * _nd)

    tok = lambda w: pl.BlockSpec((None, tm, w), lambda b, i: (b, i, 0))
    in_specs = [
        tok(D), full(prm["attn_norm"]), full(prm["w_in"]), full(prm["q_norm"]), full(prm["w_uqT"]),
        full(prm["kv_norm"]), full(prm["w_k"]), full(prm["w_vT"]),
        full(prm["w_gf"]), full(prm["b_gf"]), full(prm["w_gb"]), full(prm["b_gb"]),
        pl.BlockSpec((tm, LANES), lambda b, i: (i, 0)),
        pl.BlockSpec((tm, LANES), lambda b, i: (i, 0)),
        pl.BlockSpec((tm, LANES), lambda b, i: (i, 0)),
        pl.BlockSpec((H * HALF_ROPE, tm), lambda b, i: (0, i)),
        pl.BlockSpec((H * HALF_ROPE, tm), lambda b, i: (0, i)),
    ]
    out_shape = [
        jax.ShapeDtypeStruct((B, H, QK_PAD, S), BF16),
        jax.ShapeDtypeStruct((B, H, S, QK_PAD), BF16),
        jax.ShapeDtypeStruct((B, H, nS, MLA_V, tm), BF16),
        jax.ShapeDtypeStruct((B, S, GLA_KDIM), F32),
        jax.ShapeDtypeStruct((B, S, GLA_KDIM), F32),
        jax.ShapeDtypeStruct((B, S, GLA_WIDTH), F32),
        jax.ShapeDtypeStruct((B, S, GLA_KDIM), F32),
        jax.ShapeDtypeStruct((B, S, GLA_KDIM), F32),
        jax.ShapeDtypeStruct((B, S, GLA_WIDTH), F32),
    ]
    out_specs = [
        pl.BlockSpec((None, H, QK_PAD, tm), lambda b, i: (b, 0, 0, i)),
        pl.BlockSpec((None, H, tm, QK_PAD), lambda b, i: (b, 0, i, 0)),
        pl.BlockSpec((None, H, None, MLA_V, tm), lambda b, i: (b, 0, i, 0, 0)),
        tok(GLA_KDIM), tok(GLA_KDIM), tok(GLA_WIDTH), tok(GLA_KDIM), tok(GLA_KDIM), tok(GLA_WIDTH),
    ]
    return pl.pallas_call(
        _pre_kernel, grid=(B, nS), in_specs=in_specs, out_specs=out_specs, out_shape=out_shape,
        compiler_params=_cparams(("parallel", "parallel")), name="pre",
    )(x, prm["attn_norm"], prm["w_in"], prm["q_norm"], prm["w_uqT"], prm["kv_norm"], prm["w_k"],
      prm["w_vT"], prm["w_gf"], prm["b_gf"], prm["w_gb"], prm["b_gb"],
      rope["c"], rope["s1"], rope["s2"], rope["cosT"], rope["sinT"])


def _attn_kernel(qT_ref, k_ref, vT_ref, o_ref, *, n_kblk):
    qT = qT_ref[...]
    tq = qT.shape[1]
    tk = k_ref.shape[0] // n_kblk
    ones = jnp.ones((V_ONES_ROWS, tk), BF16)

    def body(j, carry):
        m, acc = carry
        kb = k_ref[pl.ds(pl.multiple_of(j * tk, tk), tk), :]
        sT = jnp.dot(kb, qT, preferred_element_type=F32)
        m_new = jnp.maximum(m, jnp.max(sT, axis=0, keepdims=True))
        p = jnp.exp(sT - m_new).astype(BF16)
        alpha = jnp.exp(m - m_new)
        vext = jnp.concatenate([vT_ref[j], ones], axis=0)
        acc = alpha * acc + jnp.dot(vext, p, preferred_element_type=F32)
        return m_new, acc

    m0 = jnp.full((1, tq), -jnp.inf, F32)
    acc0 = jnp.zeros((MLA_V + V_ONES_ROWS, tq), F32)
    _, acc = lax.fori_loop(0, n_kblk, body, (m0, acc0))
    o_ref[...] = (acc[0:MLA_V] / acc[MLA_V:MLA_V + 1]).astype(o_ref.dtype)


def _attn_call(qT, k, vT, tq):
    B, H, _, S = qT.shape
    n_kblk, tk = vT.shape[2], vT.shape[4]
    return pl.pallas_call(
        functools.partial(_attn_kernel, n_kblk=n_kblk),
        grid=(B, H, S // tq),
        in_specs=[
            pl.BlockSpec((None, None, QK_PAD, tq), lambda b, h, i: (b, h, 0, i)),
            pl.BlockSpec((None, None, S, QK_PAD), lambda b, h, i: (b, h, 0, 0)),
            pl.BlockSpec((None, None, n_kblk, MLA_V, tk), lambda b, h, i: (b, h, 0, 0, 0)),
        ],
        out_specs=pl.BlockSpec((None, None, MLA_V, tq), lambda b, h, i: (b, h, 0, i)),
        out_shape=jax.ShapeDtypeStruct((B, H, MLA_V, S), BF16),
        compiler_params=_cparams(("parallel", "parallel", "parallel")), name="attn",
    )(qT, k, vT)


def _gla_consts():
    L = GLA_TILE
    i = np.arange(L)[:, None]
    j = np.arange(L)[None, :]
    mats = []
    for lvl in range(1, GLA_LEVELS + 1):
        m = 1 << lvl
        mats.append(((i // m == j // m) & (j <= i)).astype(np.float32))
    for lvl in range(1, GLA_LEVELS + 1):
        m = 1 << lvl
        mats.append((i // m == j // m).astype(np.float32))
    x = i ^ j
    lidx = np.where(x > 0, np.floor(np.log2(np.maximum(x, 1))), -1).astype(np.int32)
    lidx_f = np.where(i > j, lidx, -1).astype(np.int32)
    lidx_b = np.where(i < j, lidx, -1).astype(np.int32)
    hd = np.arange(GLA_KDIM)[:, None] // GLA_DK
    hv = np.arange(GLA_WIDTH)[None, :] // GLA_DV
    bexp = (hd == hv).astype(np.float32)
    return (jnp.asarray(np.stack(mats), BF16), jnp.asarray(lidx_f), jnp.asarray(lidx_b),
            jnp.asarray(bexp, BF16))


def _gla_direction(q, k, v, la, pq_ref, lidx, bexp_ref, ssum_ref, state_ref, o_ref, forward):
    L = GLA_TILE
    la_hi, la_lo = _split_bf16(la)
    vb = v.astype(BF16)
    lane_head = lax.broadcasted_iota(jnp.int32, (1, GLA_KDIM), 1) // GLA_DK
    head_masks = [(lane_head == h).astype(BF16) for h in range(GLA_HEADS)]

    def seg_sums(lvl):
        if lvl == 0:
            return la, la
        P = pq_ref[lvl - 1]
        Q = pq_ref[GLA_LEVELS + lvl - 1]
        c = jnp.dot(P, la_hi, preferred_element_type=F32) + jnp.dot(P, la_lo, preferred_element_type=F32)
        t = jnp.dot(Q, la_hi, preferred_element_type=F32) + jnp.dot(Q, la_lo, preferred_element_type=F32)
        return c, t

    ssum_ref[...] = jnp.zeros_like(ssum_ref)
    for lvl in range(GLA_LEVELS):
        c, t = seg_sums(lvl)
        if forward:
            eq, ek = c, t - c
        else:
            eq, ek = t - c + la, c - la
        ql = (q * jnp.exp(eq)).astype(BF16)
        kl = (k * jnp.exp(ek)).astype(BF16)
        sel = lidx == lvl
        for h in range(GLA_HEADS):
            sc = lax.dot_general(ql * head_masks[h], kl, NT_DIMS, preferred_element_type=F32)
            ssum_ref[h] = jnp.where(sel, sc, ssum_ref[h])

    c, t = seg_sums(GLA_LEVELS)
    if forward:
        eq, ek = c, t - c
    else:
        eq, ek = t - c + la, c - la
    q_in = (q * jnp.exp(eq)).astype(BF16)
    k_out = (k * jnp.exp(ek)).astype(BF16)
    o = jnp.dot(q_in, state_ref[...].astype(BF16), preferred_element_type=F32)
    if forward:
        o = o + jnp.dot((q * k).astype(BF16), bexp_ref[...], preferred_element_type=F32) * v
    for h in range(GLA_HEADS):
        oh = jnp.dot(ssum_ref[h].astype(BF16), vb[:, h * GLA_DV:(h + 1) * GLA_DV],
                     preferred_element_type=F32)
        o_ref[:, h * GLA_DV:(h + 1) * GLA_DV] = o[:, h * GLA_DV:(h + 1) * GLA_DV] + oh

    ones = jnp.ones((L, LANES), BF16)
    tot_col = (lax.dot_general(la_hi, ones, TN_DIMS, preferred_element_type=F32)
               + lax.dot_general(la_lo, ones, TN_DIMS, preferred_element_type=F32))
    dec = jnp.exp(tot_col)
    upd = lax.dot_general(k_out, vb, TN_DIMS, preferred_element_type=F32)
    for h in range(GLA_HEADS):
        cols = slice(h * GLA_DV, (h + 1) * GLA_DV)
        state_ref[:, cols] = dec * state_ref[:, cols] + upd[:, cols] * bexp_ref[:, cols].astype(F32)


def _gla_kernel(qf_ref, kf_ref, vf_ref, laf_ref, qb_ref, kb_ref, vb_ref, lab_ref,
                pq_ref, lidxf_ref, lidxb_ref, bexp_ref, of_ref, ob_ref, sf_ref, sb_ref, ssum_ref):
    @pl.when(pl.program_id(1) == 0)
    def _():
        sf_ref[...] = jnp.zeros_like(sf_ref)
        sb_ref[...] = jnp.zeros_like(sb_ref)

    _gla_direction(qf_ref[...], kf_ref[...], vf_ref[...], laf_ref[...], pq_ref, lidxf_ref[...],
                   bexp_ref, ssum_ref, sf_ref, of_ref, True)
    _gla_direction(qb_ref[...], kb_ref[...], vb_ref[...], lab_ref[...], pq_ref, lidxb_ref[...],
                   bexp_ref, ssum_ref, sb_ref, ob_ref, False)


def _gla_call(gq, gk, gv, laf, lab):
    B, S, _ = gq.shape
    L = GLA_TILE
    n = S // L
    pq, lidx_f, lidx_b, bexp = _gla_consts()
    fwd = lambda w: pl.BlockSpec((None, L, w), lambda b, i: (b, i, 0))
    bwd = lambda w: pl.BlockSpec((None, L, w), lambda b, i: (b, n - 1 - i, 0))
    const = lambda a: pl.BlockSpec(a.shape, lambda b, i, _nd=a.ndim: (0,) * _nd)
    return pl.pallas_call(
        _gla_kernel, grid=(B, n),
        in_specs=[fwd(GLA_KDIM), fwd(GLA_KDIM), fwd(GLA_WIDTH), fwd(GLA_KDIM),
                  bwd(GLA_KDIM), bwd(GLA_KDIM), bwd(GLA_WIDTH), bwd(GLA_KDIM),
                  const(pq), const(lidx_f), const(lidx_b), const(bexp)],
        out_specs=[fwd(GLA_WIDTH), bwd(GLA_WIDTH)],
        out_shape=[jax.ShapeDtypeStruct((B, S, GLA_WIDTH), F32)] * 2,
        scratch_shapes=[pltpu.VMEM((GLA_KDIM, GLA_WIDTH), F32), pltpu.VMEM((GLA_KDIM, GLA_WIDTH), F32),
                        pltpu.VMEM((GLA_HEADS, L, L), F32)],
        compiler_params=_cparams(("parallel", "arbitrary")), name="gla",
    )(gq, gk, gv, laf, gq, gk, gv, lab, pq, lidx_f, lidx_b, bexp)


def _post_kernel(x_ref, oT_ref, of_ref, ob_ref, gr_ref, gon_ref, woa_ref, wob_ref, fn_ref,
                 rwT_ref, rb_ref, upper_ref,
                 xmid_ref, h2_ref, meta_ref, gates_ref, counts_ref, carry_ref):
    first = (pl.program_id(0) == 0) & (pl.program_id(1) == 0)

    @pl.when(first)
    def _():
        carry_ref[...] = jnp.zeros_like(carry_ref)

    o = of_ref[...] + ob_ref[...]
    gr = gr_ref[...]
    parts = []
    for h in range(GLA_HEADS):
        cols = slice(h * GLA_DV, (h + 1) * GLA_DV)
        parts.append(_rms(o[:, cols], gon_ref[...]) * jax.nn.silu(gr[:, cols]))
    gla = jnp.concatenate(parts, axis=1).astype(BF16)

    x_mid = (x_ref[...]
             + lax.dot_general(oT_ref[...], woa_ref[...], TN_DIMS, preferred_element_type=F32)
             + jnp.dot(gla, wob_ref[...], preferred_element_type=F32))
    xmid_ref[...] = x_mid
    h2 = _rms(x_mid, fn_ref[...])
    h2_ref[...] = h2

    h_hi, h_lo = _split_bf16(h2)
    w_hi, w_lo = _split_bf16(rwT_ref[...])
    logits = (lax.dot_general(w_hi, h_hi, NT_DIMS, preferred_element_type=F32)
              + lax.dot_general(w_hi, h_lo, NT_DIMS, preferred_element_type=F32)
              + lax.dot_general(w_lo, h_hi, NT_DIMS, preferred_element_type=F32)
              + rb_ref[...])
    tm = logits.shape[1]
    eidx = lax.broadcasted_iota(jnp.int32, (N_EXPERTS, tm), 0).astype(F32)
    vals, idxs, sels = [], [], []
    cur = logits
    for _ in range(TOP_K):
        mk = jnp.max(cur, axis=0, keepdims=True)
        ik = jnp.min(jnp.where(cur == mk, eidx, float(N_EXPERTS)), axis=0, keepdims=True)
        sel = eidx == ik
        vals.append(mk)
        idxs.append(ik)
        sels.append(sel)
        cur = jnp.where(sel, -jnp.inf, cur)
    exps = [jnp.exp(vk - vals[0]) for vk in vals]
    denom = exps[0] + exps[1] + exps[2] + exps[3]
    gates = [e / denom for e in exps]

    cnt = (sels[0] | sels[1] | sels[2] | sels[3])
    before = jnp.dot(cnt.astype(BF16), upper_ref[...], preferred_element_type=F32) + carry_ref[:, 0:1]
    ranks = [jnp.sum(jnp.where(sel, before, 0.0), axis=0, keepdims=True) for sel in sels]
    carry_ref[...] = carry_ref[...] + jnp.sum(cnt.astype(F32), axis=1, keepdims=True)
    counts_ref[...] = carry_ref[...]

    meta_ref[...] = jnp.concatenate(idxs + ranks, axis=0).astype(jnp.int32)
    gates_ref[...] = jnp.concatenate(gates + [jnp.zeros((TOP_K, tm), F32)], axis=0)


def _post_call(x, oT, o_f, o_b, gr, prm, tm):
    B, S, D = x.shape
    nS = S // tm
    upper = jnp.asarray(np.triu(np.ones((tm, tm), np.float32), k=1), BF16)
    full = lambda a: pl.BlockSpec(a.shape, lambda b, i, _nd=a.ndim: (0,) * _nd)
    tok = lambda w: pl.BlockSpec((None, tm, w), lambda b, i: (b, i, 0))
    colblk = lambda r: pl.BlockSpec((None, r, tm), lambda b, i: (b, 0, i))
    return pl.pallas_call(
        _post_kernel, grid=(B, nS),
        in_specs=[tok(D), colblk(MLA_WIDTH), tok(GLA_WIDTH), tok(GLA_WIDTH), tok(GLA_WIDTH),
                  full(prm["gla_out_norm"]), full(prm["w_out_a"]), full(prm["w_out_b"]),
                  full(prm["ffn_norm"]), full(prm["router_wT"]), full(prm["router_b"]), full(upper)],
        out_specs=[tok(D), tok(D), colblk(2 * TOP_K), colblk(2 * TOP_K),
                   pl.BlockSpec((N_EXPERTS, LANES), lambda b, i: (0, 0))],
        out_shape=[jax.ShapeDtypeStruct((B, S, D), F32), jax.ShapeDtypeStruct((B, S, D), F32),
                   jax.ShapeDtypeStruct((B, 2 * TOP_K, S), jnp.int32),
                   jax.ShapeDtypeStruct((B, 2 * TOP_K, S), F32),
                   jax.ShapeDtypeStruct((N_EXPERTS, LANES), F32)],
        scratch_shapes=[pltpu.VMEM((N_EXPERTS, LANES), F32)],
        compiler_params=_cparams(("arbitrary", "arbitrary")), name="post",
    )(x, oT, o_f, o_b, gr, prm["gla_out_norm"], prm["w_out_a"], prm["w_out_b"], prm["ffn_norm"],
      prm["router_wT"], prm["router_b"], upper)


def _dispatch_kernel(dest_ref, h2_ref, xs_in_ref, xs_ref, sem):
    del xs_in_ref
    td = h2_ref.shape[0]

    def row_copy(r, kk):
        return pltpu.make_async_copy(h2_ref.at[pl.ds(r, 1), :],
                                     xs_ref.at[pl.ds(dest_ref[kk, r], 1), :], sem)

    def start(r, c):
        for kk in range(TOP_K):
            row_copy(r, kk).start()
        return c

    def wait(r, c):
        for kk in range(TOP_K):
            row_copy(r, kk).wait()
        return c

    lax.fori_loop(0, td, start, 0)
    lax.fori_loop(0, td, wait, 0)


def _dispatch_call(dest, h2, n_rows, td):
    T, D = h2.shape
    xs0 = jnp.zeros((n_rows, D), F32)
    return pl.pallas_call(
        _dispatch_kernel, grid=(T // td,),
        in_specs=[pl.BlockSpec((TOP_K, td), lambda i: (0, i), memory_space=pltpu.SMEM),
                  pl.BlockSpec((td, D), lambda i: (i, 0)),
                  pl.BlockSpec(memory_space=pl.ANY)],
        out_specs=pl.BlockSpec(memory_space=pl.ANY),
        out_shape=jax.ShapeDtypeStruct((n_rows, D), F32),
        scratch_shapes=[pltpu.SemaphoreType.DMA(())],
        input_output_aliases={2: 0},
        compiler_params=_cparams(("arbitrary",)), name="dispatch",
    )(dest, h2, xs0)


def _expert_kernel(be_ref, nv_ref, xs_ref, wg_ref, wl_ref, bg_ref, bl_ref, wd_ref, bd_ref, ys_ref):
    del be_ref

    @pl.when(pl.program_id(0) < nv_ref[0])
    def _():
        xb = xs_ref[...].astype(BF16)
        g = jnp.dot(xb, wg_ref[...], preferred_element_type=F32) + bg_ref[...]
        l = jnp.dot(xb, wl_ref[...], preferred_element_type=F32) + bl_ref[...]
        glu = jnp.minimum(g, SWIGLU_LIMIT)
        lin = jnp.clip(l, -SWIGLU_LIMIT, SWIGLU_LIMIT)
        act = glu * jax.nn.sigmoid(SWIGLU_ALPHA * glu) * (lin + 1.0)
        ys_ref[...] = jnp.dot(act.astype(BF16), wd_ref[...], preferred_element_type=F32) + bd_ref[...]

    @pl.when(pl.program_id(0) >= nv_ref[0])
    def _():
        ys_ref[...] = jnp.zeros_like(ys_ref)


def _expert_call(block_expert, n_valid, xs, prm):
    n_rows, D = xs.shape
    n_blocks = n_rows // MOE_BLOCK
    rows = lambda i, be, nv: (jnp.minimum(i, nv[0] - 1), 0)
    wsel = lambda i, be, nv: (be[i], 0, 0)
    grid_spec = pltpu.PrefetchScalarGridSpec(
        num_scalar_prefetch=2, grid=(n_blocks,),
        in_specs=[pl.BlockSpec((MOE_BLOCK, D), rows),
                  pl.BlockSpec((None, D, D_FF), wsel), pl.BlockSpec((None, D, D_FF), wsel),
                  pl.BlockSpec((None, 1, D_FF), wsel), pl.BlockSpec((None, 1, D_FF), wsel),
                  pl.BlockSpec((None, D_FF, D), wsel), pl.BlockSpec((None, 1, D), wsel)],
        out_specs=pl.BlockSpec((MOE_BLOCK, D), lambda i, be, nv: (i, 0)))
    return pl.pallas_call(
        _expert_kernel, grid_spec=grid_spec,
        out_shape=jax.ShapeDtypeStruct((n_rows, D), F32),
        compiler_params=_cparams(("arbitrary",)), name="experts",
    )(block_expert, n_valid, xs, prm["w_glu"], prm["w_lin"], prm["b_glu"], prm["b_lin"],
      prm["w_dn"], prm["b_dn"])


def _combine_kernel(dest_ref, xmid_ref, gates_ref, fnorm_ref, ys_ref, out_ref, buf_ref, sem):
    tc = xmid_ref.shape[0]

    def row_copy(r, kk):
        return pltpu.make_async_copy(ys_ref.at[pl.ds(dest_ref[kk, r], 1), :],
                                     buf_ref.at[kk, pl.ds(r, 1), :], sem)

    def start(r, c):
        for kk in range(TOP_K):
            row_copy(r, kk).start()
        return c

    def wait(r, c):
        for kk in range(TOP_K):
            row_copy(r, kk).wait()
        return c

    lax.fori_loop(0, tc, start, 0)
    g = jnp.transpose(gates_ref[...])
    lax.fori_loop(0, tc, wait, 0)
    y = xmid_ref[...]
    for kk in range(TOP_K):
        y = y + buf_ref[kk] * g[:, kk:kk + 1]
    out_ref[...] = _rms(y, fnorm_ref[...])


def _combine_call(dest, x_mid, gates, final_norm, ys, tc):
    T, D = x_mid.shape
    return pl.pallas_call(
        _combine_kernel, grid=(T // tc,),
        in_specs=[pl.BlockSpec((TOP_K, tc), lambda i: (0, i), memory_space=pltpu.SMEM),
                  pl.BlockSpec((tc, D), lambda i: (i, 0)),
                  pl.BlockSpec((2 * TOP_K, tc), lambda i: (0, i)),
                  pl.BlockSpec(final_norm.shape, lambda i: (0, 0)),
                  pl.BlockSpec(memory_space=pl.ANY)],
        out_specs=pl.BlockSpec((tc, D), lambda i: (i, 0)),
        out_shape=jax.ShapeDtypeStruct((T, D), F32),
        scratch_shapes=[pltpu.VMEM((TOP_K, tc, D), F32), pltpu.SemaphoreType.DMA(())],
        compiler_params=_cparams(("arbitrary",)), name="combine",
    )(dest, x_mid, gates, final_norm, ys)


def _prep_params(attn_norm, w_in, mla_q_norm, mla_w_uq, mla_kv_norm, mla_w_ukv, gla_w_gate_fwd,
                 gla_b_gate_fwd, gla_w_gate_bwd, gla_b_gate_bwd, gla_out_norm, w_out, ffn_norm,
                 router_w, router_b, w_gu, b_gu, w_dn, b_dn):
    D = D_MODEL
    o = np.cumsum((0, MLA_Q_LORA, MLA_KV_LORA, MLA_ROPE, GLA_KDIM, GLA_KDIM, GLA_WIDTH,
                   2 * GLA_GATE_RANK, GLA_WIDTH))
    seg = [w_in[:, o[n]:o[n + 1]] for n in range(8)]
    z = lambda n: jnp.zeros((D, n), w_in.dtype)
    w_in_p = jnp.concatenate(
        [seg[0], seg[1], z(MLA_NOPE), seg[2], z(LANES - MLA_NOPE - MLA_ROPE), seg[3], seg[4], seg[5],
         seg[6], z(LANES - 2 * GLA_GATE_RANK), seg[7]], axis=1).astype(BF16)

    wq = mla_w_uq.reshape(MLA_Q_LORA, MLA_HEADS, MLA_NOPE + MLA_ROPE)
    w_uq_p = jnp.concatenate(
        [wq[:, :, :MLA_NOPE].reshape(MLA_Q_LORA, -1),
         wq[:, :, MLA_NOPE:MLA_NOPE + HALF_ROPE].reshape(MLA_Q_LORA, -1),
         wq[:, :, MLA_NOPE + HALF_ROPE:].reshape(MLA_Q_LORA, -1)], axis=1)
    wkv = mla_w_ukv.reshape(MLA_KV_LORA, MLA_HEADS, MLA_NOPE + MLA_V)
    w_k = jnp.concatenate([wkv[:, :, :MLA_NOPE], jnp.zeros((MLA_KV_LORA, MLA_HEADS, QK_PAD - MLA_NOPE),
                                                            wkv.dtype)], axis=2)
    w_v = wkv[:, :, MLA_NOPE:].reshape(MLA_KV_LORA, -1)

    def gate_w(w, row0):
        full = jnp.zeros((LANES, GLA_KDIM), w.dtype)
        return full.at[row0:row0 + GLA_GATE_RANK].set(w).astype(BF16)

    E = N_EXPERTS
    return {
        "attn_norm": attn_norm.reshape(1, D), "w_in": w_in_p,
        "q_norm": mla_q_norm.reshape(1, -1), "w_uqT": w_uq_p.T.astype(BF16),
        "kv_norm": mla_kv_norm.reshape(1, -1),
        "w_k": w_k.reshape(MLA_KV_LORA, -1).astype(BF16), "w_vT": w_v.T.astype(BF16),
        "w_gf": gate_w(gla_w_gate_fwd, 0), "b_gf": gla_b_gate_fwd.reshape(1, -1),
        "w_gb": gate_w(gla_w_gate_bwd, GLA_GATE_RANK), "b_gb": gla_b_gate_bwd.reshape(1, -1),
        "gla_out_norm": gla_out_norm.reshape(1, -1),
        "w_out_a": w_out[:MLA_WIDTH].astype(BF16), "w_out_b": w_out[MLA_WIDTH:].astype(BF16),
        "ffn_norm": ffn_norm.reshape(1, D),
        "router_wT": router_w.T, "router_b": router_b.reshape(E, 1),
        "w_glu": w_gu[:, :, 0::2].astype(BF16), "w_lin": w_gu[:, :, 1::2].astype(BF16),
        "b_glu": b_gu[:, 0::2].reshape(E, 1, D_FF), "b_lin": b_gu[:, 1::2].reshape(E, 1, D_FF),
        "w_dn": w_dn.astype(BF16), "b_dn": b_dn.reshape(E, 1, D),
    }


def _rope_tables(S):
    inv_freq = jnp.power(ROPE_THETA, -jnp.arange(0, MLA_ROPE, 2, dtype=F32) / MLA_ROPE)
    ang = jnp.arange(S, dtype=F32)[:, None] * inv_freq[None, :]
    cos, sin = jnp.cos(ang), jnp.sin(ang)
    z = lambda n: jnp.zeros((S, n), F32)
    tail = LANES - MLA_NOPE - MLA_ROPE
    return {
        "c": jnp.concatenate([z(MLA_NOPE), cos, cos, z(tail)], axis=1),
        "s1": jnp.concatenate([z(MLA_NOPE), -sin, z(HALF_ROPE), z(tail)], axis=1),
        "s2": jnp.concatenate([z(MLA_NOPE), z(HALF_ROPE), sin, z(tail)], axis=1),
        "cosT": jnp.tile(cos.T, (MLA_HEADS, 1)), "sinT": jnp.tile(sin.T, (MLA_HEADS, 1)),
    }


def _encoder(x, prm, final_norm):
    B, S, D = x.shape
    tm = min(512, S)
    rope = _rope_tables(S)
    qT, k, vT, gq, gk, gv, laf, lab, gr = _pre_call(x, prm, rope, tm)
    oT = _attn_call(qT, k, vT, min(256, S)).reshape(B, MLA_WIDTH, S)
    o_f, o_b = _gla_call(gq, gk, gv, laf, lab)
    x_mid, h2, meta, gates, counts = _post_call(x, oT, o_f, o_b, gr, prm, min(256, S))

    T = B * S
    n_rows = T * TOP_K + N_EXPERTS * MOE_BLOCK
    n_blocks = n_rows // MOE_BLOCK
    cnt = counts[:, 0].astype(jnp.int32)
    padded = ((cnt + MOE_BLOCK - 1) // MOE_BLOCK) * MOE_BLOCK
    pends = jnp.cumsum(padded)
    pstarts = pends - padded
    dest = pstarts[meta[:, :TOP_K, :]] + meta[:, TOP_K:, :]
    dest = jnp.transpose(dest, (1, 0, 2)).reshape(TOP_K, T)
    gates = jnp.transpose(gates, (1, 0, 2)).reshape(2 * TOP_K, T)
    n_valid = (pends[-1] // MOE_BLOCK).astype(jnp.int32).reshape(1)
    blk = jnp.minimum(jnp.arange(n_blocks, dtype=jnp.int32), n_valid[0] - 1) * MOE_BLOCK
    block_expert = jnp.minimum(jnp.searchsorted(pends, blk, side="right"), N_EXPERTS - 1).astype(jnp.int32)

    xs = _dispatch_call(dest, h2.reshape(T, D), n_rows, min(128, S))
    ys = _expert_call(block_expert, n_valid, xs, prm)
    y = _combine_call(dest, x_mid.reshape(T, D), gates, final_norm.reshape(1, D), ys, min(128, S))
    return y.reshape(B, S, D)


def kernel(x_prompt, x_sample, attn_norm, w_in, mla_q_norm, mla_w_uq, mla_kv_norm, mla_w_ukv,
           gla_w_gate_fwd, gla_b_gate_fwd, gla_w_gate_bwd, gla_b_gate_bwd, gla_out_norm, w_out, ffn_norm,
           router_w, router_b, expert_w_gate_up, expert_b_gate_up, expert_w_down, expert_b_down,
           final_norm):
    layer = (attn_norm, w_in, mla_q_norm, mla_w_uq, mla_kv_norm, mla_w_ukv, gla_w_gate_fwd,
             gla_b_gate_fwd, gla_w_gate_bwd, gla_b_gate_bwd, gla_out_norm, w_out, ffn_norm, router_w,
             router_b, expert_w_gate_up, expert_b_gate_up, expert_w_down, expert_b_down)
    assert all(p.shape[0] == 1 for p in layer), "single layer expected"
    assert x_prompt.shape[1:] == x_sample.shape[1:]
    prm = _prep_params(*[p[0] for p in layer])
    nb = x_prompt.shape[0]
    y = _encoder(jnp.concatenate([x_prompt, x_sample], axis=0), prm, final_norm)
    return y[:nb], y[nb:]
```

```python
import functools

import jax
import jax.numpy as jnp
import numpy as np
from jax import lax
from jax.experimental import pallas as pl
from jax.experimental.pallas import tpu as pltpu

F32 = jnp.float32
BF16 = jnp.bfloat16

D_MODEL = 1024
MLA_HEADS = 8
MLA_NOPE = 64
MLA_ROPE = 32
MLA_V = 64
MLA_Q_LORA = 384
MLA_KV_LORA = 256
ROPE_THETA = 10000.0
GLA_HEADS = 4
GLA_DK = 64
GLA_DV = 128
GLA_GATE_RANK = 16
GLA_GATE_NORM = 16.0
N_EXPERTS = 32
TOP_K = 4
D_FF = 1024
SWIGLU_LIMIT = 7.0
SWIGLU_ALPHA = 1.702
MOE_BLOCK = 128
RMS_EPS = 1e-6

MLA_WIDTH = MLA_HEADS * MLA_V
GLA_WIDTH = GLA_HEADS * GLA_DV
GLA_KDIM = GLA_HEADS * GLA_DK
HALF_ROPE = MLA_ROPE // 2
QK_PAD = 128
LANES = 128
V_ONES_ROWS = 16

OFF_CQ = 0
OFF_CKV = OFF_CQ + MLA_Q_LORA
OFF_KR = OFF_CKV + MLA_KV_LORA
OFF_GQ = OFF_KR + LANES
OFF_GK = OFF_GQ + GLA_KDIM
OFF_GV = OFF_GK + GLA_KDIM
OFF_LR = OFF_GV + GLA_WIDTH
OFF_GR = OFF_LR + LANES
PROJ_COLS = OFF_GR + GLA_WIDTH

ATTN_KEY_TILE = 256
ATTN_QUERY_TILE = 512
ATTN_UNROLL = 16
ATTN_SCORE_SLOTS = 4
GLA_TILE = 256
GLA_LEVELS = 8
VMEM_LIMIT = 56 * 1024 * 1024

LOG2_E = 1.4426950408889634
NT_DIMS = (((1,), (1,)), ((), ()))
TN_DIMS = (((0,), (0,)), ((), ()))


def _cparams(sem):
    return pltpu.CompilerParams(dimension_semantics=sem, vmem_limit_bytes=VMEM_LIMIT)


def _rms(x, gain):
    return x * lax.rsqrt(jnp.mean(x * x, axis=-1, keepdims=True) + RMS_EPS) * gain


def _split_bf16(x):
    hi = x.astype(BF16)
    lo = (x - hi.astype(F32)).astype(BF16)
    return hi, lo


def _pre_kernel(x_ref, an_ref, win_ref, qn_ref, wuqT_ref, kvn_ref, wk_ref, wvT_ref,
                wgf_ref, bgf_ref, wgb_ref, bgb_ref, rc_ref, rs1_ref, rs2_ref, cosT_ref, sinT_ref,
                qT_ref, k_ref, vT_ref, gq_ref, gk_ref, gv_ref, laf_ref, lab_ref, gr_ref):
    x = x_ref[...]
    h = _rms(x, an_ref[...]).astype(BF16)
    proj = jnp.dot(h, win_ref[...], preferred_element_type=F32)

    cqn = _rms(proj[:, OFF_CQ:OFF_CQ + MLA_Q_LORA], qn_ref[...]).astype(BF16)
    ckvn = _rms(proj[:, OFF_CKV:OFF_CKV + MLA_KV_LORA], kvn_ref[...]).astype(BF16)

    scale = (MLA_NOPE + MLA_ROPE) ** -0.5 * LOG2_E
    qT = lax.dot_general(wuqT_ref[...], cqn, NT_DIMS, preferred_element_type=F32) * scale
    n_nope = MLA_HEADS * MLA_NOPE
    n_half = MLA_HEADS * HALF_ROPE
    x1 = qT[n_nope:n_nope + n_half]
    x2 = qT[n_nope + n_half:n_nope + 2 * n_half]
    c = cosT_ref[...]
    s = sinT_ref[...]
    x1r = x1 * c - x2 * s
    x2r = x1 * s + x2 * c
    zpad = jnp.zeros((QK_PAD - MLA_NOPE - MLA_ROPE, qT.shape[1]), BF16)
    for hd in range(MLA_HEADS):
        qT_ref[hd, 0:MLA_NOPE, :] = qT[hd * MLA_NOPE:(hd + 1) * MLA_NOPE].astype(BF16)
        qT_ref[hd, MLA_NOPE:MLA_NOPE + HALF_ROPE, :] = x1r[hd * HALF_ROPE:(hd + 1) * HALF_ROPE].astype(BF16)
        qT_ref[hd, MLA_NOPE + HALF_ROPE:MLA_NOPE + MLA_ROPE, :] = (
            x2r[hd * HALF_ROPE:(hd + 1) * HALF_ROPE].astype(BF16))
        qT_ref[hd, MLA_NOPE + MLA_ROPE:QK_PAD, :] = zpad

    kr = proj[:, OFF_KR:OFF_KR + LANES]
    kr = (kr * rc_ref[...]
          + pltpu.roll(kr, LANES - HALF_ROPE, axis=1) * rs1_ref[...]
          + pltpu.roll(kr, HALF_ROPE, axis=1) * rs2_ref[...])
    kfull = jnp.dot(ckvn, wk_ref[...], preferred_element_type=F32)
    for hd in range(MLA_HEADS):
        k_ref[hd] = (kfull[:, hd * QK_PAD:(hd + 1) * QK_PAD] + kr).astype(BF16)

    vT = lax.dot_general(wvT_ref[...], ckvn, NT_DIMS, preferred_element_type=F32)
    for hd in range(MLA_HEADS):
        vT_ref[hd] = vT[hd * MLA_V:(hd + 1) * MLA_V].astype(BF16)

    gq_ref[...] = proj[:, OFF_GQ:OFF_GQ + GLA_KDIM] * (GLA_DK ** -0.5)
    gk_ref[...] = proj[:, OFF_GK:OFF_GK + GLA_KDIM]
    gv_ref[...] = proj[:, OFF_GV:OFF_GV + GLA_WIDTH]
    gr_ref[...] = proj[:, OFF_GR:OFF_GR + GLA_WIDTH]
    lr = proj[:, OFF_LR:OFF_LR + LANES].astype(BF16)

    def log_decay(w_ref, b_ref):
        z = jnp.dot(lr, w_ref[...], preferred_element_type=F32) + b_ref[...]
        return (jnp.minimum(z, 0.0) - jnp.log1p(jnp.exp(-jnp.abs(z)))) * (1.0 / GLA_GATE_NORM)

    laf_ref[...] = log_decay(wgf_ref, bgf_ref)
    lab_ref[...] = log_decay(wgb_ref, bgb_ref)


def _pre_call(x, prm, rope, tm):
    B, S, D = x.shape
    nS = S // tm
    H = MLA_HEADS

    def full(a):
        nd = a.ndim
        return pl.BlockSpec(a.shape, lambda b, i, _nd=nd: (0,) * _nd)

    tok = lambda w: pl.BlockSpec((None, tm, w), lambda b, i: (b, i, 0))
    in_specs = [
        tok(D), full(prm["attn_norm"]), full(prm["w_in"]), full(prm["q_norm"]), full(prm["w_uqT"]),
        full(prm["kv_norm"]), full(prm["w_k"]), full(prm["w_vT"]),
        full(prm["w_gf"]), full(prm["b_gf"]), full(prm["w_gb"]), full(prm["b_gb"]),
        pl.BlockSpec((tm, LANES), lambda b, i: (i, 0)),
        pl.BlockSpec((tm, LANES), lambda b, i: (i, 0)),
        pl.BlockSpec((tm, LANES), lambda b, i: (i, 0)),
        pl.BlockSpec((H * HALF_ROPE, tm), lambda b, i: (0, i)),
        pl.BlockSpec((H * HALF_ROPE, tm), lambda b, i: (0, i)),
    ]
    out_shape = [
        jax.ShapeDtypeStruct((B, H, QK_PAD, S), BF16),
        jax.ShapeDtypeStruct((B, H, S, QK_PAD), BF16),
        jax.ShapeDtypeStruct((B, H, nS, MLA_V, tm), BF16),
        jax.ShapeDtypeStruct((B, S, GLA_KDIM), F32),
        jax.ShapeDtypeStruct((B, S, GLA_KDIM), F32),
        jax.ShapeDtypeStruct((B, S, GLA_WIDTH), F32),
        jax.ShapeDtypeStruct((B, S, GLA_KDIM), F32),
        jax.ShapeDtypeStruct((B, S, GLA_KDIM), F32),
        jax.ShapeDtypeStruct((B, S, GLA_WIDTH), F32),
    ]
    out_specs = [
        pl.BlockSpec((None, H, QK_PAD, tm), lambda b, i: (b, 0, 0, i)),
        pl.BlockSpec((None, H, tm, QK_PAD), lambda b, i: (b, 0, i, 0)),
        pl.BlockSpec((None, H, None, MLA_V, tm), lambda b, i: (b, 0, i, 0, 0)),
        tok(GLA_KDIM), tok(GLA_KDIM), tok(GLA_WIDTH), tok(GLA_KDIM), tok(GLA_KDIM), tok(GLA_WIDTH),
    ]
    return pl.pallas_call(
        _pre_kernel, grid=(B, nS), in_specs=in_specs, out_specs=out_specs, out_shape=out_shape,
        compiler_params=_cparams(("parallel", "parallel")), name="pre",
    )(x, prm["attn_norm"], prm["w_in"], prm["q_norm"], prm["w_uqT"], prm["kv_norm"], prm["w_k"],
      prm["w_vT"], prm["w_gf"], prm["b_gf"], prm["w_gb"], prm["b_gb"],
      rope["c"], rope["s1"], rope["s2"], rope["cosT"], rope["sinT"])


def _attn_kernel(qT_ref, k_ref, vT_ref, o_ref, s_ref, p_ref, *, n_kblk, unroll):
    qT = qT_ref[...]
    tq = qT.shape[1]
    tk = k_ref.shape[0] // n_kblk
    ones = jnp.ones((V_ONES_ROWS, tk), BF16)

    def scores(j, slot):
        kb = k_ref[pl.ds(pl.multiple_of(j * tk, tk), tk), :]
        sT = jnp.dot(kb, qT, preferred_element_type=F32)
        s_ref[slot] = sT
        return jnp.max(sT, axis=0, keepdims=True)

    def values(j, slot):
        vext = jnp.concatenate([vT_ref[j], ones], axis=0)
        return jnp.dot(vext, p_ref[slot], preferred_element_type=F32)

    def softmax(s_slot, p_slot, blk_max, m):
        m_new = jnp.maximum(m, blk_max)
        p_ref[p_slot] = jnp.exp2(s_ref[s_slot] - m_new).astype(BF16)
        return m_new, jnp.exp2(m - m_new)

    def body(jj, carry):
        max_0, max_1, m, alpha_prev, acc = carry
        for u in range(unroll):
            j = unroll * jj + u
            pv = values(jnp.maximum(j - 1, 0), (u + 1) % 2)
            max_2 = scores(jnp.minimum(j + 2, n_kblk - 1), (u + 2) % ATTN_SCORE_SLOTS)
            m, alpha = softmax(u % ATTN_SCORE_SLOTS, u % 2, max_0, m)
            acc = alpha_prev * acc + pv
            max_0, max_1, alpha_prev = max_1, max_2, alpha
        return max_0, max_1, m, alpha_prev, acc

    p_ref[1] = jnp.zeros(p_ref.shape[1:], BF16)
    m0 = jnp.full((1, tq), -jnp.inf, F32)
    acc0 = jnp.zeros((MLA_V + V_ONES_ROWS, tq), F32)
    carry0 = (scores(0, 0), scores(1, 1), m0, jnp.ones((1, tq), F32), acc0)
    _, _, _, alpha, acc = lax.fori_loop(0, n_kblk // unroll, body, carry0)
    acc = alpha * acc + values(n_kblk - 1, 1)
    o_ref[...] = (acc[0:MLA_V] / acc[MLA_V:MLA_V + 1]).astype(o_ref.dtype)


def _attn_call(qT, k, vT, tq):
    B, H, _, S = qT.shape
    n_kblk, tk = vT.shape[2], vT.shape[4]
    unroll = min(ATTN_UNROLL, n_kblk)
    assert unroll % ATTN_SCORE_SLOTS == 0 and n_kblk % unroll == 0 and S % tq == 0
    return pl.pallas_call(
        functools.partial(_attn_kernel, n_kblk=n_kblk, unroll=unroll),
        grid=(B, H, S // tq),
        in_specs=[
            pl.BlockSpec((None, None, QK_PAD, tq), lambda b, h, i: (b, h, 0, i)),
            pl.BlockSpec((None, None, S, QK_PAD), lambda b, h, i: (b, h, 0, 0)),
            pl.BlockSpec((None, None, n_kblk, MLA_V, tk), lambda b, h, i: (b, h, 0, 0, 0)),
        ],
        out_specs=pl.BlockSpec((None, None, MLA_V, tq), lambda b, h, i: (b, h, 0, i)),
        out_shape=jax.ShapeDtypeStruct((B, H, MLA_V, S), BF16),
        scratch_shapes=[pltpu.VMEM((ATTN_SCORE_SLOTS, tk, tq), F32), pltpu.VMEM((2, tk, tq), BF16)],
        compiler_params=_cparams(("parallel", "parallel", "parallel")), name="attn",
    )(qT, k, vT)


def _gla_consts():
    L = GLA_TILE
    i = np.arange(L)[:, None]
    j = np.arange(L)[None, :]
    mats = []
    for lvl in range(1, GLA_LEVELS + 1):
        m = 1 << lvl
        mats.append(((i // m == j // m) & (j <= i)).astype(np.float32))
    for lvl in range(1, GLA_LEVELS + 1):
        m = 1 << lvl
        mats.append((i // m == j // m).astype(np.float32))
    x = i ^ j
    lidx = np.where(x > 0, np.floor(np.log2(np.maximum(x, 1))), -1).astype(np.int32)
    lidx_f = np.where(i > j, lidx, -1).astype(np.int32)
    lidx_b = np.where(i < j, lidx, -1).astype(np.int32)
    hd = np.arange(GLA_KDIM)[:, None] // GLA_DK
    hv = np.arange(GLA_WIDTH)[None, :] // GLA_DV
    bexp = (hd == hv).astype(np.float32)
    return (jnp.asarray(np.stack(mats), BF16), jnp.asarray(lidx_f), jnp.asarray(lidx_b),
            jnp.asarray(bexp, BF16))


def _gla_direction(q, k, v, la, pq_ref, lidx, bexp_ref, ssum_ref, state_ref, o_ref, forward):
    L = GLA_TILE
    la_hi, la_lo = _split_bf16(la)
    vb = v.astype(BF16)
    lane_head = lax.broadcasted_iota(jnp.int32, (1, GLA_KDIM), 1) // GLA_DK
    head_masks = [(lane_head == h).astype(BF16) for h in range(GLA_HEADS)]

    def seg_sums(lvl):
        if lvl == 0:
            return la, la
        P = pq_ref[lvl - 1]
        Q = pq_ref[GLA_LEVELS + lvl - 1]
        c = jnp.dot(P, la_hi, preferred_element_type=F32) + jnp.dot(P, la_lo, preferred_element_type=F32)
        t = jnp.dot(Q, la_hi, preferred_element_type=F32) + jnp.dot(Q, la_lo, preferred_element_type=F32)
        return c, t

    ssum_ref[...] = jnp.zeros_like(ssum_ref)
    for lvl in range(GLA_LEVELS):
        c, t = seg_sums(lvl)
        if forward:
            eq, ek = c, t - c
        else:
            eq, ek = t - c + la, c - la
        ql = (q * jnp.exp(eq)).astype(BF16)
        kl = (k * jnp.exp(ek)).astype(BF16)
        sel = lidx == lvl
        for h in range(GLA_HEADS):
            sc = lax.dot_general(ql * head_masks[h], kl, NT_DIMS, preferred_element_type=F32)
            ssum_ref[h] = jnp.where(sel, sc, ssum_ref[h])

    c, t = seg_sums(GLA_LEVELS)
    if forward:
        eq, ek = c, t - c
    else:
        eq, ek = t - c + la, c - la
    q_in = (q * jnp.exp(eq)).astype(BF16)
    k_out = (k * jnp.exp(ek)).astype(BF16)
    o = jnp.dot(q_in, state_ref[...].astype(BF16), preferred_element_type=F32)
    if forward:
        o = o + jnp.dot((q * k).astype(BF16), bexp_ref[...], preferred_element_type=F32) * v
    for h in range(GLA_HEADS):
        oh = jnp.dot(ssum_ref[h].astype(BF16), vb[:, h * GLA_DV:(h + 1) * GLA_DV],
                     preferred_element_type=F32)
        o_ref[:, h * GLA_DV:(h + 1) * GLA_DV] = o[:, h * GLA_DV:(h + 1) * GLA_DV] + oh

    ones = jnp.ones((L, LANES), BF16)
    tot_col = (lax.dot_general(la_hi, ones, TN_DIMS, preferred_element_type=F32)
               + lax.dot_general(la_lo, ones, TN_DIMS, preferred_element_type=F32))
    dec = jnp.exp(tot_col)
    upd = lax.dot_general(k_out, vb, TN_DIMS, preferred_element_type=F32)
    for h in range(GLA_HEADS):
        cols = slice(h * GLA_DV, (h + 1) * GLA_DV)
        state_ref[:, cols] = dec * state_ref[:, cols] + upd[:, cols] * bexp_ref[:, cols].astype(F32)


def _gla_kernel(qf_ref, kf_ref, vf_ref, laf_ref, qb_ref, kb_ref, vb_ref, lab_ref,
                pq_ref, lidxf_ref, lidxb_ref, bexp_ref, of_ref, ob_ref, sf_ref, sb_ref, ssum_ref):
    @pl.when(pl.program_id(1) == 0)
    def _():
        sf_ref[...] = jnp.zeros_like(sf_ref)
        sb_ref[...] = jnp.zeros_like(sb_ref)

    _gla_direction(qf_ref[...], kf_ref[...], vf_ref[...], laf_ref[...], pq_ref, lidxf_ref[...],
                   bexp_ref, ssum_ref, sf_ref, of_ref, True)
    _gla_direction(qb_ref[...], kb_ref[...], vb_ref[...], lab_ref[...], pq_ref, lidxb_ref[...],
                   bexp_ref, ssum_ref, sb_ref, ob_ref, False)


def _gla_call(gq, gk, gv, laf, lab):
    B, S, _ = gq.shape
    L = GLA_TILE
    n = S // L
    pq, lidx_f, lidx_b, bexp = _gla_consts()
    fwd = lambda w: pl.BlockSpec((None, L, w), lambda b, i: (b, i, 0))
    bwd = lambda w: pl.BlockSpec((None, L, w), lambda b, i: (b, n - 1 - i, 0))
    const = lambda a: pl.BlockSpec(a.shape, lambda b, i, _nd=a.ndim: (0,) * _nd)
    return pl.pallas_call(
        _gla_kernel, grid=(B, n),
        in_specs=[fwd(GLA_KDIM), fwd(GLA_KDIM), fwd(GLA_WIDTH), fwd(GLA_KDIM),
                  bwd(GLA_KDIM), bwd(GLA_KDIM), bwd(GLA_WIDTH), bwd(GLA_KDIM),
                  const(pq), const(lidx_f), const(lidx_b), const(bexp)],
        out_specs=[fwd(GLA_WIDTH), bwd(GLA_WIDTH)],
        out_shape=[jax.ShapeDtypeStruct((B, S, GLA_WIDTH), F32)] * 2,
        scratch_shapes=[pltpu.VMEM((GLA_KDIM, GLA_WIDTH), F32), pltpu.VMEM((GLA_KDIM, GLA_WIDTH), F32),
                        pltpu.VMEM((GLA_HEADS, L, L), F32)],
        compiler_params=_cparams(("parallel", "arbitrary")), name="gla",
    )(gq, gk, gv, laf, gq, gk, gv, lab, pq, lidx_f, lidx_b, bexp)


def _post_kernel(x_ref, oT_ref, of_ref, ob_ref, gr_ref, gon_ref, woa_ref, wob_ref, fn_ref,
                 rwT_ref, rb_ref, upper_ref,
                 xmid_ref, h2_ref, meta_ref, gates_ref, counts_ref, carry_ref):
    first = (pl.program_id(0) == 0) & (pl.program_id(1) == 0)

    @pl.when(first)
    def _():
        carry_ref[...] = jnp.zeros_like(carry_ref)

    o = of_ref[...] + ob_ref[...]
    gr = gr_ref[...]
    parts = []
    for h in range(GLA_HEADS):
        cols = slice(h * GLA_DV, (h + 1) * GLA_DV)
        parts.append(_rms(o[:, cols], gon_ref[...]) * jax.nn.silu(gr[:, cols]))
    gla = jnp.concatenate(parts, axis=1).astype(BF16)

    x_mid = (x_ref[...]
             + lax.dot_general(oT_ref[...], woa_ref[...], TN_DIMS, preferred_element_type=F32)
             + jnp.dot(gla, wob_ref[...], preferred_element_type=F32))
    xmid_ref[...] = x_mid
    h2 = _rms(x_mid, fn_ref[...])
    h2_ref[...] = h2

    h_hi, h_lo = _split_bf16(h2)
    w_hi, w_lo = _split_bf16(rwT_ref[...])
    logits = (lax.dot_general(w_hi, h_hi, NT_DIMS, preferred_element_type=F32)
              + lax.dot_general(w_hi, h_lo, NT_DIMS, preferred_element_type=F32)
              + lax.dot_general(w_lo, h_hi, NT_DIMS, preferred_element_type=F32)
              + rb_ref[...])
    tm = logits.shape[1]
    eidx = lax.broadcasted_iota(jnp.int32, (N_EXPERTS, tm), 0).astype(F32)
    vals, idxs, sels = [], [], []
    cur = logits
    for _ in range(TOP_K):
        mk = jnp.max(cur, axis=0, keepdims=True)
        ik = jnp.min(jnp.where(cur == mk, eidx, float(N_EXPERTS)), axis=0, keepdims=True)
        sel = eidx == ik
        vals.append(mk)
        idxs.append(ik)
        sels.append(sel)
        cur = jnp.where(sel, -jnp.inf, cur)
    exps = [jnp.exp(vk - vals[0]) for vk in vals]
    denom = exps[0] + exps[1] + exps[2] + exps[3]
    gates = [e / denom for e in exps]

    cnt = (sels[0] | sels[1] | sels[2] | sels[3])
    before = jnp.dot(cnt.astype(BF16), upper_ref[...], preferred_element_type=F32) + carry_ref[:, 0:1]
    ranks = [jnp.sum(jnp.where(sel, before, 0.0), axis=0, keepdims=True) for sel in sels]
    carry_ref[...] = carry_ref[...] + jnp.sum(cnt.astype(F32), axis=1, keepdims=True)
    counts_ref[...] = carry_ref[...]

    meta_ref[...] = jnp.concatenate(idxs + ranks, axis=0).astype(jnp.int32)
    gates_ref[...] = jnp.concatenate(gates + [jnp.zeros((TOP_K, tm), F32)], axis=0)


def _post_call(x, oT, o_f, o_b, gr, prm, tm):
    B, S, D = x.shape
    nS = S // tm
    upper = jnp.asarray(np.triu(np.ones((tm, tm), np.float32), k=1), BF16)
    full = lambda a: pl.BlockSpec(a.shape, lambda b, i, _nd=a.ndim: (0,) * _nd)
    tok = lambda w: pl.BlockSpec((None, tm, w), lambda b, i: (b, i, 0))
    colblk = lambda r: pl.BlockSpec((None, r, tm), lambda b, i: (b, 0, i))
    return pl.pallas_call(
        _post_kernel, grid=(B, nS),
        in_specs=[tok(D), colblk(MLA_WIDTH), tok(GLA_WIDTH), tok(GLA_WIDTH), tok(GLA_WIDTH),
                  full(prm["gla_out_norm"]), full(prm["w_out_a"]), full(prm["w_out_b"]),
                  full(prm["ffn_norm"]), full(prm["router_wT"]), full(prm["router_b"]), full(upper)],
        out_specs=[tok(D), tok(D), colblk(2 * TOP_K), colblk(2 * TOP_K),
                   pl.BlockSpec((N_EXPERTS, LANES), lambda b, i: (0, 0))],
        out_shape=[jax.ShapeDtypeStruct((B, S, D), F32), jax.ShapeDtypeStruct((B, S, D), F32),
                   jax.ShapeDtypeStruct((B, 2 * TOP_K, S), jnp.int32),
                   jax.ShapeDtypeStruct((B, 2 * TOP_K, S), F32),
                   jax.ShapeDtypeStruct((N_EXPERTS, LANES), F32)],
        scratch_shapes=[pltpu.VMEM((N_EXPERTS, LANES), F32)],
        compiler_params=_cparams(("arbitrary", "arbitrary")), name="post",
    )(x, oT, o_f, o_b, gr, prm["gla_out_norm"], prm["w_out_a"], prm["w_out_b"], prm["ffn_norm"],
      prm["router_wT"], prm["router_b"], upper)


def _dispatch_kernel(dest_ref, h2_ref, xs_in_ref, xs_ref, sem):
    del xs_in_ref
    td = h2_ref.shape[0]

    def row_copy(r, kk):
        return pltpu.make_async_copy(h2_ref.at[pl.ds(r, 1), :],
                                     xs_ref.at[pl.ds(dest_ref[kk, r], 1), :], sem)

    def start(r, c):
        for kk in range(TOP_K):
            row_copy(r, kk).start()
        return c

    def wait(r, c):
        for kk in range(TOP_K):
            row_copy(r, kk).wait()
        return c

    lax.fori_loop(0, td, start, 0)
    lax.fori_loop(0, td, wait, 0)


def _dispatch_call(dest, h2, n_rows, td):
    T, D = h2.shape
    xs0 = jnp.zeros((n_rows, D), F32)
    return pl.pallas_call(
        _dispatch_kernel, grid=(T // td,),
        in_specs=[pl.BlockSpec((TOP_K, td), lambda i: (0, i), memory_space=pltpu.SMEM),
                  pl.BlockSpec((td, D), lambda i: (i, 0)),
                  pl.BlockSpec(memory_space=pl.ANY)],
        out_specs=pl.BlockSpec(memory_space=pl.ANY),
        out_shape=jax.ShapeDtypeStruct((n_rows, D), F32),
        scratch_shapes=[pltpu.SemaphoreType.DMA(())],
        input_output_aliases={2: 0},
        compiler_params=_cparams(("arbitrary",)), name="dispatch",
    )(dest, h2, xs0)


def _expert_kernel(be_ref, nv_ref, xs_ref, wg_ref, wl_ref, bg_ref, bl_ref, wd_ref, bd_ref, ys_ref):
    del be_ref

    @pl.when(pl.program_id(0) < nv_ref[0])
    def _():
        xb = xs_ref[...].astype(BF16)
        g = jnp.dot(xb, wg_ref[...], preferred_element_type=F32) + bg_ref[...]
        l = jnp.dot(xb, wl_ref[...], preferred_element_type=F32) + bl_ref[...]
        glu = jnp.minimum(g, SWIGLU_LIMIT)
        lin = jnp.clip(l, -SWIGLU_LIMIT, SWIGLU_LIMIT)
        act = glu * jax.nn.sigmoid(SWIGLU_ALPHA * glu) * (lin + 1.0)
        ys_ref[...] = jnp.dot(act.astype(BF16), wd_ref[...], preferred_element_type=F32) + bd_ref[...]

    @pl.when(pl.program_id(0) >= nv_ref[0])
    def _():
        ys_ref[...] = jnp.zeros_like(ys_ref)


def _expert_call(block_expert, n_valid, xs, prm):
    n_rows, D = xs.shape
    n_blocks = n_rows // MOE_BLOCK
    rows = lambda i, be, nv: (jnp.minimum(i, nv[0] - 1), 0)
    wsel = lambda i, be, nv: (be[i], 0, 0)
    grid_spec = pltpu.PrefetchScalarGridSpec(
        num_scalar_prefetch=2, grid=(n_blocks,),
        in_specs=[pl.BlockSpec((MOE_BLOCK, D), rows),
                  pl.BlockSpec((None, D, D_FF), wsel), pl.BlockSpec((None, D, D_FF), wsel),
                  pl.BlockSpec((None, 1, D_FF), wsel), pl.BlockSpec((None, 1, D_FF), wsel),
                  pl.BlockSpec((None, D_FF, D), wsel), pl.BlockSpec((None, 1, D), wsel)],
        out_specs=pl.BlockSpec((MOE_BLOCK, D), lambda i, be, nv: (i, 0)))
    return pl.pallas_call(
        _expert_kernel, grid_spec=grid_spec,
        out_shape=jax.ShapeDtypeStruct((n_rows, D), F32),
        compiler_params=_cparams(("arbitrary",)), name="experts",
    )(block_expert, n_valid, xs, prm["w_glu"], prm["w_lin"], prm["b_glu"], prm["b_lin"],
      prm["w_dn"], prm["b_dn"])


def _combine_kernel(dest_ref, xmid_ref, gates_ref, fnorm_ref, ys_ref, out_ref, buf_ref, sem):
    tc = xmid_ref.shape[0]

    def row_copy(r, kk):
        return pltpu.make_async_copy(ys_ref.at[pl.ds(dest_ref[kk, r], 1), :],
                                     buf_ref.at[kk, pl.ds(r, 1), :], sem)

    def start(r, c):
        for kk in range(TOP_K):
            row_copy(r, kk).start()
        return c

    def wait(r, c):
        for kk in range(TOP_K):
            row_copy(r, kk).wait()
        return c

    lax.fori_loop(0, tc, start, 0)
    g = jnp.transpose(gates_ref[...])
    lax.fori_loop(0, tc, wait, 0)
    y = xmid_ref[...]
    for kk in range(TOP_K):
        y = y + buf_ref[kk] * g[:, kk:kk + 1]
    out_ref[...] = _rms(y, fnorm_ref[...])


def _combine_call(dest, x_mid, gates, final_norm, ys, tc):
    T, D = x_mid.shape
    return pl.pallas_call(
        _combine_kernel, grid=(T // tc,),
        in_specs=[pl.BlockSpec((TOP_K, tc), lambda i: (0, i), memory_space=pltpu.SMEM),
                  pl.BlockSpec((tc, D), lambda i: (i, 0)),
                  pl.BlockSpec((2 * TOP_K, tc), lambda i: (0, i)),
                  pl.BlockSpec(final_norm.shape, lambda i: (0, 0)),
                  pl.BlockSpec(memory_space=pl.ANY)],
        out_specs=pl.BlockSpec((tc, D), lambda i: (i, 0)),
        out_shape=jax.ShapeDtypeStruct((T, D), F32),
        scratch_shapes=[pltpu.VMEM((TOP_K, tc, D), F32), pltpu.SemaphoreType.DMA(())],
        compiler_params=_cparams(("arbitrary",)), name="combine",
    )(dest, x_mid, gates, final_norm, ys)


def _prep_params(attn_norm, w_in, mla_q_norm, mla_w_uq, mla_kv_norm, mla_w_ukv, gla_w_gate_fwd,
                 gla_b_gate_fwd, gla_w_gate_bwd, gla_b_gate_bwd, gla_out_norm, w_out, ffn_norm,
                 router_w, router_b, w_gu, b_gu, w_dn, b_dn):
    D = D_MODEL
    o = np.cumsum((0, MLA_Q_LORA, MLA_KV_LORA, MLA_ROPE, GLA_KDIM, GLA_KDIM, GLA_WIDTH,
                   2 * GLA_GATE_RANK, GLA_WIDTH))
    seg = [w_in[:, o[n]:o[n + 1]] for n in range(8)]
    z = lambda n: jnp.zeros((D, n), w_in.dtype)
    w_in_p = jnp.concatenate(
        [seg[0], seg[1], z(MLA_NOPE), seg[2], z(LANES - MLA_NOPE - MLA_ROPE), seg[3], seg[4], seg[5],
         seg[6], z(LANES - 2 * GLA_GATE_RANK), seg[7]], axis=1).astype(BF16)

    wq = mla_w_uq.reshape(MLA_Q_LORA, MLA_HEADS, MLA_NOPE + MLA_ROPE)
    w_uq_p = jnp.concatenate(
        [wq[:, :, :MLA_NOPE].reshape(MLA_Q_LORA, -1),
         wq[:, :, MLA_NOPE:MLA_NOPE + HALF_ROPE].reshape(MLA_Q_LORA, -1),
         wq[:, :, MLA_NOPE + HALF_ROPE:].reshape(MLA_Q_LORA, -1)], axis=1)
    wkv = mla_w_ukv.reshape(MLA_KV_LORA, MLA_HEADS, MLA_NOPE + MLA_V)
    w_k = jnp.concatenate([wkv[:, :, :MLA_NOPE], jnp.zeros((MLA_KV_LORA, MLA_HEADS, QK_PAD - MLA_NOPE),
                                                            wkv.dtype)], axis=2)
    w_v = wkv[:, :, MLA_NOPE:].reshape(MLA_KV_LORA, -1)

    def gate_w(w, row0):
        full = jnp.zeros((LANES, GLA_KDIM), w.dtype)
        return full.at[row0:row0 + GLA_GATE_RANK].set(w).astype(BF16)

    E = N_EXPERTS
    return {
        "attn_norm": attn_norm.reshape(1, D), "w_in": w_in_p,
        "q_norm": mla_q_norm.reshape(1, -1), "w_uqT": w_uq_p.T.astype(BF16),
        "kv_norm": mla_kv_norm.reshape(1, -1),
        "w_k": w_k.reshape(MLA_KV_LORA, -1).astype(BF16), "w_vT": w_v.T.astype(BF16),
        "w_gf": gate_w(gla_w_gate_fwd, 0), "b_gf": gla_b_gate_fwd.reshape(1, -1),
        "w_gb": gate_w(gla_w_gate_bwd, GLA_GATE_RANK), "b_gb": gla_b_gate_bwd.reshape(1, -1),
        "gla_out_norm": gla_out_norm.reshape(1, -1),
        "w_out_a": w_out[:MLA_WIDTH].astype(BF16), "w_out_b": w_out[MLA_WIDTH:].astype(BF16),
        "ffn_norm": ffn_norm.reshape(1, D),
        "router_wT": router_w.T, "router_b": router_b.reshape(E, 1),
        "w_glu": w_gu[:, :, 0::2].astype(BF16), "w_lin": w_gu[:, :, 1::2].astype(BF16),
        "b_glu": b_gu[:, 0::2].reshape(E, 1, D_FF), "b_lin": b_gu[:, 1::2].reshape(E, 1, D_FF),
        "w_dn": w_dn.astype(BF16), "b_dn": b_dn.reshape(E, 1, D),
    }


def _rope_tables(S):
    inv_freq = jnp.power(ROPE_THETA, -jnp.arange(0, MLA_ROPE, 2, dtype=F32) / MLA_ROPE)
    ang = jnp.arange(S, dtype=F32)[:, None] * inv_freq[None, :]
    cos, sin = jnp.cos(ang), jnp.sin(ang)
    z = lambda n: jnp.zeros((S, n), F32)
    tail = LANES - MLA_NOPE - MLA_ROPE
    return {
        "c": jnp.concatenate([z(MLA_NOPE), cos, cos, z(tail)], axis=1),
        "s1": jnp.concatenate([z(MLA_NOPE), -sin, z(HALF_ROPE), z(tail)], axis=1),
        "s2": jnp.concatenate([z(MLA_NOPE), z(HALF_ROPE), sin, z(tail)], axis=1),
        "cosT": jnp.tile(cos.T, (MLA_HEADS, 1)), "sinT": jnp.tile(sin.T, (MLA_HEADS, 1)),
    }


def _encoder(x, prm, final_norm):
    B, S, D = x.shape
    rope = _rope_tables(S)
    qT, k, vT, gq, gk, gv, laf, lab, gr = _pre_call(x, prm, rope, ATTN_KEY_TILE)
    oT = _attn_call(qT, k, vT, min(ATTN_QUERY_TILE, S)).reshape(B, MLA_WIDTH, S)
    o_f, o_b = _gla_call(gq, gk, gv, laf, lab)
    x_mid, h2, meta, gates, counts = _post_call(x, oT, o_f, o_b, gr, prm, min(256, S))

    T = B * S
    n_rows = T * TOP_K + N_EXPERTS * MOE_BLOCK
    n_blocks = n_rows // MOE_BLOCK
    cnt = counts[:, 0].astype(jnp.int32)
    padded = ((cnt + MOE_BLOCK - 1) // MOE_BLOCK) * MOE_BLOCK
    pends = jnp.cumsum(padded)
    pstarts = pends - padded
    dest = pstarts[meta[:, :TOP_K, :]] + meta[:, TOP_K:, :]
    dest = jnp.transpose(dest, (1, 0, 2)).reshape(TOP_K, T)
    gates = jnp.transpose(gates, (1, 0, 2)).reshape(2 * TOP_K, T)
    n_valid = (pends[-1] // MOE_BLOCK).astype(jnp.int32).reshape(1)
    blk = jnp.minimum(jnp.arange(n_blocks, dtype=jnp.int32), n_valid[0] - 1) * MOE_BLOCK
    block_expert = jnp.minimum(jnp.searchsorted(pends, blk, side="right"), N_EXPERTS - 1).astype(jnp.int32)

    xs = _dispatch_call(dest, h2.reshape(T, D), n_rows, min(128, S))
    ys = _expert_call(block_expert, n_valid, xs, prm)
    y = _combine_call(dest, x_mid.reshape(T, D), gates, final_norm.reshape(1, D), ys, min(128, S))
    return y.reshape(B, S, D)


def kernel(x_prompt, x_sample, attn_norm, w_in, mla_q_norm, mla_w_uq, mla_kv_norm, mla_w_ukv,
           gla_w_gate_fwd, gla_b_gate_fwd, gla_w_gate_bwd, gla_b_gate_bwd, gla_out_norm, w_out, ffn_norm,
           router_w, router_b, expert_w_gate_up, expert_b_gate_up, expert_w_down, expert_b_down,
           final_norm):
    layer = (attn_norm, w_in, mla_q_norm, mla_w_uq, mla_kv_norm, mla_w_ukv, gla_w_gate_fwd,
             gla_b_gate_fwd, gla_w_gate_bwd, gla_b_gate_bwd, gla_out_norm, w_out, ffn_norm, router_w,
             router_b, expert_w_gate_up, expert_b_gate_up, expert_w_down, expert_b_down)
    assert all(p.shape[0] == 1 for p in layer), "single layer expected"
    assert x_prompt.shape[1:] == x_sample.shape[1:]
    prm = _prep_params(*[p[0] for p in layer])
    nb = x_prompt.shape[0]
    y = _encoder(jnp.concatenate([x_prompt, x_sample], axis=0), prm, final_norm)
    return y[:nb], y[nb:]
```

```python
import functools

import jax
import jax.numpy as jnp
import numpy as np
from jax import lax
from jax.experimental import pallas as pl
from jax.experimental.pallas import tpu as pltpu

F32 = jnp.float32
BF16 = jnp.bfloat16

D_MODEL = 1024
MLA_HEADS = 8
MLA_NOPE = 64
MLA_ROPE = 32
MLA_V = 64
MLA_Q_LORA = 384
MLA_KV_LORA = 256
ROPE_THETA = 10000.0
GLA_HEADS = 4
GLA_DK = 64
GLA_DV = 128
GLA_GATE_RANK = 16
GLA_GATE_NORM = 16.0
N_EXPERTS = 32
TOP_K = 4
D_FF = 1024
SWIGLU_LIMIT = 7.0
SWIGLU_ALPHA = 1.702
MOE_BLOCK = 128
RMS_EPS = 1e-6

MLA_WIDTH = MLA_HEADS * MLA_V
GLA_WIDTH = GLA_HEADS * GLA_DV
GLA_KDIM = GLA_HEADS * GLA_DK
HALF_ROPE = MLA_ROPE // 2
QK_PAD = 128
LANES = 128
V_ONES_ROWS = 16

OFF_CQ = 0
OFF_CKV = OFF_CQ + MLA_Q_LORA
OFF_KR = OFF_CKV + MLA_KV_LORA
OFF_GQ = OFF_KR + LANES
OFF_GK = OFF_GQ + GLA_KDIM
OFF_GV = OFF_GK + GLA_KDIM
OFF_LR = OFF_GV + GLA_WIDTH
OFF_GR = OFF_LR + LANES
PROJ_COLS = OFF_GR + GLA_WIDTH

ATTN_KEY_TILE = 256
ATTN_QUERY_TILE = 512
ATTN_UNROLL = 16
ATTN_SCORE_SLOTS = 4
GLA_TILE = 256
GLA_LEVELS = 8
PERM_GROUP = 256
VMEM_LIMIT = 56 * 1024 * 1024

LOG2_E = 1.4426950408889634
NT_DIMS = (((1,), (1,)), ((), ()))
TN_DIMS = (((0,), (0,)), ((), ()))


def _cparams(sem):
    return pltpu.CompilerParams(dimension_semantics=sem, vmem_limit_bytes=VMEM_LIMIT)


def _rms(x, gain):
    return x * lax.rsqrt(jnp.mean(x * x, axis=-1, keepdims=True) + RMS_EPS) * gain


def _split_bf16(x):
    hi = x.astype(BF16)
    lo = (x - hi.astype(F32)).astype(BF16)
    return hi, lo


def _pre_kernel(x_ref, an_ref, win_ref, qn_ref, wuqT_ref, kvn_ref, wk_ref, wvT_ref,
                wgf_ref, bgf_ref, wgb_ref, bgb_ref, rc_ref, rs1_ref, rs2_ref, cosT_ref, sinT_ref,
                qT_ref, k_ref, vT_ref, gq_ref, gk_ref, gv_ref, laf_ref, lab_ref, gr_ref):
    x = x_ref[...]
    h = _rms(x, an_ref[...]).astype(BF16)
    proj = jnp.dot(h, win_ref[...], preferred_element_type=F32)

    cqn = _rms(proj[:, OFF_CQ:OFF_CQ + MLA_Q_LORA], qn_ref[...]).astype(BF16)
    ckvn = _rms(proj[:, OFF_CKV:OFF_CKV + MLA_KV_LORA], kvn_ref[...]).astype(BF16)

    scale = (MLA_NOPE + MLA_ROPE) ** -0.5 * LOG2_E
    qT = lax.dot_general(wuqT_ref[...], cqn, NT_DIMS, preferred_element_type=F32) * scale
    n_nope = MLA_HEADS * MLA_NOPE
    n_half = MLA_HEADS * HALF_ROPE
    x1 = qT[n_nope:n_nope + n_half]
    x2 = qT[n_nope + n_half:n_nope + 2 * n_half]
    c = cosT_ref[...]
    s = sinT_ref[...]
    x1r = x1 * c - x2 * s
    x2r = x1 * s + x2 * c
    zpad = jnp.zeros((QK_PAD - MLA_NOPE - MLA_ROPE, qT.shape[1]), BF16)
    for hd in range(MLA_HEADS):
        qT_ref[hd, 0:MLA_NOPE, :] = qT[hd * MLA_NOPE:(hd + 1) * MLA_NOPE].astype(BF16)
        qT_ref[hd, MLA_NOPE:MLA_NOPE + HALF_ROPE, :] = x1r[hd * HALF_ROPE:(hd + 1) * HALF_ROPE].astype(BF16)
        qT_ref[hd, MLA_NOPE + HALF_ROPE:MLA_NOPE + MLA_ROPE, :] = (
            x2r[hd * HALF_ROPE:(hd + 1) * HALF_ROPE].astype(BF16))
        qT_ref[hd, MLA_NOPE + MLA_ROPE:QK_PAD, :] = zpad

    kr = proj[:, OFF_KR:OFF_KR + LANES]
    kr = (kr * rc_ref[...]
          + pltpu.roll(kr, LANES - HALF_ROPE, axis=1) * rs1_ref[...]
          + pltpu.roll(kr, HALF_ROPE, axis=1) * rs2_ref[...])
    kfull = jnp.dot(ckvn, wk_ref[...], preferred_element_type=F32)
    for hd in range(MLA_HEADS):
        k_ref[hd] = (kfull[:, hd * QK_PAD:(hd + 1) * QK_PAD] + kr).astype(BF16)

    vT = lax.dot_general(wvT_ref[...], ckvn, NT_DIMS, preferred_element_type=F32)
    for hd in range(MLA_HEADS):
        vT_ref[hd] = vT[hd * MLA_V:(hd + 1) * MLA_V].astype(BF16)

    gq_ref[...] = proj[:, OFF_GQ:OFF_GQ + GLA_KDIM] * (GLA_DK ** -0.5)
    gk_ref[...] = proj[:, OFF_GK:OFF_GK + GLA_KDIM]
    gv_ref[...] = proj[:, OFF_GV:OFF_GV + GLA_WIDTH]
    gr_ref[...] = proj[:, OFF_GR:OFF_GR + GLA_WIDTH]
    lr = proj[:, OFF_LR:OFF_LR + LANES].astype(BF16)

    def log_decay(w_ref, b_ref):
        z = jnp.dot(lr, w_ref[...], preferred_element_type=F32) + b_ref[...]
        return (jnp.minimum(z, 0.0) - jnp.log1p(jnp.exp(-jnp.abs(z)))) * (1.0 / GLA_GATE_NORM)

    laf_ref[...] = log_decay(wgf_ref, bgf_ref)
    lab_ref[...] = log_decay(wgb_ref, bgb_ref)


def _pre_call(x, prm, rope, tm):
    B, S, D = x.shape
    nS = S // tm
    H = MLA_HEADS

    def full(a):
        nd = a.ndim
        return pl.BlockSpec(a.shape, lambda b, i, _nd=nd: (0,) * _nd)

    tok = lambda w: pl.BlockSpec((None, tm, w), lambda b, i: (b, i, 0))
    in_specs = [
        tok(D), full(prm["attn_norm"]), full(prm["w_in"]), full(prm["q_norm"]), full(prm["w_uqT"]),
        full(prm["kv_norm"]), full(prm["w_k"]), full(prm["w_vT"]),
        full(prm["w_gf"]), full(prm["b_gf"]), full(prm["w_gb"]), full(prm["b_gb"]),
        pl.BlockSpec((tm, LANES), lambda b, i: (i, 0)),
        pl.BlockSpec((tm, LANES), lambda b, i: (i, 0)),
        pl.BlockSpec((tm, LANES), lambda b, i: (i, 0)),
        pl.BlockSpec((H * HALF_ROPE, tm), lambda b, i: (0, i)),
        pl.BlockSpec((H * HALF_ROPE, tm), lambda b, i: (0, i)),
    ]
    out_shape = [
        jax.ShapeDtypeStruct((B, H, QK_PAD, S), BF16),
        jax.ShapeDtypeStruct((B, H, S, QK_PAD), BF16),
        jax.ShapeDtypeStruct((B, H, nS, MLA_V, tm), BF16),
        jax.ShapeDtypeStruct((B, S, GLA_KDIM), F32),
        jax.ShapeDtypeStruct((B, S, GLA_KDIM), F32),
        jax.ShapeDtypeStruct((B, S, GLA_WIDTH), F32),
        jax.ShapeDtypeStruct((B, S, GLA_KDIM), F32),
        jax.ShapeDtypeStruct((B, S, GLA_KDIM), F32),
        jax.ShapeDtypeStruct((B, S, GLA_WIDTH), F32),
    ]
    out_specs = [
        pl.BlockSpec((None, H, QK_PAD, tm), lambda b, i: (b, 0, 0, i)),
        pl.BlockSpec((None, H, tm, QK_PAD), lambda b, i: (b, 0, i, 0)),
        pl.BlockSpec((None, H, None, MLA_V, tm), lambda b, i: (b, 0, i, 0, 0)),
        tok(GLA_KDIM), tok(GLA_KDIM), tok(GLA_WIDTH), tok(GLA_KDIM), tok(GLA_KDIM), tok(GLA_WIDTH),
    ]
    return pl.pallas_call(
        _pre_kernel, grid=(B, nS), in_specs=in_specs, out_specs=out_specs, out_shape=out_shape,
        compiler_params=_cparams(("parallel", "parallel")), name="pre",
    )(x, prm["attn_norm"], prm["w_in"], prm["q_norm"], prm["w_uqT"], prm["kv_norm"], prm["w_k"],
      prm["w_vT"], prm["w_gf"], prm["b_gf"], prm["w_gb"], prm["b_gb"],
      rope["c"], rope["s1"], rope["s2"], rope["cosT"], rope["sinT"])


def _attn_kernel(qT_ref, k_ref, vT_ref, o_ref, s_ref, p_ref, *, n_kblk, unroll):
    qT = qT_ref[...]
    tq = qT.shape[1]
    tk = k_ref.shape[0] // n_kblk
    ones = jnp.ones((V_ONES_ROWS, tk), BF16)

    def scores(j, slot):
        kb = k_ref[pl.ds(pl.multiple_of(j * tk, tk), tk), :]
        sT = jnp.dot(kb, qT, preferred_element_type=F32)
        s_ref[slot] = sT
        return jnp.max(sT, axis=0, keepdims=True)

    def values(j, slot):
        vext = jnp.concatenate([vT_ref[j], ones], axis=0)
        return jnp.dot(vext, p_ref[slot], preferred_element_type=F32)

    def softmax(s_slot, p_slot, blk_max, m):
        m_new = jnp.maximum(m, blk_max)
        p_ref[p_slot] = jnp.exp2(s_ref[s_slot] - m_new).astype(BF16)
        return m_new, jnp.exp2(m - m_new)

    def body(jj, carry):
        max_0, max_1, m, alpha_prev, acc = carry
        for u in range(unroll):
            j = unroll * jj + u
            pv = values(jnp.maximum(j - 1, 0), (u + 1) % 2)
            max_2 = scores(jnp.minimum(j + 2, n_kblk - 1), (u + 2) % ATTN_SCORE_SLOTS)
            m, alpha = softmax(u % ATTN_SCORE_SLOTS, u % 2, max_0, m)
            acc = alpha_prev * acc + pv
            max_0, max_1, alpha_prev = max_1, max_2, alpha
        return max_0, max_1, m, alpha_prev, acc

    p_ref[1] = jnp.zeros(p_ref.shape[1:], BF16)
    m0 = jnp.full((1, tq), -jnp.inf, F32)
    acc0 = jnp.zeros((MLA_V + V_ONES_ROWS, tq), F32)
    carry0 = (scores(0, 0), scores(1, 1), m0, jnp.ones((1, tq), F32), acc0)
    _, _, _, alpha, acc = lax.fori_loop(0, n_kblk // unroll, body, carry0)
    acc = alpha * acc + values(n_kblk - 1, 1)
    o_ref[...] = (acc[0:MLA_V] / acc[MLA_V:MLA_V + 1]).astype(o_ref.dtype)


def _attn_call(qT, k, vT, tq):
    B, H, _, S = qT.shape
    n_kblk, tk = vT.shape[2], vT.shape[4]
    unroll = min(ATTN_UNROLL, n_kblk)
    assert unroll % ATTN_SCORE_SLOTS == 0 and n_kblk % unroll == 0 and S % tq == 0
    return pl.pallas_call(
        functools.partial(_attn_kernel, n_kblk=n_kblk, unroll=unroll),
        grid=(B, H, S // tq),
        in_specs=[
            pl.BlockSpec((None, None, QK_PAD, tq), lambda b, h, i: (b, h, 0, i)),
            pl.BlockSpec((None, None, S, QK_PAD), lambda b, h, i: (b, h, 0, 0)),
            pl.BlockSpec((None, None, n_kblk, MLA_V, tk), lambda b, h, i: (b, h, 0, 0, 0)),
        ],
        out_specs=pl.BlockSpec((None, None, MLA_V, tq), lambda b, h, i: (b, h, 0, i)),
        out_shape=jax.ShapeDtypeStruct((B, H, MLA_V, S), BF16),
        scratch_shapes=[pltpu.VMEM((ATTN_SCORE_SLOTS, tk, tq), F32), pltpu.VMEM((2, tk, tq), BF16)],
        compiler_params=_cparams(("parallel", "parallel", "parallel")), name="attn",
    )(qT, k, vT)


def _gla_consts():
    L = GLA_TILE
    i = np.arange(L)[:, None]
    j = np.arange(L)[None, :]
    mats = []
    for lvl in range(1, GLA_LEVELS + 1):
        m = 1 << lvl
        mats.append(((i // m == j // m) & (j <= i)).astype(np.float32))
    for lvl in range(1, GLA_LEVELS + 1):
        m = 1 << lvl
        mats.append((i // m == j // m).astype(np.float32))
    x = i ^ j
    lidx = np.where(x > 0, np.floor(np.log2(np.maximum(x, 1))), -1).astype(np.int32)
    lidx_f = np.where(i > j, lidx, -1).astype(np.int32)
    lidx_b = np.where(i < j, lidx, -1).astype(np.int32)
    hd = np.arange(GLA_KDIM)[:, None] // GLA_DK
    hv = np.arange(GLA_WIDTH)[None, :] // GLA_DV
    bexp = (hd == hv).astype(np.float32)
    return (jnp.asarray(np.stack(mats), BF16), jnp.asarray(lidx_f), jnp.asarray(lidx_b),
            jnp.asarray(bexp, BF16))


def _gla_direction(q, k, v, la, pq_ref, lidx, bexp_ref, ssum_ref, state_ref, o_ref, forward):
    L = GLA_TILE
    la_hi, la_lo = _split_bf16(la)
    vb = v.astype(BF16)
    lane_head = lax.broadcasted_iota(jnp.int32, (1, GLA_KDIM), 1) // GLA_DK
    head_masks = [(lane_head == h).astype(BF16) for h in range(GLA_HEADS)]

    def seg_sums(lvl):
        if lvl == 0:
            return la, la
        P = pq_ref[lvl - 1]
        Q = pq_ref[GLA_LEVELS + lvl - 1]
        c = jnp.dot(P, la_hi, preferred_element_type=F32) + jnp.dot(P, la_lo, preferred_element_type=F32)
        t = jnp.dot(Q, la_hi, preferred_element_type=F32) + jnp.dot(Q, la_lo, preferred_element_type=F32)
        return c, t

    ssum_ref[...] = jnp.zeros_like(ssum_ref)
    for lvl in range(GLA_LEVELS):
        c, t = seg_sums(lvl)
        if forward:
            eq, ek = c, t - c
        else:
            eq, ek = t - c + la, c - la
        ql = (q * jnp.exp(eq)).astype(BF16)
        kl = (k * jnp.exp(ek)).astype(BF16)
        sel = lidx == lvl
        for h in range(GLA_HEADS):
            sc = lax.dot_general(ql * head_masks[h], kl, NT_DIMS, preferred_element_type=F32)
            ssum_ref[h] = jnp.where(sel, sc, ssum_ref[h])

    c, t = seg_sums(GLA_LEVELS)
    if forward:
        eq, ek = c, t - c
    else:
        eq, ek = t - c + la, c - la
    q_in = (q * jnp.exp(eq)).astype(BF16)
    k_out = (k * jnp.exp(ek)).astype(BF16)
    o = jnp.dot(q_in, state_ref[...].astype(BF16), preferred_element_type=F32)
    if forward:
        o = o + jnp.dot((q * k).astype(BF16), bexp_ref[...], preferred_element_type=F32) * v
    for h in range(GLA_HEADS):
        oh = jnp.dot(ssum_ref[h].astype(BF16), vb[:, h * GLA_DV:(h + 1) * GLA_DV],
                     preferred_element_type=F32)
        o_ref[:, h * GLA_DV:(h + 1) * GLA_DV] = o[:, h * GLA_DV:(h + 1) * GLA_DV] + oh

    ones = jnp.ones((L, LANES), BF16)
    tot_col = (lax.dot_general(la_hi, ones, TN_DIMS, preferred_element_type=F32)
               + lax.dot_general(la_lo, ones, TN_DIMS, preferred_element_type=F32))
    dec = jnp.exp(tot_col)
    upd = lax.dot_general(k_out, vb, TN_DIMS, preferred_element_type=F32)
    for h in range(GLA_HEADS):
        cols = slice(h * GLA_DV, (h + 1) * GLA_DV)
        state_ref[:, cols] = dec * state_ref[:, cols] + upd[:, cols] * bexp_ref[:, cols].astype(F32)


def _gla_kernel(qf_ref, kf_ref, vf_ref, laf_ref, qb_ref, kb_ref, vb_ref, lab_ref,
                pq_ref, lidxf_ref, lidxb_ref, bexp_ref, of_ref, ob_ref, sf_ref, sb_ref, ssum_ref):
    @pl.when(pl.program_id(1) == 0)
    def _():
        sf_ref[...] = jnp.zeros_like(sf_ref)
        sb_ref[...] = jnp.zeros_like(sb_ref)

    _gla_direction(qf_ref[...], kf_ref[...], vf_ref[...], laf_ref[...], pq_ref, lidxf_ref[...],
                   bexp_ref, ssum_ref, sf_ref, of_ref, True)
    _gla_direction(qb_ref[...], kb_ref[...], vb_ref[...], lab_ref[...], pq_ref, lidxb_ref[...],
                   bexp_ref, ssum_ref, sb_ref, ob_ref, False)


def _gla_call(gq, gk, gv, laf, lab):
    B, S, _ = gq.shape
    L = GLA_TILE
    n = S // L
    pq, lidx_f, lidx_b, bexp = _gla_consts()
    fwd = lambda w: pl.BlockSpec((None, L, w), lambda b, i: (b, i, 0))
    bwd = lambda w: pl.BlockSpec((None, L, w), lambda b, i: (b, n - 1 - i, 0))
    const = lambda a: pl.BlockSpec(a.shape, lambda b, i, _nd=a.ndim: (0,) * _nd)
    return pl.pallas_call(
        _gla_kernel, grid=(B, n),
        in_specs=[fwd(GLA_KDIM), fwd(GLA_KDIM), fwd(GLA_WIDTH), fwd(GLA_KDIM),
                  bwd(GLA_KDIM), bwd(GLA_KDIM), bwd(GLA_WIDTH), bwd(GLA_KDIM),
                  const(pq), const(lidx_f), const(lidx_b), const(bexp)],
        out_specs=[fwd(GLA_WIDTH), bwd(GLA_WIDTH)],
        out_shape=[jax.ShapeDtypeStruct((B, S, GLA_WIDTH), F32)] * 2,
        scratch_shapes=[pltpu.VMEM((GLA_KDIM, GLA_WIDTH), F32), pltpu.VMEM((GLA_KDIM, GLA_WIDTH), F32),
                        pltpu.VMEM((GLA_HEADS, L, L), F32)],
        compiler_params=_cparams(("parallel", "arbitrary")), name="gla",
    )(gq, gk, gv, laf, gq, gk, gv, lab, pq, lidx_f, lidx_b, bexp)


def _post_kernel(x_ref, oT_ref, of_ref, ob_ref, gr_ref, gon_ref, woa_ref, wob_ref, fn_ref,
                 rwT_ref, rb_ref, upper_ref,
                 xmid_ref, h2_ref, meta_ref, gates_ref, counts_ref, carry_ref):
    first = (pl.program_id(0) == 0) & (pl.program_id(1) == 0)

    @pl.when(first)
    def _():
        carry_ref[...] = jnp.zeros_like(carry_ref)

    o = of_ref[...] + ob_ref[...]
    gr = gr_ref[...]
    parts = []
    for h in range(GLA_HEADS):
        cols = slice(h * GLA_DV, (h + 1) * GLA_DV)
        parts.append(_rms(o[:, cols], gon_ref[...]) * jax.nn.silu(gr[:, cols]))
    gla = jnp.concatenate(parts, axis=1).astype(BF16)

    x_mid = (x_ref[...]
             + lax.dot_general(oT_ref[...], woa_ref[...], TN_DIMS, preferred_element_type=F32)
             + jnp.dot(gla, wob_ref[...], preferred_element_type=F32))
    xmid_ref[...] = x_mid
    h2 = _rms(x_mid, fn_ref[...])
    h2_ref[...] = h2

    h_hi, h_lo = _split_bf16(h2)
    w_hi, w_lo = _split_bf16(rwT_ref[...])
    logits = (lax.dot_general(w_hi, h_hi, NT_DIMS, preferred_element_type=F32)
              + lax.dot_general(w_hi, h_lo, NT_DIMS, preferred_element_type=F32)
              + lax.dot_general(w_lo, h_hi, NT_DIMS, preferred_element_type=F32)
              + rb_ref[...])
    tm = logits.shape[1]
    eidx = lax.broadcasted_iota(jnp.int32, (N_EXPERTS, tm), 0).astype(F32)
    vals, idxs, sels = [], [], []
    cur = logits
    for _ in range(TOP_K):
        mk = jnp.max(cur, axis=0, keepdims=True)
        ik = jnp.min(jnp.where(cur == mk, eidx, float(N_EXPERTS)), axis=0, keepdims=True)
        sel = eidx == ik
        vals.append(mk)
        idxs.append(ik)
        sels.append(sel)
        cur = jnp.where(sel, -jnp.inf, cur)
    exps = [jnp.exp(vk - vals[0]) for vk in vals]
    denom = exps[0] + exps[1] + exps[2] + exps[3]
    gates = [e / denom for e in exps]

    cnt = (sels[0] | sels[1] | sels[2] | sels[3])
    before = jnp.dot(cnt.astype(BF16), upper_ref[...], preferred_element_type=F32) + carry_ref[:, 0:1]
    ranks = [jnp.sum(jnp.where(sel, before, 0.0), axis=0, keepdims=True) for sel in sels]
    carry_ref[...] = carry_ref[...] + jnp.sum(cnt.astype(F32), axis=1, keepdims=True)
    counts_ref[...] = carry_ref[...]

    meta_ref[...] = jnp.concatenate(idxs + ranks, axis=0).astype(jnp.int32)
    gates_ref[...] = jnp.concatenate(gates + [jnp.zeros((TOP_K, tm), F32)], axis=0)


def _post_call(x, oT, o_f, o_b, gr, prm, tm):
    B, S, D = x.shape
    nS = S // tm
    upper = jnp.asarray(np.triu(np.ones((tm, tm), np.float32), k=1), BF16)
    full = lambda a: pl.BlockSpec(a.shape, lambda b, i, _nd=a.ndim: (0,) * _nd)
    tok = lambda w: pl.BlockSpec((None, tm, w), lambda b, i: (b, i, 0))
    colblk = lambda r: pl.BlockSpec((None, r, tm), lambda b, i: (b, 0, i))
    flat = lambda r: pl.BlockSpec((r, tm), lambda b, i: (0, b * nS + i))
    return pl.pallas_call(
        _post_kernel, grid=(B, nS),
        in_specs=[tok(D), colblk(MLA_WIDTH), tok(GLA_WIDTH), tok(GLA_WIDTH), tok(GLA_WIDTH),
                  full(prm["gla_out_norm"]), full(prm["w_out_a"]), full(prm["w_out_b"]),
                  full(prm["ffn_norm"]), full(prm["router_wT"]), full(prm["router_b"]), full(upper)],
        out_specs=[tok(D), tok(D), flat(2 * TOP_K), flat(2 * TOP_K),
                   pl.BlockSpec((N_EXPERTS, LANES), lambda b, i: (0, 0))],
        out_shape=[jax.ShapeDtypeStruct((B, S, D), F32), jax.ShapeDtypeStruct((B, S, D), F32),
                   jax.ShapeDtypeStruct((2 * TOP_K, B * S), jnp.int32),
                   jax.ShapeDtypeStruct((2 * TOP_K, B * S), F32),
                   jax.ShapeDtypeStruct((N_EXPERTS, LANES), F32)],
        scratch_shapes=[pltpu.VMEM((N_EXPERTS, LANES), F32)],
        compiler_params=_cparams(("arbitrary", "arbitrary")), name="post",
    )(x, oT, o_f, o_b, gr, prm["gla_out_norm"], prm["w_out_a"], prm["w_out_b"], prm["ffn_norm"],
      prm["router_wT"], prm["router_b"], upper)


def _slot(starts_ref, meta_ref, kk, r):
    return starts_ref[meta_ref[kk, r]] + meta_ref[TOP_K + kk, r]


def _dispatch_kernel(starts_ref, meta_ref, h2_ref, xs_in_ref, xs_ref, sem):
    del xs_in_ref
    td = h2_ref.shape[0]

    def row_copy(r, kk):
        return pltpu.make_async_copy(h2_ref.at[pl.ds(r, 1), :],
                                     xs_ref.at[pl.ds(_slot(starts_ref, meta_ref, kk, r), 1), :], sem)

    def start(r, c):
        for kk in range(TOP_K):
            row_copy(r, kk).start()
        return c

    def wait(r, c):
        for kk in range(TOP_K):
            row_copy(r, kk).wait()
        return c

    lax.fori_loop(0, td, start, 0)
    lax.fori_loop(0, td, wait, 0)


def _dispatch_call(starts, meta, h2, n_rows, td):
    T, D = h2.shape
    xs0 = jnp.zeros((n_rows, D), F32)
    grid_spec = pltpu.PrefetchScalarGridSpec(
        num_scalar_prefetch=1, grid=(T // td,),
        in_specs=[pl.BlockSpec((2 * TOP_K, td), lambda i, st: (0, i), memory_space=pltpu.SMEM),
                  pl.BlockSpec((td, D), lambda i, st: (i, 0)),
                  pl.BlockSpec(memory_space=pl.ANY)],
        out_specs=pl.BlockSpec(memory_space=pl.ANY),
        scratch_shapes=[pltpu.SemaphoreType.DMA(())])
    return pl.pallas_call(
        _dispatch_kernel, grid_spec=grid_spec,
        out_shape=jax.ShapeDtypeStruct((n_rows, D), F32),
        input_output_aliases={3: 0},
        compiler_params=_cparams(("arbitrary",)), name="dispatch",
    )(starts, meta, h2, xs0)


def _expert_kernel(be_ref, nv_ref, xs_ref, wgu_ref, bg_ref, bl_ref, wdn_ref, bd_ref, perm_ref, ys_ref,
                   wg_s, wl_s, wd_s):
    i = pl.program_id(0)
    valid = i < nv_ref[0]
    new_expert = valid & ((i == 0) | (be_ref[i] != be_ref[jnp.maximum(i - 1, 0)]))

    @pl.when(new_expert)
    def _():
        perm = perm_ref[...]
        half = PERM_GROUP // 2
        for c in range(2 * D_FF // PERM_GROUP):
            w = wgu_ref[:, c * PERM_GROUP:(c + 1) * PERM_GROUP].astype(BF16)
            sep = jnp.dot(w, perm, preferred_element_type=F32).astype(BF16)
            wg_s[:, c * half:(c + 1) * half] = sep[:, :half]
            wl_s[:, c * half:(c + 1) * half] = sep[:, half:]
        wd_s[...] = wdn_ref[...].astype(BF16)

    @pl.when(valid)
    def _():
        xb = xs_ref[...].astype(BF16)
        g = jnp.dot(xb, wg_s[...], preferred_element_type=F32) + bg_ref[...]
        l = jnp.dot(xb, wl_s[...], preferred_element_type=F32) + bl_ref[...]
        glu = jnp.minimum(g, SWIGLU_LIMIT)
        lin = jnp.clip(l, -SWIGLU_LIMIT, SWIGLU_LIMIT)
        act = glu * jax.nn.sigmoid(SWIGLU_ALPHA * glu) * (lin + 1.0)
        ys_ref[...] = jnp.dot(act.astype(BF16), wd_s[...], preferred_element_type=F32) + bd_ref[...]

    @pl.when(jnp.logical_not(valid))
    def _():
        ys_ref[...] = jnp.zeros_like(ys_ref)


def _expert_call(block_expert, n_valid, xs, prm):
    n_rows, D = xs.shape
    n_blocks = n_rows // MOE_BLOCK
    half = PERM_GROUP // 2
    src = np.concatenate([2 * np.arange(half), 2 * np.arange(half) + 1])
    perm = jnp.asarray(np.arange(PERM_GROUP)[:, None] == src[None, :], BF16)
    rows = lambda i, be, nv: (jnp.minimum(i, nv[0] - 1), 0)
    wsel = lambda i, be, nv: (be[i], 0, 0)
    grid_spec = pltpu.PrefetchScalarGridSpec(
        num_scalar_prefetch=2, grid=(n_blocks,),
        in_specs=[pl.BlockSpec((MOE_BLOCK, D), rows),
                  pl.BlockSpec((None, D, 2 * D_FF), wsel),
                  pl.BlockSpec((None, 1, D_FF), wsel), pl.BlockSpec((None, 1, D_FF), wsel),
                  pl.BlockSpec((None, D_FF, D), wsel), pl.BlockSpec((None, 1, D), wsel),
                  pl.BlockSpec((PERM_GROUP, PERM_GROUP), lambda i, be, nv: (0, 0))],
        out_specs=pl.BlockSpec((MOE_BLOCK, D), lambda i, be, nv: (i, 0)),
        scratch_shapes=[pltpu.VMEM((D, D_FF), BF16), pltpu.VMEM((D, D_FF), BF16),
                        pltpu.VMEM((D_FF, D), BF16)])
    return pl.pallas_call(
        _expert_kernel, grid_spec=grid_spec,
        out_shape=jax.ShapeDtypeStruct((n_rows, D), F32),
        compiler_params=_cparams(("arbitrary",)), name="experts",
    )(block_expert, n_valid, xs, prm["w_gu"], prm["b_glu"], prm["b_lin"], prm["w_dn"], prm["b_dn"], perm)


def _combine_kernel(starts_ref, meta_ref, xmid_ref, gates_ref, fnorm_ref, ys_ref, out_ref, buf_ref, sem):
    tc = xmid_ref.shape[0]

    def row_copy(r, kk):
        return pltpu.make_async_copy(ys_ref.at[pl.ds(_slot(starts_ref, meta_ref, kk, r), 1), :],
                                     buf_ref.at[kk, pl.ds(r, 1), :], sem)

    def start(r, c):
        for kk in range(TOP_K):
            row_copy(r, kk).start()
        return c

    def wait(r, c):
        for kk in range(TOP_K):
            row_copy(r, kk).wait()
        return c

    lax.fori_loop(0, tc, start, 0)
    g = jnp.transpose(gates_ref[...])
    lax.fori_loop(0, tc, wait, 0)
    y = xmid_ref[...]
    for kk in range(TOP_K):
        y = y + buf_ref[kk] * g[:, kk:kk + 1]
    out_ref[...] = _rms(y, fnorm_ref[...])


def _combine_call(starts, meta, x_mid, gates, final_norm, ys, tc):
    T, D = x_mid.shape
    grid_spec = pltpu.PrefetchScalarGridSpec(
        num_scalar_prefetch=1, grid=(T // tc,),
        in_specs=[pl.BlockSpec((2 * TOP_K, tc), lambda i, st: (0, i), memory_space=pltpu.SMEM),
                  pl.BlockSpec((tc, D), lambda i, st: (i, 0)),
                  pl.BlockSpec((2 * TOP_K, tc), lambda i, st: (0, i)),
                  pl.BlockSpec(final_norm.shape, lambda i, st: (0, 0)),
                  pl.BlockSpec(memory_space=pl.ANY)],
        out_specs=pl.BlockSpec((tc, D), lambda i, st: (i, 0)),
        scratch_shapes=[pltpu.VMEM((TOP_K, tc, D), F32), pltpu.SemaphoreType.DMA(())])
    return pl.pallas_call(
        _combine_kernel, grid_spec=grid_spec,
        out_shape=jax.ShapeDtypeStruct((T, D), F32),
        compiler_params=_cparams(("arbitrary",)), name="combine",
    )(starts, meta, x_mid, gates, final_norm, ys)


def _prep_params(attn_norm, w_in, mla_q_norm, mla_w_uq, mla_kv_norm, mla_w_ukv, gla_w_gate_fwd,
                 gla_b_gate_fwd, gla_w_gate_bwd, gla_b_gate_bwd, gla_out_norm, w_out, ffn_norm,
                 router_w, router_b, w_gu, b_gu, w_dn, b_dn):
    D = D_MODEL
    o = np.cumsum((0, MLA_Q_LORA, MLA_KV_LORA, MLA_ROPE, GLA_KDIM, GLA_KDIM, GLA_WIDTH,
                   2 * GLA_GATE_RANK, GLA_WIDTH))
    seg = [w_in[:, o[n]:o[n + 1]] for n in range(8)]
    z = lambda n: jnp.zeros((D, n), w_in.dtype)
    w_in_p = jnp.concatenate(
        [seg[0], seg[1], z(MLA_NOPE), seg[2], z(LANES - MLA_NOPE - MLA_ROPE), seg[3], seg[4], seg[5],
         seg[6], z(LANES - 2 * GLA_GATE_RANK), seg[7]], axis=1).astype(BF16)

    wq = mla_w_uq.reshape(MLA_Q_LORA, MLA_HEADS, MLA_NOPE + MLA_ROPE)
    w_uq_p = jnp.concatenate(
        [wq[:, :, :MLA_NOPE].reshape(MLA_Q_LORA, -1),
         wq[:, :, MLA_NOPE:MLA_NOPE + HALF_ROPE].reshape(MLA_Q_LORA, -1),
         wq[:, :, MLA_NOPE + HALF_ROPE:].reshape(MLA_Q_LORA, -1)], axis=1)
    wkv = mla_w_ukv.reshape(MLA_KV_LORA, MLA_HEADS, MLA_NOPE + MLA_V)
    w_k = jnp.concatenate([wkv[:, :, :MLA_NOPE], jnp.zeros((MLA_KV_LORA, MLA_HEADS, QK_PAD - MLA_NOPE),
                                                            wkv.dtype)], axis=2)
    w_v = wkv[:, :, MLA_NOPE:].reshape(MLA_KV_LORA, -1)

    def gate_w(w, row0):
        full = jnp.zeros((LANES, GLA_KDIM), w.dtype)
        return full.at[row0:row0 + GLA_GATE_RANK].set(w).astype(BF16)

    E = N_EXPERTS
    return {
        "attn_norm": attn_norm.reshape(1, D), "w_in": w_in_p,
        "q_norm": mla_q_norm.reshape(1, -1), "w_uqT": w_uq_p.T.astype(BF16),
        "kv_norm": mla_kv_norm.reshape(1, -1),
        "w_k": w_k.reshape(MLA_KV_LORA, -1).astype(BF16), "w_vT": w_v.T.astype(BF16),
        "w_gf": gate_w(gla_w_gate_fwd, 0), "b_gf": gla_b_gate_fwd.reshape(1, -1),
        "w_gb": gate_w(gla_w_gate_bwd, GLA_GATE_RANK), "b_gb": gla_b_gate_bwd.reshape(1, -1),
        "gla_out_norm": gla_out_norm.reshape(1, -1),
        "w_out_a": w_out[:MLA_WIDTH].astype(BF16), "w_out_b": w_out[MLA_WIDTH:].astype(BF16),
        "ffn_norm": ffn_norm.reshape(1, D),
        "router_wT": router_w.T, "router_b": router_b.reshape(E, 1),
        "w_gu": w_gu, "w_dn": w_dn,
        "b_glu": b_gu[:, 0::2].reshape(E, 1, D_FF), "b_lin": b_gu[:, 1::2].reshape(E, 1, D_FF),
        "b_dn": b_dn.reshape(E, 1, D),
    }


def _rope_tables(S):
    inv_freq = jnp.power(ROPE_THETA, -jnp.arange(0, MLA_ROPE, 2, dtype=F32) / MLA_ROPE)
    ang = jnp.arange(S, dtype=F32)[:, None] * inv_freq[None, :]
    cos, sin = jnp.cos(ang), jnp.sin(ang)
    z = lambda n: jnp.zeros((S, n), F32)
    tail = LANES - MLA_NOPE - MLA_ROPE
    return {
        "c": jnp.concatenate([z(MLA_NOPE), cos, cos, z(tail)], axis=1),
        "s1": jnp.concatenate([z(MLA_NOPE), -sin, z(HALF_ROPE), z(tail)], axis=1),
        "s2": jnp.concatenate([z(MLA_NOPE), z(HALF_ROPE), sin, z(tail)], axis=1),
        "cosT": jnp.tile(cos.T, (MLA_HEADS, 1)), "sinT": jnp.tile(sin.T, (MLA_HEADS, 1)),
    }


def _encoder(x, prm, final_norm):
    B, S, D = x.shape
    rope = _rope_tables(S)
    qT, k, vT, gq, gk, gv, laf, lab, gr = _pre_call(x, prm, rope, ATTN_KEY_TILE)
    oT = _attn_call(qT, k, vT, min(ATTN_QUERY_TILE, S)).reshape(B, MLA_WIDTH, S)
    o_f, o_b = _gla_call(gq, gk, gv, laf, lab)
    x_mid, h2, meta, gates, counts = _post_call(x, oT, o_f, o_b, gr, prm, min(256, S))

    T = B * S
    n_rows = T * TOP_K + N_EXPERTS * MOE_BLOCK
    n_blocks = n_rows // MOE_BLOCK
    cnt = counts[:, 0].astype(jnp.int32)
    padded = ((cnt + MOE_BLOCK - 1) // MOE_BLOCK) * MOE_BLOCK
    pends = jnp.cumsum(padded)
    starts = (pends - padded).astype(jnp.int32)
    n_valid = (pends[-1] // MOE_BLOCK).astype(jnp.int32).reshape(1)
    blk = jnp.minimum(jnp.arange(n_blocks, dtype=jnp.int32), n_valid[0] - 1) * MOE_BLOCK
    block_expert = jnp.minimum(jnp.sum(pends[None, :] <= blk[:, None], axis=1), N_EXPERTS - 1).astype(jnp.int32)

    xs = _dispatch_call(starts, meta, h2.reshape(T, D), n_rows, min(128, S))
    ys = _expert_call(block_expert, n_valid, xs, prm)
    y = _combine_call(starts, meta, x_mid.reshape(T, D), gates, final_norm.reshape(1, D), ys, min(128, S))
    return y.reshape(B, S, D)


def kernel(x_prompt, x_sample, attn_norm, w_in, mla_q_norm, mla_w_uq, mla_kv_norm, mla_w_ukv,
           gla_w_gate_fwd, gla_b_gate_fwd, gla_w_gate_bwd, gla_b_gate_bwd, gla_out_norm, w_out, ffn_norm,
           router_w, router_b, expert_w_gate_up, expert_b_gate_up, expert_w_down, expert_b_down,
           final_norm):
    layer = (attn_norm, w_in, mla_q_norm, mla_w_uq, mla_kv_norm, mla_w_ukv, gla_w_gate_fwd,
             gla_b_gate_fwd, gla_w_gate_bwd, gla_b_gate_bwd, gla_out_norm, w_out, ffn_norm, router_w,
             router_b, expert_w_gate_up, expert_b_gate_up, expert_w_down, expert_b_down)
    assert all(p.shape[0] == 1 for p in layer), "single layer expected"
    assert x_prompt.shape[1:] == x_sample.shape[1:]
    prm = _prep_params(*[p[0] for p in layer])
    nb = x_prompt.shape[0]
    y = _encoder(jnp.concatenate([x_prompt, x_sample], axis=0), prm, final_norm)
    return y[:nb], y[nb:]
```

```python
import functools

import jax
import jax.numpy as jnp
import numpy as np
from jax import lax
from jax.experimental import pallas as pl
from jax.experimental.pallas import tpu as pltpu

F32 = jnp.float32
BF16 = jnp.bfloat16

D_MODEL = 1024
MLA_HEADS = 8
MLA_NOPE = 64
MLA_ROPE = 32
MLA_V = 64
MLA_Q_LORA = 384
MLA_KV_LORA = 256
ROPE_THETA = 10000.0
GLA_HEADS = 4
GLA_DK = 64
GLA_DV = 128
GLA_GATE_RANK = 16
GLA_GATE_NORM = 16.0
N_EXPERTS = 32
TOP_K = 4
D_FF = 1024
SWIGLU_LIMIT = 7.0
SWIGLU_ALPHA = 1.702
MOE_BLOCK = 512
RMS_EPS = 1e-6

MLA_WIDTH = MLA_HEADS * MLA_V
GLA_WIDTH = GLA_HEADS * GLA_DV
GLA_KDIM = GLA_HEADS * GLA_DK
HALF_ROPE = MLA_ROPE // 2
QK_PAD = 128
LANES = 128
V_ONES_ROWS = 16

OFF_CQ = 0
OFF_CKV = OFF_CQ + MLA_Q_LORA
OFF_KR = OFF_CKV + MLA_KV_LORA
OFF_GQ = OFF_KR + LANES
OFF_GK = OFF_GQ + GLA_KDIM
OFF_GV = OFF_GK + GLA_KDIM
OFF_LR = OFF_GV + GLA_WIDTH
OFF_GR = OFF_LR + LANES
PROJ_COLS = OFF_GR + GLA_WIDTH

ATTN_KEY_TILE = 256
ATTN_QUERY_TILE = 512
ATTN_UNROLL = 32
ATTN_SCORE_SLOTS = 4
GLA_TILE = 256
GLA_LEVELS = 8
PERM_GROUP = 256
VMEM_LIMIT = 56 * 1024 * 1024

LOG2_E = 1.4426950408889634
NT_DIMS = (((1,), (1,)), ((), ()))
TN_DIMS = (((0,), (0,)), ((), ()))


def _cparams(sem):
    return pltpu.CompilerParams(dimension_semantics=sem, vmem_limit_bytes=VMEM_LIMIT)


def _rms(x, gain):
    return x * lax.rsqrt(jnp.mean(x * x, axis=-1, keepdims=True) + RMS_EPS) * gain


def _split_bf16(x):
    hi = x.astype(BF16)
    lo = (x - hi.astype(F32)).astype(BF16)
    return hi, lo


def _pre_kernel(xa_ref, xb_ref, an_ref, win_ref, qn_ref, wuqT_ref, kvn_ref, wk_ref, wvT_ref,
                wgf_ref, bgf_ref, wgb_ref, bgb_ref, rc_ref, rs1_ref, rs2_ref, cosT_ref, sinT_ref,
                qT_ref, k_ref, vT_ref, gq_ref, gk_ref, gv_ref, laf_ref, lab_ref, gr_ref, *, n_first):
    x = jnp.where(pl.program_id(0) < n_first, xa_ref[...], xb_ref[...])
    h = _rms(x, an_ref[...]).astype(BF16)
    proj = jnp.dot(h, win_ref[...], preferred_element_type=F32)

    cqn = _rms(proj[:, OFF_CQ:OFF_CQ + MLA_Q_LORA], qn_ref[...]).astype(BF16)
    ckvn = _rms(proj[:, OFF_CKV:OFF_CKV + MLA_KV_LORA], kvn_ref[...]).astype(BF16)

    scale = (MLA_NOPE + MLA_ROPE) ** -0.5 * LOG2_E
    qT = lax.dot_general(wuqT_ref[...], cqn, NT_DIMS, preferred_element_type=F32) * scale
    n_nope = MLA_HEADS * MLA_NOPE
    n_half = MLA_HEADS * HALF_ROPE
    x1 = qT[n_nope:n_nope + n_half]
    x2 = qT[n_nope + n_half:n_nope + 2 * n_half]
    c = cosT_ref[...]
    s = sinT_ref[...]
    x1r = x1 * c - x2 * s
    x2r = x1 * s + x2 * c
    zpad = jnp.zeros((QK_PAD - MLA_NOPE - MLA_ROPE, qT.shape[1]), BF16)
    for hd in range(MLA_HEADS):
        qT_ref[hd, 0:MLA_NOPE, :] = qT[hd * MLA_NOPE:(hd + 1) * MLA_NOPE].astype(BF16)
        qT_ref[hd, MLA_NOPE:MLA_NOPE + HALF_ROPE, :] = x1r[hd * HALF_ROPE:(hd + 1) * HALF_ROPE].astype(BF16)
        qT_ref[hd, MLA_NOPE + HALF_ROPE:MLA_NOPE + MLA_ROPE, :] = (
            x2r[hd * HALF_ROPE:(hd + 1) * HALF_ROPE].astype(BF16))
        qT_ref[hd, MLA_NOPE + MLA_ROPE:QK_PAD, :] = zpad

    kr = proj[:, OFF_KR:OFF_KR + LANES]
    kr = (kr * rc_ref[...]
          + pltpu.roll(kr, LANES - HALF_ROPE, axis=1) * rs1_ref[...]
          + pltpu.roll(kr, HALF_ROPE, axis=1) * rs2_ref[...])
    kfull = jnp.dot(ckvn, wk_ref[...], preferred_element_type=F32)
    for hd in range(MLA_HEADS):
        k_ref[hd] = (kfull[:, hd * QK_PAD:(hd + 1) * QK_PAD] + kr).astype(BF16)

    vT = lax.dot_general(wvT_ref[...], ckvn, NT_DIMS, preferred_element_type=F32)
    for hd in range(MLA_HEADS):
        vT_ref[hd] = vT[hd * MLA_V:(hd + 1) * MLA_V].astype(BF16)

    gq_ref[...] = proj[:, OFF_GQ:OFF_GQ + GLA_KDIM] * (GLA_DK ** -0.5)
    gk_ref[...] = proj[:, OFF_GK:OFF_GK + GLA_KDIM]
    gv_ref[...] = proj[:, OFF_GV:OFF_GV + GLA_WIDTH]
    gr_ref[...] = proj[:, OFF_GR:OFF_GR + GLA_WIDTH]
    lr = proj[:, OFF_LR:OFF_LR + LANES].astype(BF16)

    def log_decay(w_ref, b_ref):
        z = jnp.dot(lr, w_ref[...], preferred_element_type=F32) + b_ref[...]
        return (jnp.minimum(z, 0.0) - jnp.log1p(jnp.exp(-jnp.abs(z)))) * (1.0 / GLA_GATE_NORM)

    laf_ref[...] = log_decay(wgf_ref, bgf_ref)
    lab_ref[...] = log_decay(wgb_ref, bgb_ref)


def _two_source_specs(n_first, tm, width):
    return [pl.BlockSpec((None, tm, width), lambda b, i: (jnp.minimum(b, n_first - 1), i, 0)),
            pl.BlockSpec((None, tm, width), lambda b, i: (jnp.maximum(b - n_first, 0), i, 0))]


def _pre_call(xa, xb, prm, rope, tm):
    n_first, S, D = xa.shape
    B = n_first + xb.shape[0]
    nS = S // tm
    H = MLA_HEADS

    def full(a):
        nd = a.ndim
        return pl.BlockSpec(a.shape, lambda b, i, _nd=nd: (0,) * _nd)

    tok = lambda w: pl.BlockSpec((None, tm, w), lambda b, i: (b, i, 0))
    in_specs = [
        *_two_source_specs(n_first, tm, D), full(prm["attn_norm"]), full(prm["w_in"]), full(prm["q_norm"]), full(prm["w_uqT"]),
        full(prm["kv_norm"]), full(prm["w_k"]), full(prm["w_vT"]),
        full(prm["w_gf"]), full(prm["b_gf"]), full(prm["w_gb"]), full(prm["b_gb"]),
        pl.BlockSpec((tm, LANES), lambda b, i: (i, 0)),
        pl.BlockSpec((tm, LANES), lambda b, i: (i, 0)),
        pl.BlockSpec((tm, LANES), lambda b, i: (i, 0)),
        pl.BlockSpec((H * HALF_ROPE, tm), lambda b, i: (0, i)),
        pl.BlockSpec((H * HALF_ROPE, tm), lambda b, i: (0, i)),
    ]
    out_shape = [
        jax.ShapeDtypeStruct((B, H, QK_PAD, S), BF16),
        jax.ShapeDtypeStruct((B, H, S, QK_PAD), BF16),
        jax.ShapeDtypeStruct((B, H, nS, MLA_V, tm), BF16),
        jax.ShapeDtypeStruct((B, S, GLA_KDIM), F32),
        jax.ShapeDtypeStruct((B, S, GLA_KDIM), F32),
        jax.ShapeDtypeStruct((B, S, GLA_WIDTH), F32),
        jax.ShapeDtypeStruct((B, S, GLA_KDIM), F32),
        jax.ShapeDtypeStruct((B, S, GLA_KDIM), F32),
        jax.ShapeDtypeStruct((B, S, GLA_WIDTH), F32),
    ]
    out_specs = [
        pl.BlockSpec((None, H, QK_PAD, tm), lambda b, i: (b, 0, 0, i)),
        pl.BlockSpec((None, H, tm, QK_PAD), lambda b, i: (b, 0, i, 0)),
        pl.BlockSpec((None, H, None, MLA_V, tm), lambda b, i: (b, 0, i, 0, 0)),
        tok(GLA_KDIM), tok(GLA_KDIM), tok(GLA_WIDTH), tok(GLA_KDIM), tok(GLA_KDIM), tok(GLA_WIDTH),
    ]
    return pl.pallas_call(
        functools.partial(_pre_kernel, n_first=n_first), grid=(B, nS), in_specs=in_specs, out_specs=out_specs, out_shape=out_shape,
        compiler_params=_cparams(("parallel", "parallel")), name="pre",
    )(xa, xb, prm["attn_norm"], prm["w_in"], prm["q_norm"], prm["w_uqT"], prm["kv_norm"], prm["w_k"],
      prm["w_vT"], prm["w_gf"], prm["b_gf"], prm["w_gb"], prm["b_gb"],
      rope["c"], rope["s1"], rope["s2"], rope["cosT"], rope["sinT"])


def _attn_kernel(qT_ref, k_ref, vT_ref, o_ref, s_ref, p_ref, *, n_kblk, unroll):
    qT = qT_ref[...]
    tq = qT.shape[1]
    tk = k_ref.shape[0] // n_kblk
    ones = jnp.ones((V_ONES_ROWS, tk), BF16)

    def scores(j, slot):
        kb = k_ref[pl.ds(pl.multiple_of(j * tk, tk), tk), :]
        sT = jnp.dot(kb, qT, preferred_element_type=F32)
        s_ref[slot] = sT
        return jnp.max(sT, axis=0, keepdims=True)

    def values(j, slot):
        vext = jnp.concatenate([vT_ref[j], ones], axis=0)
        return jnp.dot(vext, p_ref[slot], preferred_element_type=F32)

    def softmax(s_slot, p_slot, blk_max, m):
        m_new = jnp.maximum(m, blk_max)
        p_ref[p_slot] = jnp.exp2(s_ref[s_slot] - m_new).astype(BF16)
        return m_new, jnp.exp2(m - m_new)

    def body(jj, carry):
        max_0, max_1, m, alpha_prev, acc = carry
        for u in range(unroll):
            j = unroll * jj + u
            pv = values(jnp.maximum(j - 1, 0), (u + 1) % 2)
            max_2 = scores(jnp.minimum(j + 2, n_kblk - 1), (u + 2) % ATTN_SCORE_SLOTS)
            m, alpha = softmax(u % ATTN_SCORE_SLOTS, u % 2, max_0, m)
            acc = alpha_prev * acc + pv
            max_0, max_1, alpha_prev = max_1, max_2, alpha
        return max_0, max_1, m, alpha_prev, acc

    p_ref[1] = jnp.zeros(p_ref.shape[1:], BF16)
    m0 = jnp.full((1, tq), -jnp.inf, F32)
    acc0 = jnp.zeros((MLA_V + V_ONES_ROWS, tq), F32)
    carry0 = (scores(0, 0), scores(1, 1), m0, jnp.ones((1, tq), F32), acc0)
    _, _, _, alpha, acc = lax.fori_loop(0, n_kblk // unroll, body, carry0)
    acc = alpha * acc + values(n_kblk - 1, 1)
    o_ref[...] = (acc[0:MLA_V] / acc[MLA_V:MLA_V + 1]).astype(o_ref.dtype)


def _attn_call(qT, k, vT, tq):
    B, H, _, S = qT.shape
    n_kblk, tk = vT.shape[2], vT.shape[4]
    unroll = min(ATTN_UNROLL, n_kblk)
    assert unroll % ATTN_SCORE_SLOTS == 0 and n_kblk % unroll == 0 and S % tq == 0
    return pl.pallas_call(
        functools.partial(_attn_kernel, n_kblk=n_kblk, unroll=unroll),
        grid=(B, H, S // tq),
        in_specs=[
            pl.BlockSpec((None, None, QK_PAD, tq), lambda b, h, i: (b, h, 0, i)),
            pl.BlockSpec((None, None, S, QK_PAD), lambda b, h, i: (b, h, 0, 0)),
            pl.BlockSpec((None, None, n_kblk, MLA_V, tk), lambda b, h, i: (b, h, 0, 0, 0)),
        ],
        out_specs=pl.BlockSpec((None, None, MLA_V, tq), lambda b, h, i: (b, h, 0, i)),
        out_shape=jax.ShapeDtypeStruct((B, H, MLA_V, S), BF16),
        scratch_shapes=[pltpu.VMEM((ATTN_SCORE_SLOTS, tk, tq), F32), pltpu.VMEM((2, tk, tq), BF16)],
        compiler_params=_cparams(("parallel", "parallel", "parallel")), name="attn",
    )(qT, k, vT)


def _gla_consts():
    L = GLA_TILE
    i = np.arange(L)[:, None]
    j = np.arange(L)[None, :]
    x = i ^ j
    lidx = np.where(x > 0, np.floor(np.log2(np.maximum(x, 1))), -1).astype(np.int32)
    lidx_f = np.where(i > j, lidx, -1).astype(np.int32)
    lidx_b = np.where(i < j, lidx, -1).astype(np.int32)
    hd = np.arange(GLA_KDIM)[:, None] // GLA_DK
    hv = np.arange(GLA_WIDTH)[None, :] // GLA_DV
    bexp = (hd == hv).astype(np.float32)
    return jnp.asarray(lidx_f), jnp.asarray(lidx_b), jnp.asarray(bexp, BF16)


def _gla_direction(q, k, v, la, lidx, bexp_ref, ssum_ref, state_ref, o_ref, forward):
    L = GLA_TILE
    la_hi, la_lo = _split_bf16(la)
    vb = v.astype(BF16)
    lane_head = lax.broadcasted_iota(jnp.int32, (1, GLA_KDIM), 1) // GLA_DK
    head_masks = [(lane_head == h).astype(BF16) for h in range(GLA_HEADS)]
    row = lax.broadcasted_iota(jnp.int32, la.shape, 0)

    def widen(c, t, m):
        upper_half = (row & m) != 0
        sibling_total = jnp.where(upper_half, pltpu.roll(t, m, axis=0), pltpu.roll(t, L - m, axis=0))
        return c + jnp.where(upper_half, sibling_total, 0.0), t + sibling_total

    ssum_ref[...] = jnp.zeros_like(ssum_ref)
    c, t = la, la
    for lvl in range(GLA_LEVELS):
        if lvl > 0:
            c, t = widen(c, t, 1 << (lvl - 1))
        if forward:
            eq, ek = c, t - c
        else:
            eq, ek = t - c + la, c - la
        ql = (q * jnp.exp(eq)).astype(BF16)
        kl = (k * jnp.exp(ek)).astype(BF16)
        sel = lidx == lvl
        for h in range(GLA_HEADS):
            sc = lax.dot_general(ql * head_masks[h], kl, NT_DIMS, preferred_element_type=F32)
            ssum_ref[h] = jnp.where(sel, sc, ssum_ref[h])

    c, t = widen(c, t, L // 2)
    if forward:
        eq, ek = c, t - c
    else:
        eq, ek = t - c + la, c - la
    q_in = (q * jnp.exp(eq)).astype(BF16)
    k_out = (k * jnp.exp(ek)).astype(BF16)
    o = jnp.dot(q_in, state_ref[...].astype(BF16), preferred_element_type=F32)
    if forward:
        o = o + jnp.dot((q * k).astype(BF16), bexp_ref[...], preferred_element_type=F32) * v
    for h in range(GLA_HEADS):
        oh = jnp.dot(ssum_ref[h].astype(BF16), vb[:, h * GLA_DV:(h + 1) * GLA_DV],
                     preferred_element_type=F32)
        o_ref[:, h * GLA_DV:(h + 1) * GLA_DV] = o[:, h * GLA_DV:(h + 1) * GLA_DV] + oh

    ones = jnp.ones((L, LANES), BF16)
    tot_col = (lax.dot_general(la_hi, ones, TN_DIMS, preferred_element_type=F32)
               + lax.dot_general(la_lo, ones, TN_DIMS, preferred_element_type=F32))
    dec = jnp.exp(tot_col)
    upd = lax.dot_general(k_out, vb, TN_DIMS, preferred_element_type=F32)
    for h in range(GLA_HEADS):
        cols = slice(h * GLA_DV, (h + 1) * GLA_DV)
        state_ref[:, cols] = dec * state_ref[:, cols] + upd[:, cols] * bexp_ref[:, cols].astype(F32)


def _gla_kernel(qf_ref, kf_ref, vf_ref, laf_ref, qb_ref, kb_ref, vb_ref, lab_ref,
                lidxf_ref, lidxb_ref, bexp_ref, of_ref, ob_ref, sf_ref, sb_ref, ssum_ref):
    @pl.when(pl.program_id(1) == 0)
    def _():
        sf_ref[...] = jnp.zeros_like(sf_ref)
        sb_ref[...] = jnp.zeros_like(sb_ref)

    _gla_direction(qf_ref[...], kf_ref[...], vf_ref[...], laf_ref[...], lidxf_ref[...],
                   bexp_ref, ssum_ref, sf_ref, of_ref, True)
    _gla_direction(qb_ref[...], kb_ref[...], vb_ref[...], lab_ref[...], lidxb_ref[...],
                   bexp_ref, ssum_ref, sb_ref, ob_ref, False)


def _gla_call(gq, gk, gv, laf, lab):
    B, S, _ = gq.shape
    L = GLA_TILE
    n = S // L
    lidx_f, lidx_b, bexp = _gla_consts()
    fwd = lambda w: pl.BlockSpec((None, L, w), lambda b, i: (b, i, 0))
    bwd = lambda w: pl.BlockSpec((None, L, w), lambda b, i: (b, n - 1 - i, 0))
    const = lambda a: pl.BlockSpec(a.shape, lambda b, i, _nd=a.ndim: (0,) * _nd)
    return pl.pallas_call(
        _gla_kernel, grid=(B, n),
        in_specs=[fwd(GLA_KDIM), fwd(GLA_KDIM), fwd(GLA_WIDTH), fwd(GLA_KDIM),
                  bwd(GLA_KDIM), bwd(GLA_KDIM), bwd(GLA_WIDTH), bwd(GLA_KDIM),
                  const(lidx_f), const(lidx_b), const(bexp)],
        out_specs=[fwd(GLA_WIDTH), bwd(GLA_WIDTH)],
        out_shape=[jax.ShapeDtypeStruct((B, S, GLA_WIDTH), F32)] * 2,
        scratch_shapes=[pltpu.VMEM((GLA_KDIM, GLA_WIDTH), F32), pltpu.VMEM((GLA_KDIM, GLA_WIDTH), F32),
                        pltpu.VMEM((GLA_HEADS, L, L), F32)],
        compiler_params=_cparams(("parallel", "arbitrary")), name="gla",
    )(gq, gk, gv, laf, gq, gk, gv, lab, lidx_f, lidx_b, bexp)


def _post_kernel(xa_ref, xb_ref, oT_ref, of_ref, ob_ref, gr_ref, gon_ref, woa_ref, wob_ref, fn_ref,
                 rwT_ref, rb_ref, upper_ref,
                 xmid_ref, h2_ref, meta_ref, gates_ref, counts_ref, carry_ref, *, n_first):
    first = (pl.program_id(0) == 0) & (pl.program_id(1) == 0)

    @pl.when(first)
    def _():
        carry_ref[...] = jnp.zeros_like(carry_ref)

    o = of_ref[...] + ob_ref[...]
    gr = gr_ref[...]
    parts = []
    for h in range(GLA_HEADS):
        cols = slice(h * GLA_DV, (h + 1) * GLA_DV)
        parts.append(_rms(o[:, cols], gon_ref[...]) * jax.nn.silu(gr[:, cols]))
    gla = jnp.concatenate(parts, axis=1).astype(BF16)

    x_mid = (jnp.where(pl.program_id(0) < n_first, xa_ref[...], xb_ref[...])
             + lax.dot_general(oT_ref[...], woa_ref[...], TN_DIMS, preferred_element_type=F32)
             + jnp.dot(gla, wob_ref[...], preferred_element_type=F32))
    xmid_ref[...] = x_mid
    h2 = _rms(x_mid, fn_ref[...])
    h2_ref[...] = h2

    h_hi, h_lo = _split_bf16(h2)
    w_hi, w_lo = _split_bf16(rwT_ref[...])
    logits = (lax.dot_general(w_hi, h_hi, NT_DIMS, preferred_element_type=F32)
              + lax.dot_general(w_hi, h_lo, NT_DIMS, preferred_element_type=F32)
              + lax.dot_general(w_lo, h_hi, NT_DIMS, preferred_element_type=F32)
              + rb_ref[...])
    tm = logits.shape[1]
    eidx = lax.broadcasted_iota(jnp.int32, (N_EXPERTS, tm), 0).astype(F32)
    vals, idxs, sels = [], [], []
    cur = logits
    for _ in range(TOP_K):
        mk = jnp.max(cur, axis=0, keepdims=True)
        ik = jnp.min(jnp.where(cur == mk, eidx, float(N_EXPERTS)), axis=0, keepdims=True)
        sel = eidx == ik
        vals.append(mk)
        idxs.append(ik)
        sels.append(sel)
        cur = jnp.where(sel, -jnp.inf, cur)
    exps = [jnp.exp(vk - vals[0]) for vk in vals]
    denom = exps[0] + exps[1] + exps[2] + exps[3]
    gates = [e / denom for e in exps]

    cnt = (sels[0] | sels[1] | sels[2] | sels[3])
    before = jnp.dot(cnt.astype(BF16), upper_ref[...], preferred_element_type=F32) + carry_ref[:, 0:1]
    ranks = [jnp.sum(jnp.where(sel, before, 0.0), axis=0, keepdims=True) for sel in sels]
    carry_ref[...] = carry_ref[...] + jnp.sum(cnt.astype(F32), axis=1, keepdims=True)
    counts_ref[...] = carry_ref[...]

    meta_ref[...] = jnp.concatenate(idxs + ranks, axis=0).astype(jnp.int32)
    gates_ref[...] = jnp.concatenate(gates + [jnp.zeros((TOP_K, tm), F32)], axis=0)


def _post_call(xa, xb, oT, o_f, o_b, gr, prm, tm):
    n_first, S, D = xa.shape
    B = n_first + xb.shape[0]
    nS = S // tm
    upper = jnp.asarray(np.triu(np.ones((tm, tm), np.float32), k=1), BF16)
    full = lambda a: pl.BlockSpec(a.shape, lambda b, i, _nd=a.ndim: (0,) * _nd)
    tok = lambda w: pl.BlockSpec((None, tm, w), lambda b, i: (b, i, 0))
    colblk = lambda r: pl.BlockSpec((None, r, tm), lambda b, i: (b, 0, i))
    flat = lambda r: pl.BlockSpec((r, tm), lambda b, i: (0, b * nS + i))
    return pl.pallas_call(
        functools.partial(_post_kernel, n_first=n_first), grid=(B, nS),
        in_specs=[*_two_source_specs(n_first, tm, D), colblk(MLA_WIDTH), tok(GLA_WIDTH), tok(GLA_WIDTH), tok(GLA_WIDTH),
                  full(prm["gla_out_norm"]), full(prm["w_out_a"]), full(prm["w_out_b"]),
                  full(prm["ffn_norm"]), full(prm["router_wT"]), full(prm["router_b"]), full(upper)],
        out_specs=[tok(D), tok(D), flat(2 * TOP_K), flat(2 * TOP_K),
                   pl.BlockSpec((N_EXPERTS, LANES), lambda b, i: (0, 0))],
        out_shape=[jax.ShapeDtypeStruct((B, S, D), F32), jax.ShapeDtypeStruct((B, S, D), F32),
                   jax.ShapeDtypeStruct((2 * TOP_K, B * S), jnp.int32),
                   jax.ShapeDtypeStruct((2 * TOP_K, B * S), F32),
                   jax.ShapeDtypeStruct((N_EXPERTS, LANES), F32)],
        scratch_shapes=[pltpu.VMEM((N_EXPERTS, LANES), F32)],
        compiler_params=_cparams(("arbitrary", "arbitrary")), name="post",
    )(xa, xb, oT, o_f, o_b, gr, prm["gla_out_norm"], prm["w_out_a"], prm["w_out_b"], prm["ffn_norm"],
      prm["router_wT"], prm["router_b"], upper)


def _slot(starts_ref, meta_ref, kk, r):
    return starts_ref[meta_ref[kk, r]] + meta_ref[TOP_K + kk, r]


def _dispatch_kernel(starts_ref, meta_ref, h2_ref, xs_in_ref, xs_ref, sem):
    del xs_in_ref
    td = h2_ref.shape[0]

    def row_copy(r, kk):
        return pltpu.make_async_copy(h2_ref.at[pl.ds(r, 1), :],
                                     xs_ref.at[pl.ds(_slot(starts_ref, meta_ref, kk, r), 1), :], sem)

    def start(r, c):
        for kk in range(TOP_K):
            row_copy(r, kk).start()
        return c

    def wait(r, c):
        for kk in range(TOP_K):
            row_copy(r, kk).wait()
        return c

    lax.fori_loop(0, td, start, 0)
    lax.fori_loop(0, td, wait, 0)


def _dispatch_call(starts, meta, h2, n_rows, td):
    T, D = h2.shape
    xs0 = jnp.zeros((n_rows, D), F32)
    grid_spec = pltpu.PrefetchScalarGridSpec(
        num_scalar_prefetch=1, grid=(T // td,),
        in_specs=[pl.BlockSpec((2 * TOP_K, td), lambda i, st: (0, i), memory_space=pltpu.SMEM),
                  pl.BlockSpec((td, D), lambda i, st: (i, 0)),
                  pl.BlockSpec(memory_space=pl.ANY)],
        out_specs=pl.BlockSpec(memory_space=pl.ANY),
        scratch_shapes=[pltpu.SemaphoreType.DMA(())])
    return pl.pallas_call(
        _dispatch_kernel, grid_spec=grid_spec,
        out_shape=jax.ShapeDtypeStruct((n_rows, D), F32),
        input_output_aliases={3: 0},
        compiler_params=_cparams(("arbitrary",)), name="dispatch",
    )(starts, meta, h2, xs0)


def _expert_kernel(be_ref, nv_ref, xs_ref, wgu_ref, bg_ref, bl_ref, wdn_ref, bd_ref, perm_ref, ys_ref,
                   wg_s, wl_s, wd_s):
    i = pl.program_id(0)
    valid = i < nv_ref[0]
    new_expert = valid & ((i == 0) | (be_ref[i] != be_ref[jnp.maximum(i - 1, 0)]))

    @pl.when(new_expert)
    def _():
        perm = perm_ref[...]
        half = PERM_GROUP // 2
        for c in range(2 * D_FF // PERM_GROUP):
            w = wgu_ref[:, c * PERM_GROUP:(c + 1) * PERM_GROUP].astype(BF16)
            sep = jnp.dot(w, perm, preferred_element_type=F32).astype(BF16)
            wg_s[:, c * half:(c + 1) * half] = sep[:, :half]
            wl_s[:, c * half:(c + 1) * half] = sep[:, half:]
        wd_s[...] = wdn_ref[...].astype(BF16)

    @pl.when(valid)
    def _():
        xb = xs_ref[...].astype(BF16)
        g = jnp.dot(xb, wg_s[...], preferred_element_type=F32) + bg_ref[...]
        l = jnp.dot(xb, wl_s[...], preferred_element_type=F32) + bl_ref[...]
        glu = jnp.minimum(g, SWIGLU_LIMIT)
        lin = jnp.clip(l, -SWIGLU_LIMIT, SWIGLU_LIMIT)
        act = glu * jax.nn.sigmoid(SWIGLU_ALPHA * glu) * (lin + 1.0)
        ys_ref[...] = jnp.dot(act.astype(BF16), wd_s[...], preferred_element_type=F32) + bd_ref[...]

    @pl.when(jnp.logical_not(valid))
    def _():
        ys_ref[...] = jnp.zeros_like(ys_ref)


def _expert_call(block_expert, n_valid, xs, prm):
    n_rows, D = xs.shape
    n_blocks = n_rows // MOE_BLOCK
    half = PERM_GROUP // 2
    src = np.concatenate([2 * np.arange(half), 2 * np.arange(half) + 1])
    perm = jnp.asarray(np.arange(PERM_GROUP)[:, None] == src[None, :], BF16)
    rows = lambda i, be, nv: (jnp.minimum(i, nv[0] - 1), 0)
    wsel = lambda i, be, nv: (be[i], 0, 0)
    grid_spec = pltpu.PrefetchScalarGridSpec(
        num_scalar_prefetch=2, grid=(n_blocks,),
        in_specs=[pl.BlockSpec((MOE_BLOCK, D), rows),
                  pl.BlockSpec((None, D, 2 * D_FF), wsel),
                  pl.BlockSpec((None, 1, D_FF), wsel), pl.BlockSpec((None, 1, D_FF), wsel),
                  pl.BlockSpec((None, D_FF, D), wsel), pl.BlockSpec((None, 1, D), wsel),
                  pl.BlockSpec((PERM_GROUP, PERM_GROUP), lambda i, be, nv: (0, 0))],
        out_specs=pl.BlockSpec((MOE_BLOCK, D), lambda i, be, nv: (i, 0)),
        scratch_shapes=[pltpu.VMEM((D, D_FF), BF16), pltpu.VMEM((D, D_FF), BF16),
                        pltpu.VMEM((D_FF, D), BF16)])
    return pl.pallas_call(
        _expert_kernel, grid_spec=grid_spec,
        out_shape=jax.ShapeDtypeStruct((n_rows, D), F32),
        compiler_params=_cparams(("arbitrary",)), name="experts",
    )(block_expert, n_valid, xs, prm["w_gu"], prm["b_glu"], prm["b_lin"], prm["w_dn"], prm["b_dn"], perm)


def _combine_kernel(starts_ref, meta_ref, xmid_ref, gates_ref, fnorm_ref, ys_ref, outa_ref, outb_ref,
                    buf_ref, sem, *, tiles_first):
    tc = xmid_ref.shape[0]

    def row_copy(r, kk):
        return pltpu.make_async_copy(ys_ref.at[pl.ds(_slot(starts_ref, meta_ref, kk, r), 1), :],
                                     buf_ref.at[kk, pl.ds(r, 1), :], sem)

    def start(r, c):
        for kk in range(TOP_K):
            row_copy(r, kk).start()
        return c

    def wait(r, c):
        for kk in range(TOP_K):
            row_copy(r, kk).wait()
        return c

    lax.fori_loop(0, tc, start, 0)
    g = jnp.transpose(gates_ref[...])
    lax.fori_loop(0, tc, wait, 0)
    y = xmid_ref[...]
    for kk in range(TOP_K):
        y = y + buf_ref[kk] * g[:, kk:kk + 1]
    y = _rms(y, fnorm_ref[...])

    @pl.when(pl.program_id(0) < tiles_first)
    def _():
        outa_ref[...] = y

    @pl.when(pl.program_id(0) >= tiles_first)
    def _():
        outb_ref[...] = y


def _combine_call(starts, meta, x_mid, gates, final_norm, ys, tc, rows_first):
    T, D = x_mid.shape
    tiles_first = rows_first // tc
    grid_spec = pltpu.PrefetchScalarGridSpec(
        num_scalar_prefetch=1, grid=(T // tc,),
        in_specs=[pl.BlockSpec((2 * TOP_K, tc), lambda i, st: (0, i), memory_space=pltpu.SMEM),
                  pl.BlockSpec((tc, D), lambda i, st: (i, 0)),
                  pl.BlockSpec((2 * TOP_K, tc), lambda i, st: (0, i)),
                  pl.BlockSpec(final_norm.shape, lambda i, st: (0, 0)),
                  pl.BlockSpec(memory_space=pl.ANY)],
        out_specs=[pl.BlockSpec((tc, D), lambda i, st: (jnp.minimum(i, tiles_first - 1), 0)),
                   pl.BlockSpec((tc, D), lambda i, st: (jnp.maximum(i - tiles_first, 0), 0))],
        scratch_shapes=[pltpu.VMEM((TOP_K, tc, D), F32), pltpu.SemaphoreType.DMA(())])
    return pl.pallas_call(
        functools.partial(_combine_kernel, tiles_first=tiles_first), grid_spec=grid_spec,
        out_shape=[jax.ShapeDtypeStruct((rows_first, D), F32), jax.ShapeDtypeStruct((T - rows_first, D), F32)],
        compiler_params=_cparams(("arbitrary",)), name="combine",
    )(starts, meta, x_mid, gates, final_norm, ys)


def _prep_params(attn_norm, w_in, mla_q_norm, mla_w_uq, mla_kv_norm, mla_w_ukv, gla_w_gate_fwd,
                 gla_b_gate_fwd, gla_w_gate_bwd, gla_b_gate_bwd, gla_out_norm, w_out, ffn_norm,
                 router_w, router_b, w_gu, b_gu, w_dn, b_dn):
    D = D_MODEL
    o = np.cumsum((0, MLA_Q_LORA, MLA_KV_LORA, MLA_ROPE, GLA_KDIM, GLA_KDIM, GLA_WIDTH,
                   2 * GLA_GATE_RANK, GLA_WIDTH))
    seg = [w_in[:, o[n]:o[n + 1]] for n in range(8)]
    z = lambda n: jnp.zeros((D, n), w_in.dtype)
    w_in_p = jnp.concatenate(
        [seg[0], seg[1], z(MLA_NOPE), seg[2], z(LANES - MLA_NOPE - MLA_ROPE), seg[3], seg[4], seg[5],
         seg[6], z(LANES - 2 * GLA_GATE_RANK), seg[7]], axis=1).astype(BF16)

    wq = mla_w_uq.reshape(MLA_Q_LORA, MLA_HEADS, MLA_NOPE + MLA_ROPE)
    w_uq_p = jnp.concatenate(
        [wq[:, :, :MLA_NOPE].reshape(MLA_Q_LORA, -1),
         wq[:, :, MLA_NOPE:MLA_NOPE + HALF_ROPE].reshape(MLA_Q_LORA, -1),
         wq[:, :, MLA_NOPE + HALF_ROPE:].reshape(MLA_Q_LORA, -1)], axis=1)
    wkv = mla_w_ukv.reshape(MLA_KV_LORA, MLA_HEADS, MLA_NOPE + MLA_V)
    w_k = jnp.concatenate([wkv[:, :, :MLA_NOPE], jnp.zeros((MLA_KV_LORA, MLA_HEADS, QK_PAD - MLA_NOPE),
                                                            wkv.dtype)], axis=2)
    w_v = wkv[:, :, MLA_NOPE:].reshape(MLA_KV_LORA, -1)

    def gate_w(w, row0):
        full = jnp.zeros((LANES, GLA_KDIM), w.dtype)
        return full.at[row0:row0 + GLA_GATE_RANK].set(w).astype(BF16)

    E = N_EXPERTS
    return {
        "attn_norm": attn_norm.reshape(1, D), "w_in": w_in_p,
        "q_norm": mla_q_norm.reshape(1, -1), "w_uqT": w_uq_p.T.astype(BF16),
        "kv_norm": mla_kv_norm.reshape(1, -1),
        "w_k": w_k.reshape(MLA_KV_LORA, -1).astype(BF16), "w_vT": w_v.T.astype(BF16),
        "w_gf": gate_w(gla_w_gate_fwd, 0), "b_gf": gla_b_gate_fwd.reshape(1, -1),
        "w_gb": gate_w(gla_w_gate_bwd, GLA_GATE_RANK), "b_gb": gla_b_gate_bwd.reshape(1, -1),
        "gla_out_norm": gla_out_norm.reshape(1, -1),
        "w_out_a": w_out[:MLA_WIDTH].astype(BF16), "w_out_b": w_out[MLA_WIDTH:].astype(BF16),
        "ffn_norm": ffn_norm.reshape(1, D),
        "router_wT": router_w.T, "router_b": router_b.reshape(E, 1),
        "w_gu": w_gu, "w_dn": w_dn,
        "b_glu": b_gu[:, 0::2].reshape(E, 1, D_FF), "b_lin": b_gu[:, 1::2].reshape(E, 1, D_FF),
        "b_dn": b_dn.reshape(E, 1, D),
    }


def _rope_tables(S):
    inv_freq = jnp.power(ROPE_THETA, -jnp.arange(0, MLA_ROPE, 2, dtype=F32) / MLA_ROPE)
    ang = jnp.arange(S, dtype=F32)[:, None] * inv_freq[None, :]
    cos, sin = jnp.cos(ang), jnp.sin(ang)
    z = lambda n: jnp.zeros((S, n), F32)
    tail = LANES - MLA_NOPE - MLA_ROPE
    return {
        "c": jnp.concatenate([z(MLA_NOPE), cos, cos, z(tail)], axis=1),
        "s1": jnp.concatenate([z(MLA_NOPE), -sin, z(HALF_ROPE), z(tail)], axis=1),
        "s2": jnp.concatenate([z(MLA_NOPE), z(HALF_ROPE), sin, z(tail)], axis=1),
        "cosT": jnp.tile(cos.T, (MLA_HEADS, 1)), "sinT": jnp.tile(sin.T, (MLA_HEADS, 1)),
    }


def _encoder(xa, xb, prm, final_norm):
    n_first, S, D = xa.shape
    B = n_first + xb.shape[0]
    rope = _rope_tables(S)
    qT, k, vT, gq, gk, gv, laf, lab, gr = _pre_call(xa, xb, prm, rope, ATTN_KEY_TILE)
    oT = _attn_call(qT, k, vT, min(ATTN_QUERY_TILE, S)).reshape(B, MLA_WIDTH, S)
    o_f, o_b = _gla_call(gq, gk, gv, laf, lab)
    x_mid, h2, meta, gates, counts = _post_call(xa, xb, oT, o_f, o_b, gr, prm, min(256, S))

    T = B * S
    n_rows = T * TOP_K + N_EXPERTS * MOE_BLOCK
    n_blocks = n_rows // MOE_BLOCK
    cnt = counts[:, 0].astype(jnp.int32)
    padded = ((cnt + MOE_BLOCK - 1) // MOE_BLOCK) * MOE_BLOCK
    pends = jnp.cumsum(padded)
    starts = (pends - padded).astype(jnp.int32)
    n_valid = (pends[-1] // MOE_BLOCK).astype(jnp.int32).reshape(1)
    blk = jnp.minimum(jnp.arange(n_blocks, dtype=jnp.int32), n_valid[0] - 1) * MOE_BLOCK
    block_expert = jnp.minimum(jnp.sum(pends[None, :] <= blk[:, None], axis=1), N_EXPERTS - 1).astype(jnp.int32)

    xs = _dispatch_call(starts, meta, h2.reshape(T, D), n_rows, min(128, S))
    ys = _expert_call(block_expert, n_valid, xs, prm)
    ya, yb = _combine_call(starts, meta, x_mid.reshape(T, D), gates, final_norm.reshape(1, D), ys,
                           min(128, S), n_first * S)
    return ya.reshape(xa.shape), yb.reshape(xb.shape)


def kernel(x_prompt, x_sample, attn_norm, w_in, mla_q_norm, mla_w_uq, mla_kv_norm, mla_w_ukv,
           gla_w_gate_fwd, gla_b_gate_fwd, gla_w_gate_bwd, gla_b_gate_bwd, gla_out_norm, w_out, ffn_norm,
           router_w, router_b, expert_w_gate_up, expert_b_gate_up, expert_w_down, expert_b_down,
           final_norm):
    layer = (attn_norm, w_in, mla_q_norm, mla_w_uq, mla_kv_norm, mla_w_ukv, gla_w_gate_fwd,
             gla_b_gate_fwd, gla_w_gate_bwd, gla_b_gate_bwd, gla_out_norm, w_out, ffn_norm, router_w,
             router_b, expert_w_gate_up, expert_b_gate_up, expert_w_down, expert_b_down)
    assert all(p.shape[0] == 1 for p in layer), "single layer expected"
    assert x_prompt.shape[1:] == x_sample.shape[1:]
    prm = _prep_params(*[p[0] for p in layer])
    return _encoder(x_prompt, x_sample, prm, final_norm)
```

```python
import functools

import jax
import jax.numpy as jnp
import numpy as np
from jax import lax
from jax.experimental import pallas as pl
from jax.experimental.pallas import tpu as pltpu
from jax.experimental.pallas import tpu_sc as plsc

F32 = jnp.float32
BF16 = jnp.bfloat16

D_MODEL = 1024
MLA_HEADS = 8
MLA_NOPE = 64
MLA_ROPE = 32
MLA_V = 64
MLA_Q_LORA = 384
MLA_KV_LORA = 256
ROPE_THETA = 10000.0
GLA_HEADS = 4
GLA_DK = 64
GLA_DV = 128
GLA_GATE_RANK = 16
GLA_GATE_NORM = 16.0
N_EXPERTS = 32
TOP_K = 4
D_FF = 1024
SWIGLU_LIMIT = 7.0
SWIGLU_ALPHA = 1.702
MOE_BLOCK = 512
RMS_EPS = 1e-6

MLA_WIDTH = MLA_HEADS * MLA_V
GLA_WIDTH = GLA_HEADS * GLA_DV
GLA_KDIM = GLA_HEADS * GLA_DK
HALF_ROPE = MLA_ROPE // 2
QK_PAD = 128
LANES = 128
V_ONES_ROWS = 16

OFF_CQ = 0
OFF_CKV = OFF_CQ + MLA_Q_LORA
OFF_KR = OFF_CKV + MLA_KV_LORA
OFF_GQ = OFF_KR + LANES
OFF_GK = OFF_GQ + GLA_KDIM
OFF_GV = OFF_GK + GLA_KDIM
OFF_LR = OFF_GV + GLA_WIDTH
OFF_GR = OFF_LR + LANES
PROJ_COLS = OFF_GR + GLA_WIDTH

ATTN_KEY_TILE = 256
ATTN_QUERY_TILE = 512
ATTN_UNROLL = 32
ATTN_SCORE_SLOTS = 4
GLA_TILE = 256
GLA_LEVELS = 8
SC_WINDOW = 32
SC_INDEX_PAD = 128
PERM_GROUP = 256
VMEM_LIMIT = 56 * 1024 * 1024

LOG2_E = 1.4426950408889634
NT_DIMS = (((1,), (1,)), ((), ()))
TN_DIMS = (((0,), (0,)), ((), ()))


def _cparams(sem):
    return pltpu.CompilerParams(dimension_semantics=sem, vmem_limit_bytes=VMEM_LIMIT)


def _rms(x, gain):
    return x * lax.rsqrt(jnp.mean(x * x, axis=-1, keepdims=True) + RMS_EPS) * gain


def _split_bf16(x):
    hi = x.astype(BF16)
    lo = (x - hi.astype(F32)).astype(BF16)
    return hi, lo


def _pre_kernel(xa_ref, xb_ref, an_ref, win_ref, qn_ref, wuqT_ref, kvn_ref, wk_ref, wvT_ref,
                wgf_ref, bgf_ref, wgb_ref, bgb_ref, rc_ref, rs1_ref, rs2_ref, cosT_ref, sinT_ref,
                qT_ref, k_ref, vT_ref, gq_ref, gk_ref, gv_ref, laf_ref, lab_ref, gr_ref, *, n_first):
    x = jnp.where(pl.program_id(0) < n_first, xa_ref[...], xb_ref[...])
    h = _rms(x, an_ref[...]).astype(BF16)
    proj = jnp.dot(h, win_ref[...], preferred_element_type=F32)

    cqn = _rms(proj[:, OFF_CQ:OFF_CQ + MLA_Q_LORA], qn_ref[...]).astype(BF16)
    ckvn = _rms(proj[:, OFF_CKV:OFF_CKV + MLA_KV_LORA], kvn_ref[...]).astype(BF16)

    scale = (MLA_NOPE + MLA_ROPE) ** -0.5 * LOG2_E
    qT = lax.dot_general(wuqT_ref[...], cqn, NT_DIMS, preferred_element_type=F32) * scale
    n_nope = MLA_HEADS * MLA_NOPE
    n_half = MLA_HEADS * HALF_ROPE
    x1 = qT[n_nope:n_nope + n_half]
    x2 = qT[n_nope + n_half:n_nope + 2 * n_half]
    c = cosT_ref[...]
    s = sinT_ref[...]
    x1r = x1 * c - x2 * s
    x2r = x1 * s + x2 * c
    zpad = jnp.zeros((QK_PAD - MLA_NOPE - MLA_ROPE, qT.shape[1]), BF16)
    for hd in range(MLA_HEADS):
        qT_ref[hd, 0:MLA_NOPE, :] = qT[hd * MLA_NOPE:(hd + 1) * MLA_NOPE].astype(BF16)
        qT_ref[hd, MLA_NOPE:MLA_NOPE + HALF_ROPE, :] = x1r[hd * HALF_ROPE:(hd + 1) * HALF_ROPE].astype(BF16)
        qT_ref[hd, MLA_NOPE + HALF_ROPE:MLA_NOPE + MLA_ROPE, :] = (
            x2r[hd * HALF_ROPE:(hd + 1) * HALF_ROPE].astype(BF16))
        qT_ref[hd, MLA_NOPE + MLA_ROPE:QK_PAD, :] = zpad

    kr = proj[:, OFF_KR:OFF_KR + LANES]
    kr = (kr * rc_ref[...]
          + pltpu.roll(kr, LANES - HALF_ROPE, axis=1) * rs1_ref[...]
          + pltpu.roll(kr, HALF_ROPE, axis=1) * rs2_ref[...])
    kfull = jnp.dot(ckvn, wk_ref[...], preferred_element_type=F32)
    for hd in range(MLA_HEADS):
        k_ref[hd] = (kfull[:, hd * QK_PAD:(hd + 1) * QK_PAD] + kr).astype(BF16)

    vT = lax.dot_general(wvT_ref[...], ckvn, NT_DIMS, preferred_element_type=F32)
    for hd in range(MLA_HEADS):
        vT_ref[hd] = vT[hd * MLA_V:(hd + 1) * MLA_V].astype(BF16)

    gq_ref[...] = proj[:, OFF_GQ:OFF_GQ + GLA_KDIM] * (GLA_DK ** -0.5)
    gk_ref[...] = proj[:, OFF_GK:OFF_GK + GLA_KDIM]
    gv_ref[...] = proj[:, OFF_GV:OFF_GV + GLA_WIDTH]
    gr_ref[...] = proj[:, OFF_GR:OFF_GR + GLA_WIDTH]
    lr = proj[:, OFF_LR:OFF_LR + LANES].astype(BF16)

    def log_decay(w_ref, b_ref):
        z = jnp.dot(lr, w_ref[...], preferred_element_type=F32) + b_ref[...]
        return (jnp.minimum(z, 0.0) - jnp.log1p(jnp.exp(-jnp.abs(z)))) * (1.0 / GLA_GATE_NORM)

    laf_ref[...] = log_decay(wgf_ref, bgf_ref)
    lab_ref[...] = log_decay(wgb_ref, bgb_ref)


def _two_source_specs(n_first, tm, width):
    return [pl.BlockSpec((None, tm, width), lambda b, i: (jnp.minimum(b, n_first - 1), i, 0)),
            pl.BlockSpec((None, tm, width), lambda b, i: (jnp.maximum(b - n_first, 0), i, 0))]


def _pre_call(xa, xb, prm, rope, tm):
    n_first, S, D = xa.shape
    B = n_first + xb.shape[0]
    nS = S // tm
    H = MLA_HEADS

    def full(a):
        nd = a.ndim
        return pl.BlockSpec(a.shape, lambda b, i, _nd=nd: (0,) * _nd)

    tok = lambda w: pl.BlockSpec((None, tm, w), lambda b, i: (b, i, 0))
    in_specs = [
        *_two_source_specs(n_first, tm, D), full(prm["attn_norm"]), full(prm["w_in"]), full(prm["q_norm"]), full(prm["w_uqT"]),
        full(prm["kv_norm"]), full(prm["w_k"]), full(prm["w_vT"]),
        full(prm["w_gf"]), full(prm["b_gf"]), full(prm["w_gb"]), full(prm["b_gb"]),
        pl.BlockSpec((tm, LANES), lambda b, i: (i, 0)),
        pl.BlockSpec((tm, LANES), lambda b, i: (i, 0)),
        pl.BlockSpec((tm, LANES), lambda b, i: (i, 0)),
        pl.BlockSpec((H * HALF_ROPE, tm), lambda b, i: (0, i)),
        pl.BlockSpec((H * HALF_ROPE, tm), lambda b, i: (0, i)),
    ]
    out_shape = [
        jax.ShapeDtypeStruct((B, H, QK_PAD, S), BF16),
        jax.ShapeDtypeStruct((B, H, S, QK_PAD), BF16),
        jax.ShapeDtypeStruct((B, H, nS, MLA_V, tm), BF16),
        jax.ShapeDtypeStruct((B, S, GLA_KDIM), F32),
        jax.ShapeDtypeStruct((B, S, GLA_KDIM), F32),
        jax.ShapeDtypeStruct((B, S, GLA_WIDTH), F32),
        jax.ShapeDtypeStruct((B, S, GLA_KDIM), F32),
        jax.ShapeDtypeStruct((B, S, GLA_KDIM), F32),
        jax.ShapeDtypeStruct((B, S, GLA_WIDTH), F32),
    ]
    out_specs = [
        pl.BlockSpec((None, H, QK_PAD, tm), lambda b, i: (b, 0, 0, i)),
        pl.BlockSpec((None, H, tm, QK_PAD), lambda b, i: (b, 0, i, 0)),
        pl.BlockSpec((None, H, None, MLA_V, tm), lambda b, i: (b, 0, i, 0, 0)),
        tok(GLA_KDIM), tok(GLA_KDIM), tok(GLA_WIDTH), tok(GLA_KDIM), tok(GLA_KDIM), tok(GLA_WIDTH),
    ]
    return pl.pallas_call(
        functools.partial(_pre_kernel, n_first=n_first), grid=(B, nS), in_specs=in_specs, out_specs=out_specs, out_shape=out_shape,
        compiler_params=_cparams(("parallel", "parallel")), name="pre",
    )(xa, xb, prm["attn_norm"], prm["w_in"], prm["q_norm"], prm["w_uqT"], prm["kv_norm"], prm["w_k"],
      prm["w_vT"], prm["w_gf"], prm["b_gf"], prm["w_gb"], prm["b_gb"],
      rope["c"], rope["s1"], rope["s2"], rope["cosT"], rope["sinT"])


def _attn_kernel(qT_ref, k_ref, vT_ref, o_ref, s_ref, p_ref, *, n_kblk, unroll):
    qT = qT_ref[...]
    tq = qT.shape[1]
    tk = k_ref.shape[0] // n_kblk
    ones = jnp.ones((V_ONES_ROWS, tk), BF16)

    def scores(j, slot):
        kb = k_ref[pl.ds(pl.multiple_of(j * tk, tk), tk), :]
        sT = jnp.dot(kb, qT, preferred_element_type=F32)
        s_ref[slot] = sT
        return jnp.max(sT, axis=0, keepdims=True)

    def values(j, slot):
        vext = jnp.concatenate([vT_ref[j], ones], axis=0)
        return jnp.dot(vext, p_ref[slot], preferred_element_type=F32)

    def softmax(s_slot, p_slot, blk_max, m):
        m_new = jnp.maximum(m, blk_max)
        p_ref[p_slot] = jnp.exp2(s_ref[s_slot] - m_new).astype(BF16)
        return m_new, jnp.exp2(m - m_new)

    def body(jj, carry):
        max_0, max_1, m, alpha_prev, acc = carry
        for u in range(unroll):
            j = unroll * jj + u
            pv = values(jnp.maximum(j - 1, 0), (u + 1) % 2)
            max_2 = scores(jnp.minimum(j + 2, n_kblk - 1), (u + 2) % ATTN_SCORE_SLOTS)
            m, alpha = softmax(u % ATTN_SCORE_SLOTS, u % 2, max_0, m)
            acc = alpha_prev * acc + pv
            max_0, max_1, alpha_prev = max_1, max_2, alpha
        return max_0, max_1, m, alpha_prev, acc

    p_ref[1] = jnp.zeros(p_ref.shape[1:], BF16)
    m0 = jnp.full((1, tq), -jnp.inf, F32)
    acc0 = jnp.zeros((MLA_V + V_ONES_ROWS, tq), F32)
    carry0 = (scores(0, 0), scores(1, 1), m0, jnp.ones((1, tq), F32), acc0)
    _, _, _, alpha, acc = lax.fori_loop(0, n_kblk // unroll, body, carry0)
    acc = alpha * acc + values(n_kblk - 1, 1)
    o_ref[...] = (acc[0:MLA_V] / acc[MLA_V:MLA_V + 1]).astype(o_ref.dtype)


def _attn_call(qT, k, vT, tq):
    B, H, _, S = qT.shape
    n_kblk, tk = vT.shape[2], vT.shape[4]
    unroll = min(ATTN_UNROLL, n_kblk)
    assert unroll % ATTN_SCORE_SLOTS == 0 and n_kblk % unroll == 0 and S % tq == 0
    return pl.pallas_call(
        functools.partial(_attn_kernel, n_kblk=n_kblk, unroll=unroll),
        grid=(B, H, S // tq),
        in_specs=[
            pl.BlockSpec((None, None, QK_PAD, tq), lambda b, h, i: (b, h, 0, i)),
            pl.BlockSpec((None, None, S, QK_PAD), lambda b, h, i: (b, h, 0, 0)),
            pl.BlockSpec((None, None, n_kblk, MLA_V, tk), lambda b, h, i: (b, h, 0, 0, 0)),
        ],
        out_specs=pl.BlockSpec((None, None, MLA_V, tq), lambda b, h, i: (b, h, 0, i)),
        out_shape=jax.ShapeDtypeStruct((B, H, MLA_V, S), BF16),
        scratch_shapes=[pltpu.VMEM((ATTN_SCORE_SLOTS, tk, tq), F32), pltpu.VMEM((2, tk, tq), BF16)],
        compiler_params=_cparams(("parallel", "parallel", "parallel")), name="attn",
    )(qT, k, vT)


def _gla_consts():
    L = GLA_TILE
    i = np.arange(L)[:, None]
    j = np.arange(L)[None, :]
    x = i ^ j
    lidx = np.where(x > 0, np.floor(np.log2(np.maximum(x, 1))), -1).astype(np.int32)
    lidx_f = np.where(i > j, lidx, -1).astype(np.int32)
    lidx_b = np.where(i < j, lidx, -1).astype(np.int32)
    hd = np.arange(GLA_KDIM)[:, None] // GLA_DK
    hv = np.arange(GLA_WIDTH)[None, :] // GLA_DV
    bexp = (hd == hv).astype(np.float32)
    return jnp.asarray(lidx_f), jnp.asarray(lidx_b), jnp.asarray(bexp, BF16)


def _gla_direction(q, k, v, la, lidx, bexp_ref, ssum_ref, state_ref, o_ref, forward):
    L = GLA_TILE
    la_hi, la_lo = _split_bf16(la)
    vb = v.astype(BF16)
    lane_head = lax.broadcasted_iota(jnp.int32, (1, GLA_KDIM), 1) // GLA_DK
    head_masks = [(lane_head == h).astype(BF16) for h in range(GLA_HEADS)]
    row = lax.broadcasted_iota(jnp.int32, la.shape, 0)

    def widen(c, t, m):
        upper_half = (row & m) != 0
        sibling_total = jnp.where(upper_half, pltpu.roll(t, m, axis=0), pltpu.roll(t, L - m, axis=0))
        return c + jnp.where(upper_half, sibling_total, 0.0), t + sibling_total

    ssum_ref[...] = jnp.zeros_like(ssum_ref)
    c, t = la, la
    for lvl in range(GLA_LEVELS):
        if lvl > 0:
            c, t = widen(c, t, 1 << (lvl - 1))
        if forward:
            eq, ek = c, t - c
        else:
            eq, ek = t - c + la, c - la
        ql = (q * jnp.exp(eq)).astype(BF16)
        kl = (k * jnp.exp(ek)).astype(BF16)
        sel = lidx == lvl
        for h in range(GLA_HEADS):
            sc = lax.dot_general(ql * head_masks[h], kl, NT_DIMS, preferred_element_type=F32)
            ssum_ref[h] = jnp.where(sel, sc, ssum_ref[h])

    c, t = widen(c, t, L // 2)
    if forward:
        eq, ek = c, t - c
    else:
        eq, ek = t - c + la, c - la
    q_in = (q * jnp.exp(eq)).astype(BF16)
    k_out = (k * jnp.exp(ek)).astype(BF16)
    o = jnp.dot(q_in, state_ref[...].astype(BF16), preferred_element_type=F32)
    if forward:
        o = o + jnp.dot((q * k).astype(BF16), bexp_ref[...], preferred_element_type=F32) * v
    for h in range(GLA_HEADS):
        oh = jnp.dot(ssum_ref[h].astype(BF16), vb[:, h * GLA_DV:(h + 1) * GLA_DV],
                     preferred_element_type=F32)
        o_ref[:, h * GLA_DV:(h + 1) * GLA_DV] = o[:, h * GLA_DV:(h + 1) * GLA_DV] + oh

    ones = jnp.ones((L, LANES), BF16)
    tot_col = (lax.dot_general(la_hi, ones, TN_DIMS, preferred_element_type=F32)
               + lax.dot_general(la_lo, ones, TN_DIMS, preferred_element_type=F32))
    dec = jnp.exp(tot_col)
    upd = lax.dot_general(k_out, vb, TN_DIMS, preferred_element_type=F32)
    for h in range(GLA_HEADS):
        cols = slice(h * GLA_DV, (h + 1) * GLA_DV)
        state_ref[:, cols] = dec * state_ref[:, cols] + upd[:, cols] * bexp_ref[:, cols].astype(F32)


def _gla_kernel(qf_ref, kf_ref, vf_ref, laf_ref, qb_ref, kb_ref, vb_ref, lab_ref,
                lidxf_ref, lidxb_ref, bexp_ref, of_ref, ob_ref, sf_ref, sb_ref, ssum_ref):
    @pl.when(pl.program_id(1) == 0)
    def _():
        sf_ref[...] = jnp.zeros_like(sf_ref)
        sb_ref[...] = jnp.zeros_like(sb_ref)

    _gla_direction(qf_ref[...], kf_ref[...], vf_ref[...], laf_ref[...], lidxf_ref[...],
                   bexp_ref, ssum_ref, sf_ref, of_ref, True)
    _gla_direction(qb_ref[...], kb_ref[...], vb_ref[...], lab_ref[...], lidxb_ref[...],
                   bexp_ref, ssum_ref, sb_ref, ob_ref, False)


def _gla_call(gq, gk, gv, laf, lab):
    B, S, _ = gq.shape
    L = GLA_TILE
    n = S // L
    lidx_f, lidx_b, bexp = _gla_consts()
    fwd = lambda w: pl.BlockSpec((None, L, w), lambda b, i: (b, i, 0))
    bwd = lambda w: pl.BlockSpec((None, L, w), lambda b, i: (b, n - 1 - i, 0))
    const = lambda a: pl.BlockSpec(a.shape, lambda b, i, _nd=a.ndim: (0,) * _nd)
    return pl.pallas_call(
        _gla_kernel, grid=(B, n),
        in_specs=[fwd(GLA_KDIM), fwd(GLA_KDIM), fwd(GLA_WIDTH), fwd(GLA_KDIM),
                  bwd(GLA_KDIM), bwd(GLA_KDIM), bwd(GLA_WIDTH), bwd(GLA_KDIM),
                  const(lidx_f), const(lidx_b), const(bexp)],
        out_specs=[fwd(GLA_WIDTH), bwd(GLA_WIDTH)],
        out_shape=[jax.ShapeDtypeStruct((B, S, GLA_WIDTH), F32)] * 2,
        scratch_shapes=[pltpu.VMEM((GLA_KDIM, GLA_WIDTH), F32), pltpu.VMEM((GLA_KDIM, GLA_WIDTH), F32),
                        pltpu.VMEM((GLA_HEADS, L, L), F32)],
        compiler_params=_cparams(("parallel", "arbitrary")), name="gla",
    )(gq, gk, gv, laf, gq, gk, gv, lab, lidx_f, lidx_b, bexp)


def _post_kernel(xa_ref, xb_ref, oT_ref, of_ref, ob_ref, gr_ref, gon_ref, woa_ref, wob_ref, fn_ref,
                 rwT_ref, rb_ref, upper_ref,
                 xmid_ref, h2_ref, meta_ref, gates_ref, counts_ref, carry_ref, *, n_first):
    first = (pl.program_id(0) == 0) & (pl.program_id(1) == 0)

    @pl.when(first)
    def _():
        carry_ref[...] = jnp.zeros_like(carry_ref)

    o = of_ref[...] + ob_ref[...]
    gr = gr_ref[...]
    parts = []
    for h in range(GLA_HEADS):
        cols = slice(h * GLA_DV, (h + 1) * GLA_DV)
        parts.append(_rms(o[:, cols], gon_ref[...]) * jax.nn.silu(gr[:, cols]))
    gla = jnp.concatenate(parts, axis=1).astype(BF16)

    x_mid = (jnp.where(pl.program_id(0) < n_first, xa_ref[...], xb_ref[...])
             + lax.dot_general(oT_ref[...], woa_ref[...], TN_DIMS, preferred_element_type=F32)
             + jnp.dot(gla, wob_ref[...], preferred_element_type=F32))
    xmid_ref[...] = x_mid
    h2 = _rms(x_mid, fn_ref[...])
    h2_ref[...] = h2

    h_hi, h_lo = _split_bf16(h2)
    w_hi, w_lo = _split_bf16(rwT_ref[...])
    logits = (lax.dot_general(w_hi, h_hi, NT_DIMS, preferred_element_type=F32)
              + lax.dot_general(w_hi, h_lo, NT_DIMS, preferred_element_type=F32)
              + lax.dot_general(w_lo, h_hi, NT_DIMS, preferred_element_type=F32)
              + rb_ref[...])
    tm = logits.shape[1]
    eidx = lax.broadcasted_iota(jnp.int32, (N_EXPERTS, tm), 0).astype(F32)
    vals, idxs, sels = [], [], []
    cur = logits
    for _ in range(TOP_K):
        mk = jnp.max(cur, axis=0, keepdims=True)
        ik = jnp.min(jnp.where(cur == mk, eidx, float(N_EXPERTS)), axis=0, keepdims=True)
        sel = eidx == ik
        vals.append(mk)
        idxs.append(ik)
        sels.append(sel)
        cur = jnp.where(sel, -jnp.inf, cur)
    exps = [jnp.exp(vk - vals[0]) for vk in vals]
    denom = exps[0] + exps[1] + exps[2] + exps[3]
    gates = [e / denom for e in exps]

    cnt = (sels[0] | sels[1] | sels[2] | sels[3])
    before = jnp.dot(cnt.astype(BF16), upper_ref[...], preferred_element_type=F32) + carry_ref[:, 0:1]
    ranks = [jnp.sum(jnp.where(sel, before, 0.0), axis=0, keepdims=True) for sel in sels]
    carry_ref[...] = carry_ref[...] + jnp.sum(cnt.astype(F32), axis=1, keepdims=True)
    counts_ref[...] = carry_ref[...]

    meta_ref[...] = jnp.concatenate(idxs + ranks, axis=0).astype(jnp.int32)
    gates_ref[...] = jnp.concatenate(gates + [jnp.zeros((TOP_K, tm), F32)], axis=0)


def _post_call(xa, xb, oT, o_f, o_b, gr, prm, tm):
    n_first, S, D = xa.shape
    B = n_first + xb.shape[0]
    nS = S // tm
    upper = jnp.asarray(np.triu(np.ones((tm, tm), np.float32), k=1), BF16)
    full = lambda a: pl.BlockSpec(a.shape, lambda b, i, _nd=a.ndim: (0,) * _nd)
    tok = lambda w: pl.BlockSpec((None, tm, w), lambda b, i: (b, i, 0))
    colblk = lambda r: pl.BlockSpec((None, r, tm), lambda b, i: (b, 0, i))
    flat = lambda r: pl.BlockSpec((r, tm), lambda b, i: (0, b * nS + i))
    return pl.pallas_call(
        functools.partial(_post_kernel, n_first=n_first), grid=(B, nS),
        in_specs=[*_two_source_specs(n_first, tm, D), colblk(MLA_WIDTH), tok(GLA_WIDTH), tok(GLA_WIDTH), tok(GLA_WIDTH),
                  full(prm["gla_out_norm"]), full(prm["w_out_a"]), full(prm["w_out_b"]),
                  full(prm["ffn_norm"]), full(prm["router_wT"]), full(prm["router_b"]), full(upper)],
        out_specs=[tok(D), tok(D), flat(2 * TOP_K), flat(2 * TOP_K),
                   pl.BlockSpec((N_EXPERTS, LANES), lambda b, i: (0, 0))],
        out_shape=[jax.ShapeDtypeStruct((B, S, D), F32), jax.ShapeDtypeStruct((B, S, D), F32),
                   jax.ShapeDtypeStruct((2 * TOP_K, B * S), jnp.int32),
                   jax.ShapeDtypeStruct((2 * TOP_K, B * S), F32),
                   jax.ShapeDtypeStruct((N_EXPERTS, LANES), F32)],
        scratch_shapes=[pltpu.VMEM((N_EXPERTS, LANES), F32)],
        compiler_params=_cparams(("arbitrary", "arbitrary")), name="post",
    )(xa, xb, oT, o_f, o_b, gr, prm["gla_out_norm"], prm["w_out_a"], prm["w_out_b"], prm["ffn_norm"],
      prm["router_wT"], prm["router_b"], upper)


def _sc_mesh():
    return plsc.VectorSubcoreMesh(core_axis_name="core", subcore_axis_name="subcore")


def _sc_scatter_rows(x, slots, n_rows):
    T, D = x.shape
    K = slots.shape[0]

    @pl.kernel(out_type=jax.ShapeDtypeStruct((n_rows, D), x.dtype), mesh=_sc_mesh(), scratch_types=[])
    def scatter(x_hbm, slots_hbm, out_hbm):
        def window(x_vmem, slots_vmem):
            for kk in range(K):
                pltpu.sync_copy(x_vmem, out_hbm.at[slots_vmem.at[kk, 0, pl.ds(0, SC_WINDOW)]])

        pltpu.emit_pipeline(
            window, grid=(T // SC_WINDOW,),
            in_specs=[pl.BlockSpec((SC_WINDOW, D), lambda i: (i, 0)),
                      pl.BlockSpec((K, 1, SC_INDEX_PAD), lambda i: (0, i, 0))],
            out_specs=[], core_axis_name=("core", "subcore"),
            dimension_semantics=(pltpu.PARALLEL,))(x_hbm, slots_hbm)

    return scatter(x, slots)


def _sc_gather_rows(y, slots, T):
    D = y.shape[1]
    K = slots.shape[0]

    @pl.kernel(out_type=jax.ShapeDtypeStruct((K, T, D), y.dtype), mesh=_sc_mesh(), scratch_types=[])
    def gather(y_hbm, slots_hbm, out_hbm):
        def window(slots_vmem, out_vmem):
            pltpu.sync_copy(y_hbm.at[slots_vmem.at[0, 0, pl.ds(0, SC_WINDOW)]], out_vmem.at[0])

        pltpu.emit_pipeline(
            window, grid=(K, T // SC_WINDOW),
            in_specs=[pl.BlockSpec((1, 1, SC_INDEX_PAD), lambda kk, i: (kk, i, 0))],
            out_specs=[pl.BlockSpec((1, SC_WINDOW, D), lambda kk, i: (kk, i, 0))],
            core_axis_name=("core", "subcore"),
            dimension_semantics=(pltpu.PARALLEL, pltpu.PARALLEL))(slots_hbm, out_hbm)

    return gather(y, slots)


def _expert_kernel(be_ref, nv_ref, xs_ref, wgu_ref, bg_ref, bl_ref, wdn_ref, bd_ref, perm_ref, ys_ref,
                   wg_s, wl_s, wd_s):
    i = pl.program_id(0)
    valid = i < nv_ref[0]
    new_expert = valid & ((i == 0) | (be_ref[i] != be_ref[jnp.maximum(i - 1, 0)]))

    @pl.when(new_expert)
    def _():
        perm = perm_ref[...]
        half = PERM_GROUP // 2
        for c in range(2 * D_FF // PERM_GROUP):
            w = wgu_ref[:, c * PERM_GROUP:(c + 1) * PERM_GROUP].astype(BF16)
            sep = jnp.dot(w, perm, preferred_element_type=F32).astype(BF16)
            wg_s[:, c * half:(c + 1) * half] = sep[:, :half]
            wl_s[:, c * half:(c + 1) * half] = sep[:, half:]
        wd_s[...] = wdn_ref[...].astype(BF16)

    @pl.when(valid)
    def _():
        xb = xs_ref[...].astype(BF16)
        g = jnp.dot(xb, wg_s[...], preferred_element_type=F32) + bg_ref[...]
        l = jnp.dot(xb, wl_s[...], preferred_element_type=F32) + bl_ref[...]
        glu = jnp.minimum(g, SWIGLU_LIMIT)
        lin = jnp.clip(l, -SWIGLU_LIMIT, SWIGLU_LIMIT)
        act = glu * jax.nn.sigmoid(SWIGLU_ALPHA * glu) * (lin + 1.0)
        ys_ref[...] = jnp.dot(act.astype(BF16), wd_s[...], preferred_element_type=F32) + bd_ref[...]

    @pl.when(jnp.logical_not(valid))
    def _():
        ys_ref[...] = jnp.zeros_like(ys_ref)


def _expert_call(block_expert, n_valid, xs, prm):
    n_rows, D = xs.shape
    n_blocks = n_rows // MOE_BLOCK
    half = PERM_GROUP // 2
    src = np.concatenate([2 * np.arange(half), 2 * np.arange(half) + 1])
    perm = jnp.asarray(np.arange(PERM_GROUP)[:, None] == src[None, :], BF16)
    rows = lambda i, be, nv: (jnp.minimum(i, nv[0] - 1), 0)
    wsel = lambda i, be, nv: (be[i], 0, 0)
    grid_spec = pltpu.PrefetchScalarGridSpec(
        num_scalar_prefetch=2, grid=(n_blocks,),
        in_specs=[pl.BlockSpec((MOE_BLOCK, D), rows),
                  pl.BlockSpec((None, D, 2 * D_FF), wsel),
                  pl.BlockSpec((None, 1, D_FF), wsel), pl.BlockSpec((None, 1, D_FF), wsel),
                  pl.BlockSpec((None, D_FF, D), wsel), pl.BlockSpec((None, 1, D), wsel),
                  pl.BlockSpec((PERM_GROUP, PERM_GROUP), lambda i, be, nv: (0, 0))],
        out_specs=pl.BlockSpec((MOE_BLOCK, D), lambda i, be, nv: (i, 0)),
        scratch_shapes=[pltpu.VMEM((D, D_FF), BF16), pltpu.VMEM((D, D_FF), BF16),
                        pltpu.VMEM((D_FF, D), BF16)])
    return pl.pallas_call(
        _expert_kernel, grid_spec=grid_spec,
        out_shape=jax.ShapeDtypeStruct((n_rows, D), F32),
        compiler_params=_cparams(("arbitrary",)), name="experts",
    )(block_expert, n_valid, xs, prm["w_gu"], prm["b_glu"], prm["b_lin"], prm["w_dn"], prm["b_dn"], perm)


def _combine_kernel(xmid_ref, gates_ref, fnorm_ref, y4_ref, outa_ref, outb_ref, *, tiles_first):
    g = jnp.transpose(gates_ref[...])
    y = xmid_ref[...]
    for kk in range(TOP_K):
        y = y + y4_ref[kk] * g[:, kk:kk + 1]
    y = _rms(y, fnorm_ref[...])

    @pl.when(pl.program_id(0) < tiles_first)
    def _():
        outa_ref[...] = y

    @pl.when(pl.program_id(0) >= tiles_first)
    def _():
        outb_ref[...] = y


def _combine_call(x_mid, gates, final_norm, y4, tc, rows_first):
    T, D = x_mid.shape
    tiles_first = rows_first // tc
    return pl.pallas_call(
        functools.partial(_combine_kernel, tiles_first=tiles_first), grid=(T // tc,),
        in_specs=[pl.BlockSpec((tc, D), lambda i: (i, 0)),
                  pl.BlockSpec((2 * TOP_K, tc), lambda i: (0, i)),
                  pl.BlockSpec(final_norm.shape, lambda i: (0, 0)),
                  pl.BlockSpec((TOP_K, tc, D), lambda i: (0, i, 0))],
        out_specs=[pl.BlockSpec((tc, D), lambda i: (jnp.minimum(i, tiles_first - 1), 0)),
                   pl.BlockSpec((tc, D), lambda i: (jnp.maximum(i - tiles_first, 0), 0))],
        out_shape=[jax.ShapeDtypeStruct((rows_first, D), F32), jax.ShapeDtypeStruct((T - rows_first, D), F32)],
        compiler_params=_cparams(("arbitrary",)), name="combine",
    )(x_mid, gates, final_norm, y4)


def _prep_params(attn_norm, w_in, mla_q_norm, mla_w_uq, mla_kv_norm, mla_w_ukv, gla_w_gate_fwd,
                 gla_b_gate_fwd, gla_w_gate_bwd, gla_b_gate_bwd, gla_out_norm, w_out, ffn_norm,
                 router_w, router_b, w_gu, b_gu, w_dn, b_dn):
    D = D_MODEL
    o = np.cumsum((0, MLA_Q_LORA, MLA_KV_LORA, MLA_ROPE, GLA_KDIM, GLA_KDIM, GLA_WIDTH,
                   2 * GLA_GATE_RANK, GLA_WIDTH))
    seg = [w_in[:, o[n]:o[n + 1]] for n in range(8)]
    z = lambda n: jnp.zeros((D, n), w_in.dtype)
    w_in_p = jnp.concatenate(
        [seg[0], seg[1], z(MLA_NOPE), seg[2], z(LANES - MLA_NOPE - MLA_ROPE), seg[3], seg[4], seg[5],
         seg[6], z(LANES - 2 * GLA_GATE_RANK), seg[7]], axis=1).astype(BF16)

    wq = mla_w_uq.reshape(MLA_Q_LORA, MLA_HEADS, MLA_NOPE + MLA_ROPE)
    w_uq_p = jnp.concatenate(
        [wq[:, :, :MLA_NOPE].reshape(MLA_Q_LORA, -1),
         wq[:, :, MLA_NOPE:MLA_NOPE + HALF_ROPE].reshape(MLA_Q_LORA, -1),
         wq[:, :, MLA_NOPE + HALF_ROPE:].reshape(MLA_Q_LORA, -1)], axis=1)
    wkv = mla_w_ukv.reshape(MLA_KV_LORA, MLA_HEADS, MLA_NOPE + MLA_V)
    w_k = jnp.concatenate([wkv[:, :, :MLA_NOPE], jnp.zeros((MLA_KV_LORA, MLA_HEADS, QK_PAD - MLA_NOPE),
                                                            wkv.dtype)], axis=2)
    w_v = wkv[:, :, MLA_NOPE:].reshape(MLA_KV_LORA, -1)

    def gate_w(w, row0):
        full = jnp.zeros((LANES, GLA_KDIM), w.dtype)
        return full.at[row0:row0 + GLA_GATE_RANK].set(w).astype(BF16)

    E = N_EXPERTS
    return {
        "attn_norm": attn_norm.reshape(1, D), "w_in": w_in_p,
        "q_norm": mla_q_norm.reshape(1, -1), "w_uqT": w_uq_p.T.astype(BF16),
        "kv_norm": mla_kv_norm.reshape(1, -1),
        "w_k": w_k.reshape(MLA_KV_LORA, -1).astype(BF16), "w_vT": w_v.T.astype(BF16),
        "w_gf": gate_w(gla_w_gate_fwd, 0), "b_gf": gla_b_gate_fwd.reshape(1, -1),
        "w_gb": gate_w(gla_w_gate_bwd, GLA_GATE_RANK), "b_gb": gla_b_gate_bwd.reshape(1, -1),
        "gla_out_norm": gla_out_norm.reshape(1, -1),
        "w_out_a": w_out[:MLA_WIDTH].astype(BF16), "w_out_b": w_out[MLA_WIDTH:].astype(BF16),
        "ffn_norm": ffn_norm.reshape(1, D),
        "router_wT": router_w.T, "router_b": router_b.reshape(E, 1),
        "w_gu": w_gu, "w_dn": w_dn,
        "b_glu": b_gu[:, 0::2].reshape(E, 1, D_FF), "b_lin": b_gu[:, 1::2].reshape(E, 1, D_FF),
        "b_dn": b_dn.reshape(E, 1, D),
    }


def _rope_tables(S):
    inv_freq = jnp.power(ROPE_THETA, -jnp.arange(0, MLA_ROPE, 2, dtype=F32) / MLA_ROPE)
    ang = jnp.arange(S, dtype=F32)[:, None] * inv_freq[None, :]
    cos, sin = jnp.cos(ang), jnp.sin(ang)
    z = lambda n: jnp.zeros((S, n), F32)
    tail = LANES - MLA_NOPE - MLA_ROPE
    return {
        "c": jnp.concatenate([z(MLA_NOPE), cos, cos, z(tail)], axis=1),
        "s1": jnp.concatenate([z(MLA_NOPE), -sin, z(HALF_ROPE), z(tail)], axis=1),
        "s2": jnp.concatenate([z(MLA_NOPE), z(HALF_ROPE), sin, z(tail)], axis=1),
        "cosT": jnp.tile(cos.T, (MLA_HEADS, 1)), "sinT": jnp.tile(sin.T, (MLA_HEADS, 1)),
    }


def _encoder(xa, xb, prm, final_norm):
    n_first, S, D = xa.shape
    B = n_first + xb.shape[0]
    rope = _rope_tables(S)
    qT, k, vT, gq, gk, gv, laf, lab, gr = _pre_call(xa, xb, prm, rope, ATTN_KEY_TILE)
    oT = _attn_call(qT, k, vT, min(ATTN_QUERY_TILE, S)).reshape(B, MLA_WIDTH, S)
    o_f, o_b = _gla_call(gq, gk, gv, laf, lab)
    x_mid, h2, meta, gates, counts = _post_call(xa, xb, oT, o_f, o_b, gr, prm, min(256, S))

    T = B * S
    n_rows = T * TOP_K + N_EXPERTS * MOE_BLOCK
    n_blocks = n_rows // MOE_BLOCK
    cnt = counts[:, 0].astype(jnp.int32)
    padded = ((cnt + MOE_BLOCK - 1) // MOE_BLOCK) * MOE_BLOCK
    pends = jnp.cumsum(padded)
    starts = (pends - padded).astype(jnp.int32)
    n_valid = (pends[-1] // MOE_BLOCK).astype(jnp.int32).reshape(1)
    blk = jnp.minimum(jnp.arange(n_blocks, dtype=jnp.int32), n_valid[0] - 1) * MOE_BLOCK
    block_expert = jnp.minimum(jnp.sum(pends[None, :] <= blk[:, None], axis=1), N_EXPERTS - 1).astype(jnp.int32)

    experts = jnp.arange(N_EXPERTS, dtype=jnp.int32)[:, None, None]
    slots = jnp.sum(jnp.where(meta[None, :TOP_K] == experts, starts[:, None, None], 0), axis=0) + meta[TOP_K:]
    slots = jnp.pad(slots.reshape(TOP_K, T // SC_WINDOW, SC_WINDOW),
                    ((0, 0), (0, 0), (0, SC_INDEX_PAD - SC_WINDOW)))

    xs = _sc_scatter_rows(h2.reshape(T, D), slots, n_rows)
    ys = _expert_call(block_expert, n_valid, xs, prm)
    y4 = _sc_gather_rows(ys, slots, T)
    ya, yb = _combine_call(x_mid.reshape(T, D), gates, final_norm.reshape(1, D), y4, min(256, S), n_first * S)
    return ya.reshape(xa.shape), yb.reshape(xb.shape)


def kernel(x_prompt, x_sample, attn_norm, w_in, mla_q_norm, mla_w_uq, mla_kv_norm, mla_w_ukv,
           gla_w_gate_fwd, gla_b_gate_fwd, gla_w_gate_bwd, gla_b_gate_bwd, gla_out_norm, w_out, ffn_norm,
           router_w, router_b, expert_w_gate_up, expert_b_gate_up, expert_w_down, expert_b_down,
           final_norm):
    layer = (attn_norm, w_in, mla_q_norm, mla_w_uq, mla_kv_norm, mla_w_ukv, gla_w_gate_fwd,
             gla_b_gate_fwd, gla_w_gate_bwd, gla_b_gate_bwd, gla_out_norm, w_out, ffn_norm, router_w,
             router_b, expert_w_gate_up, expert_b_gate_up, expert_w_down, expert_b_down)
    assert all(p.shape[0] == 1 for p in layer), "single layer expected"
    assert x_prompt.shape[1:] == x_sample.shape[1:]
    prm = _prep_params(*[p[0] for p in layer])
    return _encoder(x_prompt, x_sample, prm, final_norm)
```

```python
import functools

import jax
import jax.numpy as jnp
import numpy as np
from jax import lax
from jax.experimental import pallas as pl
from jax.experimental.pallas import tpu as pltpu
from jax.experimental.pallas import tpu_sc as plsc

F32 = jnp.float32
BF16 = jnp.bfloat16

D_MODEL = 1024
MLA_HEADS = 8
MLA_NOPE = 64
MLA_ROPE = 32
MLA_V = 64
MLA_Q_LORA = 384
MLA_KV_LORA = 256
ROPE_THETA = 10000.0
GLA_HEADS = 4
GLA_DK = 64
GLA_DV = 128
GLA_GATE_RANK = 16
GLA_GATE_NORM = 16.0
N_EXPERTS = 32
TOP_K = 4
D_FF = 1024
SWIGLU_LIMIT = 7.0
SWIGLU_ALPHA = 1.702
MOE_BLOCK = 512
RMS_EPS = 1e-6

MLA_WIDTH = MLA_HEADS * MLA_V
GLA_WIDTH = GLA_HEADS * GLA_DV
GLA_KDIM = GLA_HEADS * GLA_DK
HALF_ROPE = MLA_ROPE // 2
QK_PAD = 128
LANES = 128
V_ONES_ROWS = 16

OFF_CQ = 0
OFF_CKV = OFF_CQ + MLA_Q_LORA
OFF_KR = OFF_CKV + MLA_KV_LORA
OFF_GQ = OFF_KR + LANES
OFF_GK = OFF_GQ + GLA_KDIM
OFF_GV = OFF_GK + GLA_KDIM
OFF_LR = OFF_GV + GLA_WIDTH
OFF_GR = OFF_LR + LANES
PROJ_COLS = OFF_GR + GLA_WIDTH

ATTN_KEY_TILE = 256
ATTN_QUERY_TILE = 512
ATTN_SCORE_SLOTS = 4
GLA_TILE = 256
GLA_LEVELS = 8
SC_WINDOW = 32
SC_INDEX_PAD = 128
PERM_GROUP = 256
VMEM_LIMIT = 56 * 1024 * 1024

LOG2_E = 1.4426950408889634
NT_DIMS = (((1,), (1,)), ((), ()))
TN_DIMS = (((0,), (0,)), ((), ()))


def _cparams(sem):
    return pltpu.CompilerParams(dimension_semantics=sem, vmem_limit_bytes=VMEM_LIMIT)


def _rms(x, gain):
    return x * lax.rsqrt(jnp.mean(x * x, axis=-1, keepdims=True) + RMS_EPS) * gain


def _split_bf16(x):
    hi = x.astype(BF16)
    lo = (x - hi.astype(F32)).astype(BF16)
    return hi, lo


def _pre_kernel(xa_ref, xb_ref, an_ref, win_ref, qn_ref, wuqT_ref, kvn_ref, wk_ref, wvT_ref,
                wgf_ref, bgf_ref, wgb_ref, bgb_ref, rc_ref, rs1_ref, rs2_ref, cosT_ref, sinT_ref,
                qT_ref, k_ref, vT_ref, gq_ref, gk_ref, gv_ref, laf_ref, lab_ref, gr_ref, *, n_first):
    x = jnp.where(pl.program_id(0) < n_first, xa_ref[...], xb_ref[...])
    h = _rms(x, an_ref[...]).astype(BF16)
    proj = jnp.dot(h, win_ref[...], preferred_element_type=F32)

    cqn = _rms(proj[:, OFF_CQ:OFF_CQ + MLA_Q_LORA], qn_ref[...]).astype(BF16)
    ckvn = _rms(proj[:, OFF_CKV:OFF_CKV + MLA_KV_LORA], kvn_ref[...]).astype(BF16)

    scale = (MLA_NOPE + MLA_ROPE) ** -0.5 * LOG2_E
    qT = lax.dot_general(wuqT_ref[...], cqn, NT_DIMS, preferred_element_type=F32) * scale
    n_nope = MLA_HEADS * MLA_NOPE
    n_half = MLA_HEADS * HALF_ROPE
    x1 = qT[n_nope:n_nope + n_half]
    x2 = qT[n_nope + n_half:n_nope + 2 * n_half]
    c = cosT_ref[...]
    s = sinT_ref[...]
    x1r = x1 * c - x2 * s
    x2r = x1 * s + x2 * c
    zpad = jnp.zeros((QK_PAD - MLA_NOPE - MLA_ROPE, qT.shape[1]), BF16)
    for hd in range(MLA_HEADS):
        qT_ref[hd, 0:MLA_NOPE, :] = qT[hd * MLA_NOPE:(hd + 1) * MLA_NOPE].astype(BF16)
        qT_ref[hd, MLA_NOPE:MLA_NOPE + HALF_ROPE, :] = x1r[hd * HALF_ROPE:(hd + 1) * HALF_ROPE].astype(BF16)
        qT_ref[hd, MLA_NOPE + HALF_ROPE:MLA_NOPE + MLA_ROPE, :] = (
            x2r[hd * HALF_ROPE:(hd + 1) * HALF_ROPE].astype(BF16))
        qT_ref[hd, MLA_NOPE + MLA_ROPE:QK_PAD, :] = zpad

    kr = proj[:, OFF_KR:OFF_KR + LANES]
    kr = (kr * rc_ref[...]
          + pltpu.roll(kr, LANES - HALF_ROPE, axis=1) * rs1_ref[...]
          + pltpu.roll(kr, HALF_ROPE, axis=1) * rs2_ref[...])
    kfull = jnp.dot(ckvn, wk_ref[...], preferred_element_type=F32)
    for hd in range(MLA_HEADS):
        k_ref[hd] = (kfull[:, hd * QK_PAD:(hd + 1) * QK_PAD] + kr).astype(BF16)

    vT = lax.dot_general(wvT_ref[...], ckvn, NT_DIMS, preferred_element_type=F32)
    for hd in range(MLA_HEADS):
        vT_ref[hd] = vT[hd * MLA_V:(hd + 1) * MLA_V].astype(BF16)

    gq_ref[...] = proj[:, OFF_GQ:OFF_GQ + GLA_KDIM] * (GLA_DK ** -0.5)
    gk_ref[...] = proj[:, OFF_GK:OFF_GK + GLA_KDIM]
    gv_ref[...] = proj[:, OFF_GV:OFF_GV + GLA_WIDTH]
    gr_ref[...] = proj[:, OFF_GR:OFF_GR + GLA_WIDTH]
    lr = proj[:, OFF_LR:OFF_LR + LANES].astype(BF16)

    def log_decay(w_ref, b_ref):
        z = jnp.dot(lr, w_ref[...], preferred_element_type=F32) + b_ref[...]
        return (jnp.minimum(z, 0.0) - jnp.log1p(jnp.exp(-jnp.abs(z)))) * (1.0 / GLA_GATE_NORM)

    laf_ref[...] = log_decay(wgf_ref, bgf_ref)
    lab_ref[...] = log_decay(wgb_ref, bgb_ref)


def _two_source_specs(n_first, tm, width):
    return [pl.BlockSpec((None, tm, width), lambda b, i: (jnp.minimum(b, n_first - 1), i, 0)),
            pl.BlockSpec((None, tm, width), lambda b, i: (jnp.maximum(b - n_first, 0), i, 0))]


def _pre_call(xa, xb, prm, rope, tm):
    n_first, S, D = xa.shape
    B = n_first + xb.shape[0]
    nS = S // tm
    H = MLA_HEADS

    def full(a):
        nd = a.ndim
        return pl.BlockSpec(a.shape, lambda b, i, _nd=nd: (0,) * _nd)

    tok = lambda w: pl.BlockSpec((None, tm, w), lambda b, i: (b, i, 0))
    in_specs = [
        *_two_source_specs(n_first, tm, D), full(prm["attn_norm"]), full(prm["w_in"]), full(prm["q_norm"]), full(prm["w_uqT"]),
        full(prm["kv_norm"]), full(prm["w_k"]), full(prm["w_vT"]),
        full(prm["w_gf"]), full(prm["b_gf"]), full(prm["w_gb"]), full(prm["b_gb"]),
        pl.BlockSpec((tm, LANES), lambda b, i: (i, 0)),
        pl.BlockSpec((tm, LANES), lambda b, i: (i, 0)),
        pl.BlockSpec((tm, LANES), lambda b, i: (i, 0)),
        pl.BlockSpec((H * HALF_ROPE, tm), lambda b, i: (0, i)),
        pl.BlockSpec((H * HALF_ROPE, tm), lambda b, i: (0, i)),
    ]
    out_shape = [
        jax.ShapeDtypeStruct((B, H, QK_PAD, S), BF16),
        jax.ShapeDtypeStruct((B, H, S, QK_PAD), BF16),
        jax.ShapeDtypeStruct((B, H, nS, MLA_V, tm), BF16),
        jax.ShapeDtypeStruct((B, S, GLA_KDIM), F32),
        jax.ShapeDtypeStruct((B, S, GLA_KDIM), F32),
        jax.ShapeDtypeStruct((B, S, GLA_WIDTH), F32),
        jax.ShapeDtypeStruct((B, S, GLA_KDIM), F32),
        jax.ShapeDtypeStruct((B, S, GLA_KDIM), F32),
        jax.ShapeDtypeStruct((B, S, GLA_WIDTH), F32),
    ]
    out_specs = [
        pl.BlockSpec((None, H, QK_PAD, tm), lambda b, i: (b, 0, 0, i)),
        pl.BlockSpec((None, H, tm, QK_PAD), lambda b, i: (b, 0, i, 0)),
        pl.BlockSpec((None, H, None, MLA_V, tm), lambda b, i: (b, 0, i, 0, 0)),
        tok(GLA_KDIM), tok(GLA_KDIM), tok(GLA_WIDTH), tok(GLA_KDIM), tok(GLA_KDIM), tok(GLA_WIDTH),
    ]
    return pl.pallas_call(
        functools.partial(_pre_kernel, n_first=n_first), grid=(B, nS), in_specs=in_specs, out_specs=out_specs, out_shape=out_shape,
        compiler_params=_cparams(("parallel", "parallel")), name="pre",
    )(xa, xb, prm["attn_norm"], prm["w_in"], prm["q_norm"], prm["w_uqT"], prm["kv_norm"], prm["w_k"],
      prm["w_vT"], prm["w_gf"], prm["b_gf"], prm["w_gb"], prm["b_gb"],
      rope["c"], rope["s1"], rope["s2"], rope["cosT"], rope["sinT"])


def _attn_kernel(qT_ref, k_ref, vT_ref, o_ref, s_ref, p_ref, *, n_kblk):
    qT = qT_ref[...]
    tq = qT.shape[1]
    tk = k_ref.shape[0] // n_kblk

    def scores(j):
        sT = jnp.dot(k_ref[j * tk:(j + 1) * tk, :], qT, preferred_element_type=F32)
        s_ref[j % ATTN_SCORE_SLOTS] = sT
        return jnp.max(sT, axis=0, keepdims=True)

    def values(j):
        vext = jnp.concatenate([vT_ref[j], ones], axis=0)
        return jnp.dot(vext, p_ref[j % 2], preferred_element_type=F32)

    def softmax(j, blk_max, m):
        m_new = jnp.maximum(m, blk_max)
        p_ref[j % 2] = jnp.exp2(s_ref[j % ATTN_SCORE_SLOTS] - m_new).astype(BF16)
        return m_new, jnp.exp2(m - m_new)

    ones = jnp.ones((V_ONES_ROWS, tk), BF16)
    m = jnp.full((1, tq), -jnp.inf, F32)
    acc = jnp.zeros((MLA_V + V_ONES_ROWS, tq), F32)
    blk_max = {0: scores(0), 1: scores(1)}
    alpha_prev = None
    for j in range(n_kblk):
        pv = values(j - 1) if j > 0 else None
        if j + 2 < n_kblk:
            blk_max[j + 2] = scores(j + 2)
        m, alpha = softmax(j, blk_max.pop(j), m)
        if pv is not None:
            acc = alpha_prev * acc + pv
        alpha_prev = alpha
    acc = alpha_prev * acc + values(n_kblk - 1)
    o_ref[...] = (acc[0:MLA_V] / acc[MLA_V:MLA_V + 1]).astype(o_ref.dtype)


def _attn_call(qT, k, vT, tq):
    B, H, _, S = qT.shape
    n_kblk, tk = vT.shape[2], vT.shape[4]
    assert n_kblk >= 2 and S % tq == 0
    return pl.pallas_call(
        functools.partial(_attn_kernel, n_kblk=n_kblk),
        grid=(B, H, S // tq),
        in_specs=[
            pl.BlockSpec((None, None, QK_PAD, tq), lambda b, h, i: (b, h, 0, i)),
            pl.BlockSpec((None, None, S, QK_PAD), lambda b, h, i: (b, h, 0, 0)),
            pl.BlockSpec((None, None, n_kblk, MLA_V, tk), lambda b, h, i: (b, h, 0, 0, 0)),
        ],
        out_specs=pl.BlockSpec((None, None, MLA_V, tq), lambda b, h, i: (b, h, 0, i)),
        out_shape=jax.ShapeDtypeStruct((B, H, MLA_V, S), BF16),
        scratch_shapes=[pltpu.VMEM((ATTN_SCORE_SLOTS, tk, tq), F32), pltpu.VMEM((2, tk, tq), BF16)],
        compiler_params=_cparams(("parallel", "parallel", "parallel")), name="attn",
    )(qT, k, vT)


def _gla_consts():
    L = GLA_TILE
    i = np.arange(L)[:, None]
    j = np.arange(L)[None, :]
    x = i ^ j
    lidx = np.where(x > 0, np.floor(np.log2(np.maximum(x, 1))), -1).astype(np.int32)
    lidx_f = np.where(i > j, lidx, -1).astype(np.int32)
    lidx_b = np.where(i < j, lidx, -1).astype(np.int32)
    hd = np.arange(GLA_KDIM)[:, None] // GLA_DK
    hv = np.arange(GLA_WIDTH)[None, :] // GLA_DV
    bexp = (hd == hv).astype(np.float32)
    return jnp.asarray(lidx_f), jnp.asarray(lidx_b), jnp.asarray(bexp, BF16)


def _gla_direction(q, k, v, la, lidx, bexp_ref, ssum_ref, state_ref, o_ref, forward):
    L = GLA_TILE
    la_hi, la_lo = _split_bf16(la)
    vb = v.astype(BF16)
    lane_head = lax.broadcasted_iota(jnp.int32, (1, GLA_KDIM), 1) // GLA_DK
    head_masks = [(lane_head == h).astype(BF16) for h in range(GLA_HEADS)]
    row = lax.broadcasted_iota(jnp.int32, la.shape, 0)

    def widen(c, t, m):
        upper_half = (row & m) != 0
        sibling_total = jnp.where(upper_half, pltpu.roll(t, m, axis=0), pltpu.roll(t, L - m, axis=0))
        return c + jnp.where(upper_half, sibling_total, 0.0), t + sibling_total

    ssum_ref[...] = jnp.zeros_like(ssum_ref)
    c, t = la, la
    for lvl in range(GLA_LEVELS):
        if lvl > 0:
            c, t = widen(c, t, 1 << (lvl - 1))
        if forward:
            eq, ek = c, t - c
        else:
            eq, ek = t - c + la, c - la
        ql = (q * jnp.exp(eq)).astype(BF16)
        kl = (k * jnp.exp(ek)).astype(BF16)
        sel = lidx == lvl
        for h in range(GLA_HEADS):
            sc = lax.dot_general(ql * head_masks[h], kl, NT_DIMS, preferred_element_type=F32)
            ssum_ref[h] = jnp.where(sel, sc, ssum_ref[h])

    c, t = widen(c, t, L // 2)
    if forward:
        eq, ek = c, t - c
    else:
        eq, ek = t - c + la, c - la
    q_in = (q * jnp.exp(eq)).astype(BF16)
    k_out = (k * jnp.exp(ek)).astype(BF16)
    o = jnp.dot(q_in, state_ref[...].astype(BF16), preferred_element_type=F32)
    if forward:
        o = o + jnp.dot((q * k).astype(BF16), bexp_ref[...], preferred_element_type=F32) * v
    for h in range(GLA_HEADS):
        oh = jnp.dot(ssum_ref[h].astype(BF16), vb[:, h * GLA_DV:(h + 1) * GLA_DV],
                     preferred_element_type=F32)
        o_ref[:, h * GLA_DV:(h + 1) * GLA_DV] = o[:, h * GLA_DV:(h + 1) * GLA_DV] + oh

    ones = jnp.ones((L, LANES), BF16)
    tot_col = (lax.dot_general(la_hi, ones, TN_DIMS, preferred_element_type=F32)
               + lax.dot_general(la_lo, ones, TN_DIMS, preferred_element_type=F32))
    dec = jnp.exp(tot_col)
    upd = lax.dot_general(k_out, vb, TN_DIMS, preferred_element_type=F32)
    for h in range(GLA_HEADS):
        cols = slice(h * GLA_DV, (h + 1) * GLA_DV)
        state_ref[:, cols] = dec * state_ref[:, cols] + upd[:, cols] * bexp_ref[:, cols].astype(F32)


def _gla_kernel(qf_ref, kf_ref, vf_ref, laf_ref, qb_ref, kb_ref, vb_ref, lab_ref,
                lidxf_ref, lidxb_ref, bexp_ref, of_ref, ob_ref, sf_ref, sb_ref, ssum_ref):
    @pl.when(pl.program_id(1) == 0)
    def _():
        sf_ref[...] = jnp.zeros_like(sf_ref)
        sb_ref[...] = jnp.zeros_like(sb_ref)

    _gla_direction(qf_ref[...], kf_ref[...], vf_ref[...], laf_ref[...], lidxf_ref[...],
                   bexp_ref, ssum_ref, sf_ref, of_ref, True)
    _gla_direction(qb_ref[...], kb_ref[...], vb_ref[...], lab_ref[...], lidxb_ref[...],
                   bexp_ref, ssum_ref, sb_ref, ob_ref, False)


def _gla_call(gq, gk, gv, laf, lab):
    B, S, _ = gq.shape
    L = GLA_TILE
    n = S // L
    lidx_f, lidx_b, bexp = _gla_consts()
    fwd = lambda w: pl.BlockSpec((None, L, w), lambda b, i: (b, i, 0))
    bwd = lambda w: pl.BlockSpec((None, L, w), lambda b, i: (b, n - 1 - i, 0))
    const = lambda a: pl.BlockSpec(a.shape, lambda b, i, _nd=a.ndim: (0,) * _nd)
    return pl.pallas_call(
        _gla_kernel, grid=(B, n),
        in_specs=[fwd(GLA_KDIM), fwd(GLA_KDIM), fwd(GLA_WIDTH), fwd(GLA_KDIM),
                  bwd(GLA_KDIM), bwd(GLA_KDIM), bwd(GLA_WIDTH), bwd(GLA_KDIM),
                  const(lidx_f), const(lidx_b), const(bexp)],
        out_specs=[fwd(GLA_WIDTH), bwd(GLA_WIDTH)],
        out_shape=[jax.ShapeDtypeStruct((B, S, GLA_WIDTH), F32)] * 2,
        scratch_shapes=[pltpu.VMEM((GLA_KDIM, GLA_WIDTH), F32), pltpu.VMEM((GLA_KDIM, GLA_WIDTH), F32),
                        pltpu.VMEM((GLA_HEADS, L, L), F32)],
        compiler_params=_cparams(("parallel", "arbitrary")), name="gla",
    )(gq, gk, gv, laf, gq, gk, gv, lab, lidx_f, lidx_b, bexp)


def _post_kernel(xa_ref, xb_ref, oT_ref, of_ref, ob_ref, gr_ref, gon_ref, woa_ref, wob_ref, fn_ref,
                 rwT_ref, rb_ref, upper_ref,
                 xmid_ref, h2_ref, meta_ref, gates_ref, counts_ref, carry_ref, *, n_first):
    first = (pl.program_id(0) == 0) & (pl.program_id(1) == 0)

    @pl.when(first)
    def _():
        carry_ref[...] = jnp.zeros_like(carry_ref)

    o = of_ref[...] + ob_ref[...]
    gr = gr_ref[...]
    parts = []
    for h in range(GLA_HEADS):
        cols = slice(h * GLA_DV, (h + 1) * GLA_DV)
        parts.append(_rms(o[:, cols], gon_ref[...]) * jax.nn.silu(gr[:, cols]))
    gla = jnp.concatenate(parts, axis=1).astype(BF16)

    x_mid = (jnp.where(pl.program_id(0) < n_first, xa_ref[...], xb_ref[...])
             + lax.dot_general(oT_ref[...], woa_ref[...], TN_DIMS, preferred_element_type=F32)
             + jnp.dot(gla, wob_ref[...], preferred_element_type=F32))
    xmid_ref[...] = x_mid
    h2 = _rms(x_mid, fn_ref[...])
    h2_ref[...] = h2

    h_hi, h_lo = _split_bf16(h2)
    w_hi, w_lo = _split_bf16(rwT_ref[...])
    logits = (lax.dot_general(w_hi, h_hi, NT_DIMS, preferred_element_type=F32)
              + lax.dot_general(w_hi, h_lo, NT_DIMS, preferred_element_type=F32)
              + lax.dot_general(w_lo, h_hi, NT_DIMS, preferred_element_type=F32)
              + rb_ref[...])
    tm = logits.shape[1]
    eidx = lax.broadcasted_iota(jnp.int32, (N_EXPERTS, tm), 0).astype(F32)
    vals, idxs, sels = [], [], []
    cur = logits
    for _ in range(TOP_K):
        mk = jnp.max(cur, axis=0, keepdims=True)
        ik = jnp.min(jnp.where(cur == mk, eidx, float(N_EXPERTS)), axis=0, keepdims=True)
        sel = eidx == ik
        vals.append(mk)
        idxs.append(ik)
        sels.append(sel)
        cur = jnp.where(sel, -jnp.inf, cur)
    exps = [jnp.exp(vk - vals[0]) for vk in vals]
    denom = exps[0] + exps[1] + exps[2] + exps[3]
    gates = [e / denom for e in exps]

    cnt = (sels[0] | sels[1] | sels[2] | sels[3])
    before = jnp.dot(cnt.astype(BF16), upper_ref[...], preferred_element_type=F32) + carry_ref[:, 0:1]
    ranks = [jnp.sum(jnp.where(sel, before, 0.0), axis=0, keepdims=True) for sel in sels]
    carry_ref[...] = carry_ref[...] + jnp.sum(cnt.astype(F32), axis=1, keepdims=True)
    counts_ref[...] = carry_ref[...]

    meta_ref[...] = jnp.concatenate(idxs + ranks, axis=0).astype(jnp.int32)
    gates_ref[...] = jnp.concatenate(gates + [jnp.zeros((TOP_K, tm), F32)], axis=0)


def _post_call(xa, xb, oT, o_f, o_b, gr, prm, tm):
    n_first, S, D = xa.shape
    B = n_first + xb.shape[0]
    nS = S // tm
    upper = jnp.asarray(np.triu(np.ones((tm, tm), np.float32), k=1), BF16)
    full = lambda a: pl.BlockSpec(a.shape, lambda b, i, _nd=a.ndim: (0,) * _nd)
    tok = lambda w: pl.BlockSpec((None, tm, w), lambda b, i: (b, i, 0))
    colblk = lambda r: pl.BlockSpec((None, r, tm), lambda b, i: (b, 0, i))
    flat = lambda r: pl.BlockSpec((r, tm), lambda b, i: (0, b * nS + i))
    return pl.pallas_call(
        functools.partial(_post_kernel, n_first=n_first), grid=(B, nS),
        in_specs=[*_two_source_specs(n_first, tm, D), colblk(MLA_WIDTH), tok(GLA_WIDTH), tok(GLA_WIDTH), tok(GLA_WIDTH),
                  full(prm["gla_out_norm"]), full(prm["w_out_a"]), full(prm["w_out_b"]),
                  full(prm["ffn_norm"]), full(prm["router_wT"]), full(prm["router_b"]), full(upper)],
        out_specs=[tok(D), tok(D), flat(2 * TOP_K), flat(2 * TOP_K),
                   pl.BlockSpec((N_EXPERTS, LANES), lambda b, i: (0, 0))],
        out_shape=[jax.ShapeDtypeStruct((B, S, D), F32), jax.ShapeDtypeStruct((B, S, D), F32),
                   jax.ShapeDtypeStruct((2 * TOP_K, B * S), jnp.int32),
                   jax.ShapeDtypeStruct((2 * TOP_K, B * S), F32),
                   jax.ShapeDtypeStruct((N_EXPERTS, LANES), F32)],
        scratch_shapes=[pltpu.VMEM((N_EXPERTS, LANES), F32)],
        compiler_params=_cparams(("arbitrary", "arbitrary")), name="post",
    )(xa, xb, oT, o_f, o_b, gr, prm["gla_out_norm"], prm["w_out_a"], prm["w_out_b"], prm["ffn_norm"],
      prm["router_wT"], prm["router_b"], upper)


def _sc_mesh():
    return plsc.VectorSubcoreMesh(core_axis_name="core", subcore_axis_name="subcore")


def _sc_scatter_rows(x, slots, n_rows):
    T, D = x.shape
    K = slots.shape[0]

    @pl.kernel(out_type=jax.ShapeDtypeStruct((n_rows, D), x.dtype), mesh=_sc_mesh(), scratch_types=[])
    def scatter(x_hbm, slots_hbm, out_hbm):
        def window(x_vmem, slots_vmem):
            for kk in range(K):
                pltpu.sync_copy(x_vmem, out_hbm.at[slots_vmem.at[kk, 0, pl.ds(0, SC_WINDOW)]])

        pltpu.emit_pipeline(
            window, grid=(T // SC_WINDOW,),
            in_specs=[pl.BlockSpec((SC_WINDOW, D), lambda i: (i, 0)),
                      pl.BlockSpec((K, 1, SC_INDEX_PAD), lambda i: (0, i, 0))],
            out_specs=[], core_axis_name=("core", "subcore"),
            dimension_semantics=(pltpu.PARALLEL,))(x_hbm, slots_hbm)

    return scatter(x, slots)


def _sc_gather_rows(y, slots, T):
    D = y.shape[1]
    K = slots.shape[0]

    @pl.kernel(out_type=jax.ShapeDtypeStruct((K, T, D), y.dtype), mesh=_sc_mesh(), scratch_types=[])
    def gather(y_hbm, slots_hbm, out_hbm):
        def window(slots_vmem, out_vmem):
            pltpu.sync_copy(y_hbm.at[slots_vmem.at[0, 0, pl.ds(0, SC_WINDOW)]], out_vmem.at[0])

        pltpu.emit_pipeline(
            window, grid=(K, T // SC_WINDOW),
            in_specs=[pl.BlockSpec((1, 1, SC_INDEX_PAD), lambda kk, i: (kk, i, 0))],
            out_specs=[pl.BlockSpec((1, SC_WINDOW, D), lambda kk, i: (kk, i, 0))],
            core_axis_name=("core", "subcore"),
            dimension_semantics=(pltpu.PARALLEL, pltpu.PARALLEL))(slots_hbm, out_hbm)

    return gather(y, slots)


def _expert_kernel(be_ref, nv_ref, xs_ref, wgu_ref, bg_ref, bl_ref, wdn_ref, bd_ref, perm_ref, ys_ref,
                   wg_s, wl_s, wd_s):
    i = pl.program_id(0)
    valid = i < nv_ref[0]
    new_expert = valid & ((i == 0) | (be_ref[i] != be_ref[jnp.maximum(i - 1, 0)]))

    @pl.when(new_expert)
    def _():
        perm = perm_ref[...]
        half = PERM_GROUP // 2
        for c in range(2 * D_FF // PERM_GROUP):
            w = wgu_ref[:, c * PERM_GROUP:(c + 1) * PERM_GROUP].astype(BF16)
            sep = jnp.dot(w, perm, preferred_element_type=F32).astype(BF16)
            wg_s[:, c * half:(c + 1) * half] = sep[:, :half]
            wl_s[:, c * half:(c + 1) * half] = sep[:, half:]
        wd_s[...] = wdn_ref[...].astype(BF16)

    @pl.when(valid)
    def _():
        xb = xs_ref[...].astype(BF16)
        g = jnp.dot(xb, wg_s[...], preferred_element_type=F32) + bg_ref[...]
        l = jnp.dot(xb, wl_s[...], preferred_element_type=F32) + bl_ref[...]
        glu = jnp.minimum(g, SWIGLU_LIMIT)
        lin = jnp.clip(l, -SWIGLU_LIMIT, SWIGLU_LIMIT)
        act = glu * jax.nn.sigmoid(SWIGLU_ALPHA * glu) * (lin + 1.0)
        ys_ref[...] = jnp.dot(act.astype(BF16), wd_s[...], preferred_element_type=F32) + bd_ref[...]

    @pl.when(jnp.logical_not(valid))
    def _():
        ys_ref[...] = jnp.zeros_like(ys_ref)


def _expert_call(block_expert, n_valid, xs, prm):
    n_rows, D = xs.shape
    n_blocks = n_rows // MOE_BLOCK
    half = PERM_GROUP // 2
    src = np.concatenate([2 * np.arange(half), 2 * np.arange(half) + 1])
    perm = jnp.asarray(np.arange(PERM_GROUP)[:, None] == src[None, :], BF16)
    rows = lambda i, be, nv: (jnp.minimum(i, nv[0] - 1), 0)
    wsel = lambda i, be, nv: (be[i], 0, 0)
    grid_spec = pltpu.PrefetchScalarGridSpec(
        num_scalar_prefetch=2, grid=(n_blocks,),
        in_specs=[pl.BlockSpec((MOE_BLOCK, D), rows),
                  pl.BlockSpec((None, D, 2 * D_FF), wsel),
                  pl.BlockSpec((None, 1, D_FF), wsel), pl.BlockSpec((None, 1, D_FF), wsel),
                  pl.BlockSpec((None, D_FF, D), wsel), pl.BlockSpec((None, 1, D), wsel),
                  pl.BlockSpec((PERM_GROUP, PERM_GROUP), lambda i, be, nv: (0, 0))],
        out_specs=pl.BlockSpec((MOE_BLOCK, D), lambda i, be, nv: (i, 0)),
        scratch_shapes=[pltpu.VMEM((D, D_FF), BF16), pltpu.VMEM((D, D_FF), BF16),
                        pltpu.VMEM((D_FF, D), BF16)])
    return pl.pallas_call(
        _expert_kernel, grid_spec=grid_spec,
        out_shape=jax.ShapeDtypeStruct((n_rows, D), F32),
        compiler_params=_cparams(("arbitrary",)), name="experts",
    )(block_expert, n_valid, xs, prm["w_gu"], prm["b_glu"], prm["b_lin"], prm["w_dn"], prm["b_dn"], perm)


def _combine_kernel(xmid_ref, gates_ref, fnorm_ref, y4_ref, outa_ref, outb_ref, *, tiles_first):
    g = jnp.transpose(gates_ref[...])
    y = xmid_ref[...]
    for kk in range(TOP_K):
        y = y + y4_ref[kk] * g[:, kk:kk + 1]
    y = _rms(y, fnorm_ref[...])

    @pl.when(pl.program_id(0) < tiles_first)
    def _():
        outa_ref[...] = y

    @pl.when(pl.program_id(0) >= tiles_first)
    def _():
        outb_ref[...] = y


def _combine_call(x_mid, gates, final_norm, y4, tc, rows_first):
    T, D = x_mid.shape
    tiles_first = rows_first // tc
    return pl.pallas_call(
        functools.partial(_combine_kernel, tiles_first=tiles_first), grid=(T // tc,),
        in_specs=[pl.BlockSpec((tc, D), lambda i: (i, 0)),
                  pl.BlockSpec((2 * TOP_K, tc), lambda i: (0, i)),
                  pl.BlockSpec(final_norm.shape, lambda i: (0, 0)),
                  pl.BlockSpec((TOP_K, tc, D), lambda i: (0, i, 0))],
        out_specs=[pl.BlockSpec((tc, D), lambda i: (jnp.minimum(i, tiles_first - 1), 0)),
                   pl.BlockSpec((tc, D), lambda i: (jnp.maximum(i - tiles_first, 0), 0))],
        out_shape=[jax.ShapeDtypeStruct((rows_first, D), F32), jax.ShapeDtypeStruct((T - rows_first, D), F32)],
        compiler_params=_cparams(("arbitrary",)), name="combine",
    )(x_mid, gates, final_norm, y4)


def _prep_params(attn_norm, w_in, mla_q_norm, mla_w_uq, mla_kv_norm, mla_w_ukv, gla_w_gate_fwd,
                 gla_b_gate_fwd, gla_w_gate_bwd, gla_b_gate_bwd, gla_out_norm, w_out, ffn_norm,
                 router_w, router_b, w_gu, b_gu, w_dn, b_dn):
    D = D_MODEL
    o = np.cumsum((0, MLA_Q_LORA, MLA_KV_LORA, MLA_ROPE, GLA_KDIM, GLA_KDIM, GLA_WIDTH,
                   2 * GLA_GATE_RANK, GLA_WIDTH))
    seg = [w_in[:, o[n]:o[n + 1]] for n in range(8)]
    z = lambda n: jnp.zeros((D, n), w_in.dtype)
    w_in_p = jnp.concatenate(
        [seg[0], seg[1], z(MLA_NOPE), seg[2], z(LANES - MLA_NOPE - MLA_ROPE), seg[3], seg[4], seg[5],
         seg[6], z(LANES - 2 * GLA_GATE_RANK), seg[7]], axis=1).astype(BF16)

    wq = mla_w_uq.reshape(MLA_Q_LORA, MLA_HEADS, MLA_NOPE + MLA_ROPE)
    w_uq_p = jnp.concatenate(
        [wq[:, :, :MLA_NOPE].reshape(MLA_Q_LORA, -1),
         wq[:, :, MLA_NOPE:MLA_NOPE + HALF_ROPE].reshape(MLA_Q_LORA, -1),
         wq[:, :, MLA_NOPE + HALF_ROPE:].reshape(MLA_Q_LORA, -1)], axis=1)
    wkv = mla_w_ukv.reshape(MLA_KV_LORA, MLA_HEADS, MLA_NOPE + MLA_V)
    w_k = jnp.concatenate([wkv[:, :, :MLA_NOPE], jnp.zeros((MLA_KV_LORA, MLA_HEADS, QK_PAD - MLA_NOPE),
                                                            wkv.dtype)], axis=2)
    w_v = wkv[:, :, MLA_NOPE:].reshape(MLA_KV_LORA, -1)

    def gate_w(w, row0):
        full = jnp.zeros((LANES, GLA_KDIM), w.dtype)
        return full.at[row0:row0 + GLA_GATE_RANK].set(w).astype(BF16)

    E = N_EXPERTS
    return {
        "attn_norm": attn_norm.reshape(1, D), "w_in": w_in_p,
        "q_norm": mla_q_norm.reshape(1, -1), "w_uqT": w_uq_p.T.astype(BF16),
        "kv_norm": mla_kv_norm.reshape(1, -1),
        "w_k": w_k.reshape(MLA_KV_LORA, -1).astype(BF16), "w_vT": w_v.T.astype(BF16),
        "w_gf": gate_w(gla_w_gate_fwd, 0), "b_gf": gla_b_gate_fwd.reshape(1, -1),
        "w_gb": gate_w(gla_w_gate_bwd, GLA_GATE_RANK), "b_gb": gla_b_gate_bwd.reshape(1, -1),
        "gla_out_norm": gla_out_norm.reshape(1, -1),
        "w_out_a": w_out[:MLA_WIDTH].astype(BF16), "w_out_b": w_out[MLA_WIDTH:].astype(BF16),
        "ffn_norm": ffn_norm.reshape(1, D),
        "router_wT": router_w.T, "router_b": router_b.reshape(E, 1),
        "w_gu": w_gu, "w_dn": w_dn,
        "b_glu": b_gu[:, 0::2].reshape(E, 1, D_FF), "b_lin": b_gu[:, 1::2].reshape(E, 1, D_FF),
        "b_dn": b_dn.reshape(E, 1, D),
    }


def _rope_tables(S):
    inv_freq = jnp.power(ROPE_THETA, -jnp.arange(0, MLA_ROPE, 2, dtype=F32) / MLA_ROPE)
    ang = jnp.arange(S, dtype=F32)[:, None] * inv_freq[None, :]
    cos, sin = jnp.cos(ang), jnp.sin(ang)
    z = lambda n: jnp.zeros((S, n), F32)
    tail = LANES - MLA_NOPE - MLA_ROPE
    return {
        "c": jnp.concatenate([z(MLA_NOPE), cos, cos, z(tail)], axis=1),
        "s1": jnp.concatenate([z(MLA_NOPE), -sin, z(HALF_ROPE), z(tail)], axis=1),
        "s2": jnp.concatenate([z(MLA_NOPE), z(HALF_ROPE), sin, z(tail)], axis=1),
        "cosT": jnp.tile(cos.T, (MLA_HEADS, 1)), "sinT": jnp.tile(sin.T, (MLA_HEADS, 1)),
    }


def _encoder(xa, xb, prm, final_norm):
    n_first, S, D = xa.shape
    B = n_first + xb.shape[0]
    rope = _rope_tables(S)
    qT, k, vT, gq, gk, gv, laf, lab, gr = _pre_call(xa, xb, prm, rope, ATTN_KEY_TILE)
    oT = _attn_call(qT, k, vT, min(ATTN_QUERY_TILE, S)).reshape(B, MLA_WIDTH, S)
    o_f, o_b = _gla_call(gq, gk, gv, laf, lab)
    x_mid, h2, meta, gates, counts = _post_call(xa, xb, oT, o_f, o_b, gr, prm, min(256, S))

    T = B * S
    n_rows = T * TOP_K + N_EXPERTS * MOE_BLOCK
    n_blocks = n_rows // MOE_BLOCK
    cnt = counts[:, 0].astype(jnp.int32)
    padded = ((cnt + MOE_BLOCK - 1) // MOE_BLOCK) * MOE_BLOCK
    pends = jnp.cumsum(padded)
    starts = (pends - padded).astype(jnp.int32)
    n_valid = (pends[-1] // MOE_BLOCK).astype(jnp.int32).reshape(1)
    blk = jnp.minimum(jnp.arange(n_blocks, dtype=jnp.int32), n_valid[0] - 1) * MOE_BLOCK
    block_expert = jnp.minimum(jnp.sum(pends[None, :] <= blk[:, None], axis=1), N_EXPERTS - 1).astype(jnp.int32)

    experts = jnp.arange(N_EXPERTS, dtype=jnp.int32)[:, None, None]
    slots = jnp.sum(jnp.where(meta[None, :TOP_K] == experts, starts[:, None, None], 0), axis=0) + meta[TOP_K:]
    slots = jnp.pad(slots.reshape(TOP_K, T // SC_WINDOW, SC_WINDOW),
                    ((0, 0), (0, 0), (0, SC_INDEX_PAD - SC_WINDOW)))

    xs = _sc_scatter_rows(h2.reshape(T, D), slots, n_rows)
    ys = _expert_call(block_expert, n_valid, xs, prm)
    y4 = _sc_gather_rows(ys, slots, T)
    ya, yb = _combine_call(x_mid.reshape(T, D), gates, final_norm.reshape(1, D), y4, min(256, S), n_first * S)
    return ya.reshape(xa.shape), yb.reshape(xb.shape)


def kernel(x_prompt, x_sample, attn_norm, w_in, mla_q_norm, mla_w_uq, mla_kv_norm, mla_w_ukv,
           gla_w_gate_fwd, gla_b_gate_fwd, gla_w_gate_bwd, gla_b_gate_bwd, gla_out_norm, w_out, ffn_norm,
           router_w, router_b, expert_w_gate_up, expert_b_gate_up, expert_w_down, expert_b_down,
           final_norm):
    layer = (attn_norm, w_in, mla_q_norm, mla_w_uq, mla_kv_norm, mla_w_ukv, gla_w_gate_fwd,
             gla_b_gate_fwd, gla_w_gate_bwd, gla_b_gate_bwd, gla_out_norm, w_out, ffn_norm, router_w,
             router_b, expert_w_gate_up, expert_b_gate_up, expert_w_down, expert_b_down)
    assert all(p.shape[0] == 1 for p in layer), "single layer expected"
    assert x_prompt.shape[1:] == x_sample.shape[1:]
    prm = _prep_params(*[p[0] for p in layer])
    return _encoder(x_prompt, x_sample, prm, final_norm)
```

```python
import functools

import jax
import jax.numpy as jnp
import numpy as np
from jax import lax
from jax.experimental import pallas as pl
from jax.experimental.pallas import tpu as pltpu
from jax.experimental.pallas import tpu_sc as plsc

F32 = jnp.float32
BF16 = jnp.bfloat16

D_MODEL = 1024
MLA_HEADS = 8
MLA_NOPE = 64
MLA_ROPE = 32
MLA_V = 64
MLA_Q_LORA = 384
MLA_KV_LORA = 256
ROPE_THETA = 10000.0
GLA_HEADS = 4
GLA_DK = 64
GLA_DV = 128
GLA_GATE_RANK = 16
GLA_GATE_NORM = 16.0
N_EXPERTS = 32
TOP_K = 4
D_FF = 1024
SWIGLU_LIMIT = 7.0
SWIGLU_ALPHA = 1.702
MOE_BLOCK = 512
RMS_EPS = 1e-6

MLA_WIDTH = MLA_HEADS * MLA_V
GLA_WIDTH = GLA_HEADS * GLA_DV
GLA_KDIM = GLA_HEADS * GLA_DK
HALF_ROPE = MLA_ROPE // 2
QK_PAD = 128
LANES = 128
V_ONES_ROWS = 16

OFF_CQ = 0
OFF_CKV = OFF_CQ + MLA_Q_LORA
OFF_KR = OFF_CKV + MLA_KV_LORA
OFF_GQ = OFF_KR + LANES
OFF_GK = OFF_GQ + GLA_KDIM
OFF_GV = OFF_GK + GLA_KDIM
OFF_LR = OFF_GV + GLA_WIDTH
OFF_GR = OFF_LR + LANES
PROJ_COLS = OFF_GR + GLA_WIDTH

ATTN_KEY_TILE = 256
ATTN_QUERY_TILE = 512
ATTN_SCORE_SLOTS = 4
GLA_TILE = 256
GLA_LEVELS = 8
SC_WINDOW = 32
SC_INDEX_PAD = 128
PERM_GROUP = 256
VMEM_LIMIT = 56 * 1024 * 1024

LOG2_E = 1.4426950408889634
NT_DIMS = (((1,), (1,)), ((), ()))
TN_DIMS = (((0,), (0,)), ((), ()))


def _cparams(sem):
    return pltpu.CompilerParams(dimension_semantics=sem, vmem_limit_bytes=VMEM_LIMIT)


def _rms(x, gain):
    return x * lax.rsqrt(jnp.mean(x * x, axis=-1, keepdims=True) + RMS_EPS) * gain


def _split_bf16(x):
    hi = x.astype(BF16)
    lo = (x - hi.astype(F32)).astype(BF16)
    return hi, lo


def _pre_kernel(xa_ref, xb_ref, an_ref, win_ref, qn_ref, wuqT_ref, kvn_ref, wk_ref, wvT_ref,
                wgf_ref, bgf_ref, wgb_ref, bgb_ref, rc_ref, rs1_ref, rs2_ref, cosT_ref, sinT_ref,
                qT_ref, k_ref, vT_ref, gq_ref, gk_ref, gv_ref, laf_ref, lab_ref, gr_ref, *, n_first):
    x = jnp.where(pl.program_id(0) < n_first, xa_ref[...], xb_ref[...])
    h = _rms(x, an_ref[...]).astype(BF16)
    proj = jnp.dot(h, win_ref[...], preferred_element_type=F32)

    cqn = _rms(proj[:, OFF_CQ:OFF_CQ + MLA_Q_LORA], qn_ref[...]).astype(BF16)
    ckvn = _rms(proj[:, OFF_CKV:OFF_CKV + MLA_KV_LORA], kvn_ref[...]).astype(BF16)

    scale = (MLA_NOPE + MLA_ROPE) ** -0.5 * LOG2_E
    qT = lax.dot_general(wuqT_ref[...], cqn, NT_DIMS, preferred_element_type=F32) * scale
    n_nope = MLA_HEADS * MLA_NOPE
    n_half = MLA_HEADS * HALF_ROPE
    x1 = qT[n_nope:n_nope + n_half]
    x2 = qT[n_nope + n_half:n_nope + 2 * n_half]
    c = cosT_ref[...]
    s = sinT_ref[...]
    x1r = x1 * c - x2 * s
    x2r = x1 * s + x2 * c
    zpad = jnp.zeros((QK_PAD - MLA_NOPE - MLA_ROPE, qT.shape[1]), BF16)
    for hd in range(MLA_HEADS):
        qT_ref[hd, 0:MLA_NOPE, :] = qT[hd * MLA_NOPE:(hd + 1) * MLA_NOPE].astype(BF16)
        qT_ref[hd, MLA_NOPE:MLA_NOPE + HALF_ROPE, :] = x1r[hd * HALF_ROPE:(hd + 1) * HALF_ROPE].astype(BF16)
        qT_ref[hd, MLA_NOPE + HALF_ROPE:MLA_NOPE + MLA_ROPE, :] = (
            x2r[hd * HALF_ROPE:(hd + 1) * HALF_ROPE].astype(BF16))
        qT_ref[hd, MLA_NOPE + MLA_ROPE:QK_PAD, :] = zpad

    kr = proj[:, OFF_KR:OFF_KR + LANES]
    kr = (kr * rc_ref[...]
          + pltpu.roll(kr, LANES - HALF_ROPE, axis=1) * rs1_ref[...]
          + pltpu.roll(kr, HALF_ROPE, axis=1) * rs2_ref[...])
    kfull = jnp.dot(ckvn, wk_ref[...], preferred_element_type=F32)
    for hd in range(MLA_HEADS):
        k_ref[hd] = (kfull[:, hd * QK_PAD:(hd + 1) * QK_PAD] + kr).astype(BF16)

    vT = lax.dot_general(wvT_ref[...], ckvn, NT_DIMS, preferred_element_type=F32)
    for hd in range(MLA_HEADS):
        vT_ref[hd] = vT[hd * MLA_V:(hd + 1) * MLA_V].astype(BF16)

    gq_ref[...] = proj[:, OFF_GQ:OFF_GQ + GLA_KDIM] * (GLA_DK ** -0.5)
    gk_ref[...] = proj[:, OFF_GK:OFF_GK + GLA_KDIM]
    gv_ref[...] = proj[:, OFF_GV:OFF_GV + GLA_WIDTH]
    gr_ref[...] = proj[:, OFF_GR:OFF_GR + GLA_WIDTH]
    lr = proj[:, OFF_LR:OFF_LR + LANES].astype(BF16)

    def log_decay(w_ref, b_ref):
        z = jnp.dot(lr, w_ref[...], preferred_element_type=F32) + b_ref[...]
        return (jnp.minimum(z, 0.0) - jnp.log1p(jnp.exp(-jnp.abs(z)))) * (1.0 / GLA_GATE_NORM)

    laf_ref[...] = log_decay(wgf_ref, bgf_ref)
    lab_ref[...] = log_decay(wgb_ref, bgb_ref)


def _two_source_specs(n_first, tm, width):
    return [pl.BlockSpec((None, tm, width), lambda b, i: (jnp.minimum(b, n_first - 1), i, 0)),
            pl.BlockSpec((None, tm, width), lambda b, i: (jnp.maximum(b - n_first, 0), i, 0))]


def _pre_call(xa, xb, prm, rope, tm):
    n_first, S, D = xa.shape
    B = n_first + xb.shape[0]
    nS = S // tm
    H = MLA_HEADS

    def full(a):
        nd = a.ndim
        return pl.BlockSpec(a.shape, lambda b, i, _nd=nd: (0,) * _nd)

    tok = lambda w: pl.BlockSpec((None, tm, w), lambda b, i: (b, i, 0))
    in_specs = [
        *_two_source_specs(n_first, tm, D), full(prm["attn_norm"]), full(prm["w_in"]), full(prm["q_norm"]), full(prm["w_uqT"]),
        full(prm["kv_norm"]), full(prm["w_k"]), full(prm["w_vT"]),
        full(prm["w_gf"]), full(prm["b_gf"]), full(prm["w_gb"]), full(prm["b_gb"]),
        pl.BlockSpec((tm, LANES), lambda b, i: (i, 0)),
        pl.BlockSpec((tm, LANES), lambda b, i: (i, 0)),
        pl.BlockSpec((tm, LANES), lambda b, i: (i, 0)),
        pl.BlockSpec((H * HALF_ROPE, tm), lambda b, i: (0, i)),
        pl.BlockSpec((H * HALF_ROPE, tm), lambda b, i: (0, i)),
    ]
    out_shape = [
        jax.ShapeDtypeStruct((B, H, QK_PAD, S), BF16),
        jax.ShapeDtypeStruct((B, H, S, QK_PAD), BF16),
        jax.ShapeDtypeStruct((B, H, nS, MLA_V, tm), BF16),
        jax.ShapeDtypeStruct((B, S, GLA_KDIM), F32),
        jax.ShapeDtypeStruct((B, S, GLA_KDIM), F32),
        jax.ShapeDtypeStruct((B, S, GLA_WIDTH), F32),
        jax.ShapeDtypeStruct((B, S, GLA_KDIM), F32),
        jax.ShapeDtypeStruct((B, S, GLA_KDIM), F32),
        jax.ShapeDtypeStruct((B, S, GLA_WIDTH), F32),
    ]
    out_specs = [
        pl.BlockSpec((None, H, QK_PAD, tm), lambda b, i: (b, 0, 0, i)),
        pl.BlockSpec((None, H, tm, QK_PAD), lambda b, i: (b, 0, i, 0)),
        pl.BlockSpec((None, H, None, MLA_V, tm), lambda b, i: (b, 0, i, 0, 0)),
        tok(GLA_KDIM), tok(GLA_KDIM), tok(GLA_WIDTH), tok(GLA_KDIM), tok(GLA_KDIM), tok(GLA_WIDTH),
    ]
    return pl.pallas_call(
        functools.partial(_pre_kernel, n_first=n_first), grid=(B, nS), in_specs=in_specs, out_specs=out_specs, out_shape=out_shape,
        compiler_params=_cparams(("parallel", "parallel")), name="pre",
    )(xa, xb, prm["attn_norm"], prm["w_in"], prm["q_norm"], prm["w_uqT"], prm["kv_norm"], prm["w_k"],
      prm["w_vT"], prm["w_gf"], prm["b_gf"], prm["w_gb"], prm["b_gb"],
      rope["c"], rope["s1"], rope["s2"], rope["cosT"], rope["sinT"])


def _attn_kernel(qT_ref, k_ref, vT_ref, o_ref, s_ref, p_ref, *, n_kblk):
    qT = qT_ref[...]
    tq = qT.shape[1]
    tk = k_ref.shape[0] // n_kblk

    def scores(j):
        sT = jnp.dot(k_ref[j * tk:(j + 1) * tk, :], qT, preferred_element_type=F32)
        s_ref[j % ATTN_SCORE_SLOTS] = sT
        return jnp.max(sT, axis=0, keepdims=True)

    def values(j):
        vext = jnp.concatenate([vT_ref[j], ones], axis=0)
        return jnp.dot(vext, p_ref[j % 2], preferred_element_type=F32)

    def softmax(j, blk_max, m):
        m_new = jnp.maximum(m, blk_max)
        p_ref[j % 2] = jnp.exp2(s_ref[j % ATTN_SCORE_SLOTS] - m_new).astype(BF16)
        return m_new, jnp.exp2(m - m_new)

    ones = jnp.ones((V_ONES_ROWS, tk), BF16)
    m = jnp.full((1, tq), -jnp.inf, F32)
    acc = jnp.zeros((MLA_V + V_ONES_ROWS, tq), F32)
    blk_max = {0: scores(0), 1: scores(1)}
    alpha_prev = None
    for j in range(n_kblk):
        pv = values(j - 1) if j > 0 else None
        if j + 2 < n_kblk:
            blk_max[j + 2] = scores(j + 2)
        m, alpha = softmax(j, blk_max.pop(j), m)
        if pv is not None:
            acc = alpha_prev * acc + pv
        alpha_prev = alpha
    acc = alpha_prev * acc + values(n_kblk - 1)
    o_ref[...] = (acc[0:MLA_V] / acc[MLA_V:MLA_V + 1]).astype(o_ref.dtype)


def _attn_call(qT, k, vT, tq):
    B, H, _, S = qT.shape
    n_kblk, tk = vT.shape[2], vT.shape[4]
    assert n_kblk >= 2 and S % tq == 0
    return pl.pallas_call(
        functools.partial(_attn_kernel, n_kblk=n_kblk),
        grid=(B, H, S // tq),
        in_specs=[
            pl.BlockSpec((None, None, QK_PAD, tq), lambda b, h, i: (b, h, 0, i)),
            pl.BlockSpec((None, None, S, QK_PAD), lambda b, h, i: (b, h, 0, 0)),
            pl.BlockSpec((None, None, n_kblk, MLA_V, tk), lambda b, h, i: (b, h, 0, 0, 0)),
        ],
        out_specs=pl.BlockSpec((None, None, MLA_V, tq), lambda b, h, i: (b, h, 0, i)),
        out_shape=jax.ShapeDtypeStruct((B, H, MLA_V, S), BF16),
        scratch_shapes=[pltpu.VMEM((ATTN_SCORE_SLOTS, tk, tq), F32), pltpu.VMEM((2, tk, tq), BF16)],
        compiler_params=_cparams(("parallel", "parallel", "parallel")), name="attn",
    )(qT, k, vT)


def _gla_consts():
    L = GLA_TILE
    i = np.arange(L)[:, None]
    j = np.arange(L)[None, :]
    x = i ^ j
    lidx = np.where(x > 0, np.floor(np.log2(np.maximum(x, 1))), -1).astype(np.int32)
    lidx_f = np.where(i > j, lidx, -1).astype(np.int32)
    lidx_b = np.where(i < j, lidx, -1).astype(np.int32)
    hd = np.arange(GLA_KDIM)[:, None] // GLA_DK
    hv = np.arange(GLA_WIDTH)[None, :] // GLA_DV
    bexp = (hd == hv).astype(np.float32)
    return jnp.asarray(lidx_f), jnp.asarray(lidx_b), jnp.asarray(bexp, BF16)


def _gla_direction(q, k, v, la, lidx, bexp_ref, ssum_ref, state_ref, o_ref, forward):
    L = GLA_TILE
    la_hi, la_lo = _split_bf16(la)
    vb = v.astype(BF16)
    lane_head = lax.broadcasted_iota(jnp.int32, (1, GLA_KDIM), 1) // GLA_DK
    head_masks = [(lane_head == h).astype(BF16) for h in range(GLA_HEADS)]
    row = lax.broadcasted_iota(jnp.int32, la.shape, 0)

    def widen(c, t, m):
        upper_half = (row & m) != 0
        sibling_total = jnp.where(upper_half, pltpu.roll(t, m, axis=0), pltpu.roll(t, L - m, axis=0))
        return c + jnp.where(upper_half, sibling_total, 0.0), t + sibling_total

    ssum_ref[...] = jnp.zeros_like(ssum_ref)
    c, t = la, la
    for lvl in range(GLA_LEVELS):
        if lvl > 0:
            c, t = widen(c, t, 1 << (lvl - 1))
        if forward:
            eq, ek = c, t - c
        else:
            eq, ek = t - c + la, c - la
        ql = (q * jnp.exp(eq)).astype(BF16)
        kl = (k * jnp.exp(ek)).astype(BF16)
        sel = lidx == lvl
        for h in range(GLA_HEADS):
            sc = lax.dot_general(ql * head_masks[h], kl, NT_DIMS, preferred_element_type=F32)
            ssum_ref[h] = jnp.where(sel, sc, ssum_ref[h])

    c, t = widen(c, t, L // 2)
    if forward:
        eq, ek = c, t - c
    else:
        eq, ek = t - c + la, c - la
    q_in = (q * jnp.exp(eq)).astype(BF16)
    k_out = (k * jnp.exp(ek)).astype(BF16)
    o = jnp.dot(q_in, state_ref[...].astype(BF16), preferred_element_type=F32)
    if forward:
        o = o + jnp.dot((q * k).astype(BF16), bexp_ref[...], preferred_element_type=F32) * v
    for h in range(GLA_HEADS):
        oh = jnp.dot(ssum_ref[h].astype(BF16), vb[:, h * GLA_DV:(h + 1) * GLA_DV],
                     preferred_element_type=F32)
        o_ref[:, h * GLA_DV:(h + 1) * GLA_DV] = o[:, h * GLA_DV:(h + 1) * GLA_DV] + oh

    ones = jnp.ones((L, LANES), BF16)
    tot_col = (lax.dot_general(la_hi, ones, TN_DIMS, preferred_element_type=F32)
               + lax.dot_general(la_lo, ones, TN_DIMS, preferred_element_type=F32))
    dec = jnp.exp(tot_col)
    upd = lax.dot_general(k_out, vb, TN_DIMS, preferred_element_type=F32)
    for h in range(GLA_HEADS):
        cols = slice(h * GLA_DV, (h + 1) * GLA_DV)
        state_ref[:, cols] = dec * state_ref[:, cols] + upd[:, cols] * bexp_ref[:, cols].astype(F32)


def _gla_kernel(qf_ref, kf_ref, vf_ref, laf_ref, qb_ref, kb_ref, vb_ref, lab_ref,
                lidxf_ref, lidxb_ref, bexp_ref, of_ref, ob_ref, sf_ref, sb_ref, ssum_ref):
    @pl.when(pl.program_id(1) == 0)
    def _():
        sf_ref[...] = jnp.zeros_like(sf_ref)
        sb_ref[...] = jnp.zeros_like(sb_ref)

    _gla_direction(qf_ref[...], kf_ref[...], vf_ref[...], laf_ref[...], lidxf_ref[...],
                   bexp_ref, ssum_ref, sf_ref, of_ref, True)
    _gla_direction(qb_ref[...], kb_ref[...], vb_ref[...], lab_ref[...], lidxb_ref[...],
                   bexp_ref, ssum_ref, sb_ref, ob_ref, False)


def _gla_call(gq, gk, gv, laf, lab):
    B, S, _ = gq.shape
    L = GLA_TILE
    n = S // L
    lidx_f, lidx_b, bexp = _gla_consts()
    fwd = lambda w: pl.BlockSpec((None, L, w), lambda b, i: (b, i, 0))
    bwd = lambda w: pl.BlockSpec((None, L, w), lambda b, i: (b, n - 1 - i, 0))
    const = lambda a: pl.BlockSpec(a.shape, lambda b, i, _nd=a.ndim: (0,) * _nd)
    return pl.pallas_call(
        _gla_kernel, grid=(B, n),
        in_specs=[fwd(GLA_KDIM), fwd(GLA_KDIM), fwd(GLA_WIDTH), fwd(GLA_KDIM),
                  bwd(GLA_KDIM), bwd(GLA_KDIM), bwd(GLA_WIDTH), bwd(GLA_KDIM),
                  const(lidx_f), const(lidx_b), const(bexp)],
        out_specs=[fwd(GLA_WIDTH), bwd(GLA_WIDTH)],
        out_shape=[jax.ShapeDtypeStruct((B, S, GLA_WIDTH), F32)] * 2,
        scratch_shapes=[pltpu.VMEM((GLA_KDIM, GLA_WIDTH), F32), pltpu.VMEM((GLA_KDIM, GLA_WIDTH), F32),
                        pltpu.VMEM((GLA_HEADS, L, L), F32)],
        compiler_params=_cparams(("parallel", "arbitrary")), name="gla",
    )(gq, gk, gv, laf, gq, gk, gv, lab, lidx_f, lidx_b, bexp)


def _post_kernel(x_ref, oT_ref, of_ref, ob_ref, gr_ref, gon_ref, woa_ref, wob_ref, fn_ref,
                 rwT_ref, rb_ref, upper_ref,
                 xmid_ref, h2_ref, meta_ref, gates_ref, counts_ref, carry_ref):
    first = (pl.program_id(0) == 0) & (pl.program_id(1) == 0)

    @pl.when(first)
    def _():
        carry_ref[...] = jnp.zeros_like(carry_ref)

    o = of_ref[...] + ob_ref[...]
    gr = gr_ref[...]
    parts = []
    for h in range(GLA_HEADS):
        cols = slice(h * GLA_DV, (h + 1) * GLA_DV)
        parts.append(_rms(o[:, cols], gon_ref[...]) * jax.nn.silu(gr[:, cols]))
    gla = jnp.concatenate(parts, axis=1).astype(BF16)

    x_mid = (x_ref[...]
             + lax.dot_general(oT_ref[...], woa_ref[...], TN_DIMS, preferred_element_type=F32)
             + jnp.dot(gla, wob_ref[...], preferred_element_type=F32))
    xmid_ref[...] = x_mid
    h2 = _rms(x_mid, fn_ref[...])
    h2_ref[...] = h2

    h_hi, h_lo = _split_bf16(h2)
    w_hi, w_lo = _split_bf16(rwT_ref[...])
    logits = (lax.dot_general(w_hi, h_hi, NT_DIMS, preferred_element_type=F32)
              + lax.dot_general(w_hi, h_lo, NT_DIMS, preferred_element_type=F32)
              + lax.dot_general(w_lo, h_hi, NT_DIMS, preferred_element_type=F32)
              + rb_ref[...])
    tm = logits.shape[1]
    eidx = lax.broadcasted_iota(jnp.int32, (N_EXPERTS, tm), 0).astype(F32)
    vals, idxs, sels = [], [], []
    cur = logits
    for _ in range(TOP_K):
        mk = jnp.max(cur, axis=0, keepdims=True)
        ik = jnp.min(jnp.where(cur == mk, eidx, float(N_EXPERTS)), axis=0, keepdims=True)
        sel = eidx == ik
        vals.append(mk)
        idxs.append(ik)
        sels.append(sel)
        cur = jnp.where(sel, -jnp.inf, cur)
    exps = [jnp.exp(vk - vals[0]) for vk in vals]
    denom = exps[0] + exps[1] + exps[2] + exps[3]
    gates = [e / denom for e in exps]

    cnt = (sels[0] | sels[1] | sels[2] | sels[3])
    before = jnp.dot(cnt.astype(BF16), upper_ref[...], preferred_element_type=F32) + carry_ref[:, 0:1]
    ranks = [jnp.sum(jnp.where(sel, before, 0.0), axis=0, keepdims=True) for sel in sels]
    carry_ref[...] = carry_ref[...] + jnp.sum(cnt.astype(F32), axis=1, keepdims=True)
    counts_ref[...] = carry_ref[...]

    meta_ref[...] = jnp.concatenate(idxs + ranks, axis=0).astype(jnp.int32)
    gates_ref[...] = jnp.concatenate(gates + [jnp.zeros((TOP_K, tm), F32)], axis=0)


def _post_call(x, batch0, oT, o_f, o_b, gr, prm, tm):
    B, S, D = x.shape
    nS = S // tm
    upper = jnp.asarray(np.triu(np.ones((tm, tm), np.float32), k=1), BF16)
    full = lambda a: pl.BlockSpec(a.shape, lambda b, i, _nd=a.ndim: (0,) * _nd)
    own = lambda w: pl.BlockSpec((None, tm, w), lambda b, i: (b, i, 0))
    tok = lambda w: pl.BlockSpec((None, tm, w), lambda b, i: (batch0 + b, i, 0))
    colblk = lambda r: pl.BlockSpec((None, r, tm), lambda b, i: (batch0 + b, 0, i))
    flat = lambda r: pl.BlockSpec((r, tm), lambda b, i: (0, b * nS + i))
    return pl.pallas_call(
        _post_kernel, grid=(B, nS),
        in_specs=[own(D), colblk(MLA_WIDTH), tok(GLA_WIDTH), tok(GLA_WIDTH), tok(GLA_WIDTH),
                  full(prm["gla_out_norm"]), full(prm["w_out_a"]), full(prm["w_out_b"]),
                  full(prm["ffn_norm"]), full(prm["router_wT"]), full(prm["router_b"]), full(upper)],
        out_specs=[own(D), own(D), flat(2 * TOP_K), flat(2 * TOP_K),
                   pl.BlockSpec((N_EXPERTS, LANES), lambda b, i: (0, 0))],
        out_shape=[jax.ShapeDtypeStruct((B, S, D), F32), jax.ShapeDtypeStruct((B, S, D), F32),
                   jax.ShapeDtypeStruct((2 * TOP_K, B * S), jnp.int32),
                   jax.ShapeDtypeStruct((2 * TOP_K, B * S), F32),
                   jax.ShapeDtypeStruct((N_EXPERTS, LANES), F32)],
        scratch_shapes=[pltpu.VMEM((N_EXPERTS, LANES), F32)],
        compiler_params=_cparams(("arbitrary", "arbitrary")), name="post",
    )(x, oT, o_f, o_b, gr, prm["gla_out_norm"], prm["w_out_a"], prm["w_out_b"], prm["ffn_norm"],
      prm["router_wT"], prm["router_b"], upper)


def _sc_mesh():
    return plsc.VectorSubcoreMesh(core_axis_name="core", subcore_axis_name="subcore")


def _sc_scatter_rows(x, slots, n_rows):
    T, D = x.shape
    K = slots.shape[0]

    @pl.kernel(out_type=jax.ShapeDtypeStruct((n_rows, D), x.dtype), mesh=_sc_mesh(), scratch_types=[])
    def scatter(x_hbm, slots_hbm, out_hbm):
        def window(x_vmem, slots_vmem):
            for kk in range(K):
                pltpu.sync_copy(x_vmem, out_hbm.at[slots_vmem.at[kk, 0, pl.ds(0, SC_WINDOW)]])

        pltpu.emit_pipeline(
            window, grid=(T // SC_WINDOW,),
            in_specs=[pl.BlockSpec((SC_WINDOW, D), lambda i: (i, 0)),
                      pl.BlockSpec((K, 1, SC_INDEX_PAD), lambda i: (0, i, 0))],
            out_specs=[], core_axis_name=("core", "subcore"),
            dimension_semantics=(pltpu.PARALLEL,))(x_hbm, slots_hbm)

    return scatter(x, slots)


def _sc_gather_rows(y, slots, T):
    D = y.shape[1]
    K = slots.shape[0]

    @pl.kernel(out_type=jax.ShapeDtypeStruct((K, T, D), y.dtype), mesh=_sc_mesh(), scratch_types=[])
    def gather(y_hbm, slots_hbm, out_hbm):
        def window(slots_vmem, out_vmem):
            pltpu.sync_copy(y_hbm.at[slots_vmem.at[0, 0, pl.ds(0, SC_WINDOW)]], out_vmem.at[0])

        pltpu.emit_pipeline(
            window, grid=(K, T // SC_WINDOW),
            in_specs=[pl.BlockSpec((1, 1, SC_INDEX_PAD), lambda kk, i: (kk, i, 0))],
            out_specs=[pl.BlockSpec((1, SC_WINDOW, D), lambda kk, i: (kk, i, 0))],
            core_axis_name=("core", "subcore"),
            dimension_semantics=(pltpu.PARALLEL, pltpu.PARALLEL))(slots_hbm, out_hbm)

    return gather(y, slots)


def _expert_kernel(be_ref, nv_ref, xs_ref, wgu_ref, bg_ref, bl_ref, wdn_ref, bd_ref, perm_ref, ys_ref,
                   wg_s, wl_s, wd_s):
    i = pl.program_id(0)
    valid = i < nv_ref[0]
    new_expert = valid & ((i == 0) | (be_ref[i] != be_ref[jnp.maximum(i - 1, 0)]))

    @pl.when(new_expert)
    def _():
        perm = perm_ref[...]
        half = PERM_GROUP // 2
        for c in range(2 * D_FF // PERM_GROUP):
            w = wgu_ref[:, c * PERM_GROUP:(c + 1) * PERM_GROUP].astype(BF16)
            sep = jnp.dot(w, perm, preferred_element_type=F32).astype(BF16)
            wg_s[:, c * half:(c + 1) * half] = sep[:, :half]
            wl_s[:, c * half:(c + 1) * half] = sep[:, half:]
        wd_s[...] = wdn_ref[...].astype(BF16)

    @pl.when(valid)
    def _():
        xb = xs_ref[...].astype(BF16)
        g = jnp.dot(xb, wg_s[...], preferred_element_type=F32) + bg_ref[...]
        l = jnp.dot(xb, wl_s[...], preferred_element_type=F32) + bl_ref[...]
        glu = jnp.minimum(g, SWIGLU_LIMIT)
        lin = jnp.clip(l, -SWIGLU_LIMIT, SWIGLU_LIMIT)
        act = glu * jax.nn.sigmoid(SWIGLU_ALPHA * glu) * (lin + 1.0)
        ys_ref[...] = jnp.dot(act.astype(BF16), wd_s[...], preferred_element_type=F32) + bd_ref[...]

    @pl.when(jnp.logical_not(valid))
    def _():
        ys_ref[...] = jnp.zeros_like(ys_ref)


def _expert_call(block_expert, n_valid, xs, prm):
    n_rows, D = xs.shape
    n_blocks = n_rows // MOE_BLOCK
    half = PERM_GROUP // 2
    src = np.concatenate([2 * np.arange(half), 2 * np.arange(half) + 1])
    perm = jnp.asarray(np.arange(PERM_GROUP)[:, None] == src[None, :], BF16)
    rows = lambda i, be, nv: (jnp.minimum(i, nv[0] - 1), 0)
    wsel = lambda i, be, nv: (be[i], 0, 0)
    grid_spec = pltpu.PrefetchScalarGridSpec(
        num_scalar_prefetch=2, grid=(n_blocks,),
        in_specs=[pl.BlockSpec((MOE_BLOCK, D), rows),
                  pl.BlockSpec((None, D, 2 * D_FF), wsel),
                  pl.BlockSpec((None, 1, D_FF), wsel), pl.BlockSpec((None, 1, D_FF), wsel),
                  pl.BlockSpec((None, D_FF, D), wsel), pl.BlockSpec((None, 1, D), wsel),
                  pl.BlockSpec((PERM_GROUP, PERM_GROUP), lambda i, be, nv: (0, 0))],
        out_specs=pl.BlockSpec((MOE_BLOCK, D), lambda i, be, nv: (i, 0)),
        scratch_shapes=[pltpu.VMEM((D, D_FF), BF16), pltpu.VMEM((D, D_FF), BF16),
                        pltpu.VMEM((D_FF, D), BF16)])
    return pl.pallas_call(
        _expert_kernel, grid_spec=grid_spec,
        out_shape=jax.ShapeDtypeStruct((n_rows, D), F32),
        compiler_params=_cparams(("arbitrary",)), name="experts",
    )(block_expert, n_valid, xs, prm["w_gu"], prm["b_glu"], prm["b_lin"], prm["w_dn"], prm["b_dn"], perm)


def _combine_kernel(xmid_ref, gates_ref, fnorm_ref, y4_ref, out_ref):
    g = jnp.transpose(gates_ref[...])
    y = xmid_ref[...]
    for kk in range(TOP_K):
        y = y + y4_ref[kk] * g[:, kk:kk + 1]
    out_ref[...] = _rms(y, fnorm_ref[...])


def _combine_call(x_mid, gates, final_norm, y4, tc):
    T, D = x_mid.shape
    return pl.pallas_call(
        _combine_kernel, grid=(T // tc,),
        in_specs=[pl.BlockSpec((tc, D), lambda i: (i, 0)),
                  pl.BlockSpec((2 * TOP_K, tc), lambda i: (0, i)),
                  pl.BlockSpec(final_norm.shape, lambda i: (0, 0)),
                  pl.BlockSpec((TOP_K, tc, D), lambda i: (0, i, 0))],
        out_specs=pl.BlockSpec((tc, D), lambda i: (i, 0)),
        out_shape=jax.ShapeDtypeStruct((T, D), F32),
        compiler_params=_cparams(("parallel",)), name="combine",
    )(x_mid, gates, final_norm, y4)


def _prep_params(attn_norm, w_in, mla_q_norm, mla_w_uq, mla_kv_norm, mla_w_ukv, gla_w_gate_fwd,
                 gla_b_gate_fwd, gla_w_gate_bwd, gla_b_gate_bwd, gla_out_norm, w_out, ffn_norm,
                 router_w, router_b, w_gu, b_gu, w_dn, b_dn):
    D = D_MODEL
    o = np.cumsum((0, MLA_Q_LORA, MLA_KV_LORA, MLA_ROPE, GLA_KDIM, GLA_KDIM, GLA_WIDTH,
                   2 * GLA_GATE_RANK, GLA_WIDTH))
    seg = [w_in[:, o[n]:o[n + 1]] for n in range(8)]
    z = lambda n: jnp.zeros((D, n), w_in.dtype)
    w_in_p = jnp.concatenate(
        [seg[0], seg[1], z(MLA_NOPE), seg[2], z(LANES - MLA_NOPE - MLA_ROPE), seg[3], seg[4], seg[5],
         seg[6], z(LANES - 2 * GLA_GATE_RANK), seg[7]], axis=1).astype(BF16)

    wq = mla_w_uq.reshape(MLA_Q_LORA, MLA_HEADS, MLA_NOPE + MLA_ROPE)
    w_uq_p = jnp.concatenate(
        [wq[:, :, :MLA_NOPE].reshape(MLA_Q_LORA, -1),
         wq[:, :, MLA_NOPE:MLA_NOPE + HALF_ROPE].reshape(MLA_Q_LORA, -1),
         wq[:, :, MLA_NOPE + HALF_ROPE:].reshape(MLA_Q_LORA, -1)], axis=1)
    wkv = mla_w_ukv.reshape(MLA_KV_LORA, MLA_HEADS, MLA_NOPE + MLA_V)
    w_k = jnp.concatenate([wkv[:, :, :MLA_NOPE], jnp.zeros((MLA_KV_LORA, MLA_HEADS, QK_PAD - MLA_NOPE),
                                                            wkv.dtype)], axis=2)
    w_v = wkv[:, :, MLA_NOPE:].reshape(MLA_KV_LORA, -1)

    def gate_w(w, row0):
        full = jnp.zeros((LANES, GLA_KDIM), w.dtype)
        return full.at[row0:row0 + GLA_GATE_RANK].set(w).astype(BF16)

    E = N_EXPERTS
    return {
        "attn_norm": attn_norm.reshape(1, D), "w_in": w_in_p,
        "q_norm": mla_q_norm.reshape(1, -1), "w_uqT": w_uq_p.T.astype(BF16),
        "kv_norm": mla_kv_norm.reshape(1, -1),
        "w_k": w_k.reshape(MLA_KV_LORA, -1).astype(BF16), "w_vT": w_v.T.astype(BF16),
        "w_gf": gate_w(gla_w_gate_fwd, 0), "b_gf": gla_b_gate_fwd.reshape(1, -1),
        "w_gb": gate_w(gla_w_gate_bwd, GLA_GATE_RANK), "b_gb": gla_b_gate_bwd.reshape(1, -1),
        "gla_out_norm": gla_out_norm.reshape(1, -1),
        "w_out_a": w_out[:MLA_WIDTH].astype(BF16), "w_out_b": w_out[MLA_WIDTH:].astype(BF16),
        "ffn_norm": ffn_norm.reshape(1, D),
        "router_wT": router_w.T, "router_b": router_b.reshape(E, 1),
        "w_gu": w_gu, "w_dn": w_dn,
        "b_glu": b_gu[:, 0::2].reshape(E, 1, D_FF), "b_lin": b_gu[:, 1::2].reshape(E, 1, D_FF),
        "b_dn": b_dn.reshape(E, 1, D),
    }


def _rope_tables(S):
    inv_freq = jnp.power(ROPE_THETA, -jnp.arange(0, MLA_ROPE, 2, dtype=F32) / MLA_ROPE)
    ang = jnp.arange(S, dtype=F32)[:, None] * inv_freq[None, :]
    cos, sin = jnp.cos(ang), jnp.sin(ang)
    z = lambda n: jnp.zeros((S, n), F32)
    tail = LANES - MLA_NOPE - MLA_ROPE
    return {
        "c": jnp.concatenate([z(MLA_NOPE), cos, cos, z(tail)], axis=1),
        "s1": jnp.concatenate([z(MLA_NOPE), -sin, z(HALF_ROPE), z(tail)], axis=1),
        "s2": jnp.concatenate([z(MLA_NOPE), z(HALF_ROPE), sin, z(tail)], axis=1),
        "cosT": jnp.tile(cos.T, (MLA_HEADS, 1)), "sinT": jnp.tile(sin.T, (MLA_HEADS, 1)),
    }


def _encoder(xa, xb, prm, final_norm):
    n_first, S, D = xa.shape
    B = n_first + xb.shape[0]
    rope = _rope_tables(S)
    qT, k, vT, gq, gk, gv, laf, lab, gr = _pre_call(xa, xb, prm, rope, ATTN_KEY_TILE)
    oT = _attn_call(qT, k, vT, min(ATTN_QUERY_TILE, S)).reshape(B, MLA_WIDTH, S)
    o_f, o_b = _gla_call(gq, gk, gv, laf, lab)
    mixed = (oT, o_f, o_b, gr)
    return (_moe_block(xa, 0, mixed, prm, final_norm), _moe_block(xb, n_first, mixed, prm, final_norm))


def _moe_block(x, batch0, mixed, prm, final_norm):
    B, S, D = x.shape
    x_mid, h2, meta, gates, counts = _post_call(x, batch0, *mixed, prm, min(256, S))

    T = B * S
    n_rows = T * TOP_K + N_EXPERTS * MOE_BLOCK
    n_blocks = n_rows // MOE_BLOCK
    cnt = counts[:, 0].astype(jnp.int32)
    padded = ((cnt + MOE_BLOCK - 1) // MOE_BLOCK) * MOE_BLOCK
    pends = jnp.cumsum(padded)
    starts = (pends - padded).astype(jnp.int32)
    n_valid = (pends[-1] // MOE_BLOCK).astype(jnp.int32).reshape(1)
    blk = jnp.minimum(jnp.arange(n_blocks, dtype=jnp.int32), n_valid[0] - 1) * MOE_BLOCK
    block_expert = jnp.minimum(jnp.sum(pends[None, :] <= blk[:, None], axis=1), N_EXPERTS - 1).astype(jnp.int32)

    experts = jnp.arange(N_EXPERTS, dtype=jnp.int32)[:, None, None]
    slots = jnp.sum(jnp.where(meta[None, :TOP_K] == experts, starts[:, None, None], 0), axis=0) + meta[TOP_K:]
    slots = jnp.pad(slots.reshape(TOP_K, T // SC_WINDOW, SC_WINDOW),
                    ((0, 0), (0, 0), (0, SC_INDEX_PAD - SC_WINDOW)))

    xs = _sc_scatter_rows(h2.reshape(T, D), slots, n_rows)
    ys = _expert_call(block_expert, n_valid, xs, prm)
    y4 = _sc_gather_rows(ys, slots, T)
    y = _combine_call(x_mid.reshape(T, D), gates, final_norm.reshape(1, D), y4, min(256, S))
    return y.reshape(x.shape)


def kernel(x_prompt, x_sample, attn_norm, w_in, mla_q_norm, mla_w_uq, mla_kv_norm, mla_w_ukv,
           gla_w_gate_fwd, gla_b_gate_fwd, gla_w_gate_bwd, gla_b_gate_bwd, gla_out_norm, w_out, ffn_norm,
           router_w, router_b, expert_w_gate_up, expert_b_gate_up, expert_w_down, expert_b_down,
           final_norm):
    layer = (attn_norm, w_in, mla_q_norm, mla_w_uq, mla_kv_norm, mla_w_ukv, gla_w_gate_fwd,
             gla_b_gate_fwd, gla_w_gate_bwd, gla_b_gate_bwd, gla_out_norm, w_out, ffn_norm, router_w,
             router_b, expert_w_gate_up, expert_b_gate_up, expert_w_down, expert_b_down)
    assert all(p.shape[0] == 1 for p in layer), "single layer expected"
    assert x_prompt.shape[1:] == x_sample.shape[1:]
    prm = _prep_params(*[p[0] for p in layer])
    return _encoder(x_prompt, x_sample, prm, final_norm)
```

```python
import functools

import jax
import jax.numpy as jnp
import numpy as np
from jax import lax
from jax.experimental import pallas as pl
from jax.experimental.pallas import tpu as pltpu
from jax.experimental.pallas import tpu_sc as plsc

F32 = jnp.float32
BF16 = jnp.bfloat16

D_MODEL = 1024
MLA_HEADS = 8
MLA_NOPE = 64
MLA_ROPE = 32
MLA_V = 64
MLA_Q_LORA = 384
MLA_KV_LORA = 256
ROPE_THETA = 10000.0
GLA_HEADS = 4
GLA_DK = 64
GLA_DV = 128
GLA_GATE_RANK = 16
GLA_GATE_NORM = 16.0
N_EXPERTS = 32
TOP_K = 4
D_FF = 1024
SWIGLU_LIMIT = 7.0
SWIGLU_ALPHA = 1.702
MOE_BLOCK = 512
RMS_EPS = 1e-6

MLA_WIDTH = MLA_HEADS * MLA_V
GLA_WIDTH = GLA_HEADS * GLA_DV
GLA_KDIM = GLA_HEADS * GLA_DK
HALF_ROPE = MLA_ROPE // 2
QK_PAD = 128
LANES = 128
V_ONES_ROWS = 16

OFF_CQ = 0
OFF_CKV = OFF_CQ + MLA_Q_LORA
OFF_KR = OFF_CKV + MLA_KV_LORA
OFF_GQ = OFF_KR + LANES
OFF_GK = OFF_GQ + GLA_KDIM
OFF_GV = OFF_GK + GLA_KDIM
OFF_LR = OFF_GV + GLA_WIDTH
OFF_GR = OFF_LR + LANES
PROJ_COLS = OFF_GR + GLA_WIDTH

TOKEN_TILE = 512
ATTN_KEY_TILE = 256
ATTN_QUERY_TILE = 512
ATTN_SCORE_SLOTS = 4
GLA_TILE = 256
GLA_LEVELS = 8
SC_WINDOW = 32
SC_INDEX_PAD = 128
PERM_GROUP = 256
VMEM_LIMIT = 56 * 1024 * 1024

LOG2_E = 1.4426950408889634
NT_DIMS = (((1,), (1,)), ((), ()))
TN_DIMS = (((0,), (0,)), ((), ()))


def _cparams(sem):
    return pltpu.CompilerParams(dimension_semantics=sem, vmem_limit_bytes=VMEM_LIMIT)


def _rms(x, gain):
    return x * lax.rsqrt(jnp.mean(x * x, axis=-1, keepdims=True) + RMS_EPS) * gain


def _split_bf16(x):
    hi = x.astype(BF16)
    lo = (x - hi.astype(F32)).astype(BF16)
    return hi, lo


def _pre_kernel(xa_ref, xb_ref, an_ref, win_ref, qn_ref, wuqT_ref, kvn_ref, wk_ref, wvT_ref,
                wgf_ref, bgf_ref, wgb_ref, bgb_ref, rc_ref, rs1_ref, rs2_ref, cosT_ref, sinT_ref,
                qT_ref, k_ref, vT_ref, gq_ref, gk_ref, gv_ref, laf_ref, lab_ref, gr_ref, *, n_first):
    x = jnp.where(pl.program_id(0) < n_first, xa_ref[...], xb_ref[...])
    h = _rms(x, an_ref[...]).astype(BF16)
    proj = jnp.dot(h, win_ref[...], preferred_element_type=F32)

    cqn = _rms(proj[:, OFF_CQ:OFF_CQ + MLA_Q_LORA], qn_ref[...]).astype(BF16)
    ckvn = _rms(proj[:, OFF_CKV:OFF_CKV + MLA_KV_LORA], kvn_ref[...]).astype(BF16)

    scale = (MLA_NOPE + MLA_ROPE) ** -0.5 * LOG2_E
    qT = lax.dot_general(wuqT_ref[...], cqn, NT_DIMS, preferred_element_type=F32) * scale
    n_nope = MLA_HEADS * MLA_NOPE
    n_half = MLA_HEADS * HALF_ROPE
    x1 = qT[n_nope:n_nope + n_half]
    x2 = qT[n_nope + n_half:n_nope + 2 * n_half]
    c = cosT_ref[...]
    s = sinT_ref[...]
    x1r = x1 * c - x2 * s
    x2r = x1 * s + x2 * c
    zpad = jnp.zeros((QK_PAD - MLA_NOPE - MLA_ROPE, qT.shape[1]), BF16)
    for hd in range(MLA_HEADS):
        qT_ref[hd, 0:MLA_NOPE, :] = qT[hd * MLA_NOPE:(hd + 1) * MLA_NOPE].astype(BF16)
        qT_ref[hd, MLA_NOPE:MLA_NOPE + HALF_ROPE, :] = x1r[hd * HALF_ROPE:(hd + 1) * HALF_ROPE].astype(BF16)
        qT_ref[hd, MLA_NOPE + HALF_ROPE:MLA_NOPE + MLA_ROPE, :] = (
            x2r[hd * HALF_ROPE:(hd + 1) * HALF_ROPE].astype(BF16))
        qT_ref[hd, MLA_NOPE + MLA_ROPE:QK_PAD, :] = zpad

    kr = proj[:, OFF_KR:OFF_KR + LANES]
    kr = (kr * rc_ref[...]
          + pltpu.roll(kr, LANES - HALF_ROPE, axis=1) * rs1_ref[...]
          + pltpu.roll(kr, HALF_ROPE, axis=1) * rs2_ref[...])
    kfull = jnp.dot(ckvn, wk_ref[...], preferred_element_type=F32)
    for hd in range(MLA_HEADS):
        k_ref[hd] = (kfull[:, hd * QK_PAD:(hd + 1) * QK_PAD] + kr).astype(BF16)

    vT = lax.dot_general(wvT_ref[...], ckvn, NT_DIMS, preferred_element_type=F32)
    for hd in range(MLA_HEADS):
        vT_ref[hd] = vT[hd * MLA_V:(hd + 1) * MLA_V].astype(BF16)

    gq_ref[...] = proj[:, OFF_GQ:OFF_GQ + GLA_KDIM] * (GLA_DK ** -0.5)
    gk_ref[...] = proj[:, OFF_GK:OFF_GK + GLA_KDIM]
    gv_ref[...] = proj[:, OFF_GV:OFF_GV + GLA_WIDTH]
    gr_ref[...] = proj[:, OFF_GR:OFF_GR + GLA_WIDTH]
    lr = proj[:, OFF_LR:OFF_LR + LANES].astype(BF16)

    def log_decay(w_ref, b_ref):
        z = jnp.dot(lr, w_ref[...], preferred_element_type=F32) + b_ref[...]
        return (jnp.minimum(z, 0.0) - jnp.log1p(jnp.exp(-jnp.abs(z)))) * (1.0 / GLA_GATE_NORM)

    laf_ref[...] = log_decay(wgf_ref, bgf_ref)
    lab_ref[...] = log_decay(wgb_ref, bgb_ref)


def _two_source_specs(n_first, tm, width):
    return [pl.BlockSpec((None, tm, width), lambda b, i: (jnp.minimum(b, n_first - 1), i, 0)),
            pl.BlockSpec((None, tm, width), lambda b, i: (jnp.maximum(b - n_first, 0), i, 0))]


def _pre_call(xa, xb, prm, rope, tm):
    n_first, S, D = xa.shape
    B = n_first + xb.shape[0]
    nS = S // tm
    H = MLA_HEADS

    def full(a):
        nd = a.ndim
        return pl.BlockSpec(a.shape, lambda b, i, _nd=nd: (0,) * _nd)

    tok = lambda w: pl.BlockSpec((None, tm, w), lambda b, i: (b, i, 0))
    in_specs = [
        *_two_source_specs(n_first, tm, D), full(prm["attn_norm"]), full(prm["w_in"]), full(prm["q_norm"]), full(prm["w_uqT"]),
        full(prm["kv_norm"]), full(prm["w_k"]), full(prm["w_vT"]),
        full(prm["w_gf"]), full(prm["b_gf"]), full(prm["w_gb"]), full(prm["b_gb"]),
        pl.BlockSpec((tm, LANES), lambda b, i: (i, 0)),
        pl.BlockSpec((tm, LANES), lambda b, i: (i, 0)),
        pl.BlockSpec((tm, LANES), lambda b, i: (i, 0)),
        pl.BlockSpec((H * HALF_ROPE, tm), lambda b, i: (0, i)),
        pl.BlockSpec((H * HALF_ROPE, tm), lambda b, i: (0, i)),
    ]
    out_shape = [
        jax.ShapeDtypeStruct((B, H, QK_PAD, S), BF16),
        jax.ShapeDtypeStruct((B, H, S, QK_PAD), BF16),
        jax.ShapeDtypeStruct((B, H, nS, MLA_V, tm), BF16),
        jax.ShapeDtypeStruct((B, S, GLA_KDIM), F32),
        jax.ShapeDtypeStruct((B, S, GLA_KDIM), F32),
        jax.ShapeDtypeStruct((B, S, GLA_WIDTH), F32),
        jax.ShapeDtypeStruct((B, S, GLA_KDIM), F32),
        jax.ShapeDtypeStruct((B, S, GLA_KDIM), F32),
        jax.ShapeDtypeStruct((B, S, GLA_WIDTH), F32),
    ]
    out_specs = [
        pl.BlockSpec((None, H, QK_PAD, tm), lambda b, i: (b, 0, 0, i)),
        pl.BlockSpec((None, H, tm, QK_PAD), lambda b, i: (b, 0, i, 0)),
        pl.BlockSpec((None, H, None, MLA_V, tm), lambda b, i: (b, 0, i, 0, 0)),
        tok(GLA_KDIM), tok(GLA_KDIM), tok(GLA_WIDTH), tok(GLA_KDIM), tok(GLA_KDIM), tok(GLA_WIDTH),
    ]
    return pl.pallas_call(
        functools.partial(_pre_kernel, n_first=n_first), grid=(B, nS), in_specs=in_specs, out_specs=out_specs, out_shape=out_shape,
        compiler_params=_cparams(("parallel", "parallel")), name="pre",
    )(xa, xb, prm["attn_norm"], prm["w_in"], prm["q_norm"], prm["w_uqT"], prm["kv_norm"], prm["w_k"],
      prm["w_vT"], prm["w_gf"], prm["b_gf"], prm["w_gb"], prm["b_gb"],
      rope["c"], rope["s1"], rope["s2"], rope["cosT"], rope["sinT"])


def _attn_kernel(qT_ref, k_ref, vT_ref, o_ref, s_ref, *, n_kblk):
    qT = qT_ref[...]
    tq = qT.shape[1]
    tk = k_ref.shape[0] // n_kblk

    def scores(j):
        sT = jnp.dot(k_ref[j * tk:(j + 1) * tk, :], qT, preferred_element_type=F32)
        s_ref[j % ATTN_SCORE_SLOTS] = sT
        return jnp.max(sT, axis=0, keepdims=True)

    def values(j, p):
        per_tile = vT_ref.shape[2] // tk
        v_blk = vT_ref[j // per_tile, :, (j % per_tile) * tk:(j % per_tile + 1) * tk]
        vext = jnp.concatenate([v_blk, ones], axis=0)
        return jnp.dot(vext, p, preferred_element_type=F32)

    def softmax(j, blk_max, m):
        m_new = jnp.maximum(m, blk_max)
        p = jnp.exp2(s_ref[j % ATTN_SCORE_SLOTS] - m_new).astype(BF16)
        return m_new, jnp.exp2(m - m_new), p

    ones = jnp.ones((V_ONES_ROWS, tk), BF16)
    m = jnp.full((1, tq), -jnp.inf, F32)
    acc = jnp.zeros((MLA_V + V_ONES_ROWS, tq), F32)
    blk_max = {0: scores(0), 1: scores(1)}
    for j in range(n_kblk):
        if j + 2 < n_kblk:
            blk_max[j + 2] = scores(j + 2)
        m, alpha, p = softmax(j, blk_max.pop(j), m)
        acc = alpha * acc + values(j, p)
    o_ref[...] = (acc[0:MLA_V] / acc[MLA_V:MLA_V + 1]).astype(o_ref.dtype)


def _attn_call(qT, k, vT, tq, tk):
    B, H, _, S = qT.shape
    n_vblk, v_tile = vT.shape[2], vT.shape[4]
    n_kblk = S // tk
    assert n_kblk >= 2 and S % tq == 0 and v_tile % tk == 0
    return pl.pallas_call(
        functools.partial(_attn_kernel, n_kblk=n_kblk),
        grid=(B, H, S // tq),
        in_specs=[
            pl.BlockSpec((None, None, QK_PAD, tq), lambda b, h, i: (b, h, 0, i)),
            pl.BlockSpec((None, None, S, QK_PAD), lambda b, h, i: (b, h, 0, 0)),
            pl.BlockSpec((None, None, n_vblk, MLA_V, v_tile), lambda b, h, i: (b, h, 0, 0, 0)),
        ],
        out_specs=pl.BlockSpec((None, None, MLA_V, tq), lambda b, h, i: (b, h, 0, i)),
        out_shape=jax.ShapeDtypeStruct((B, H, MLA_V, S), BF16),
        scratch_shapes=[pltpu.VMEM((ATTN_SCORE_SLOTS, tk, tq), F32)],
        compiler_params=_cparams(("parallel", "parallel", "parallel")), name="attn",
    )(qT, k, vT)


def _gla_consts():
    L = GLA_TILE
    i = np.arange(L)[:, None]
    j = np.arange(L)[None, :]
    x = i ^ j
    lidx = np.where(x > 0, np.floor(np.log2(np.maximum(x, 1))), -1).astype(np.int32)
    lidx_f = np.where(i > j, lidx, -1).astype(np.int32)
    lidx_b = np.where(i < j, lidx, -1).astype(np.int32)
    hd = np.arange(GLA_KDIM)[:, None] // GLA_DK
    hv = np.arange(GLA_WIDTH)[None, :] // GLA_DV
    bexp = (hd == hv).astype(np.float32)
    return jnp.asarray(lidx_f), jnp.asarray(lidx_b), jnp.asarray(bexp, BF16)


def _gla_direction(q, k, v, la, lidx, bexp_ref, ssum_ref, state_ref, o_ref, forward):
    L = GLA_TILE
    la_hi, la_lo = _split_bf16(la)
    vb = v.astype(BF16)
    lane_head = lax.broadcasted_iota(jnp.int32, (1, GLA_KDIM), 1) // GLA_DK
    head_masks = [(lane_head == h).astype(BF16) for h in range(GLA_HEADS)]
    row = lax.broadcasted_iota(jnp.int32, la.shape, 0)

    def widen(c, t, m):
        upper_half = (row & m) != 0
        sibling_total = jnp.where(upper_half, pltpu.roll(t, m, axis=0), pltpu.roll(t, L - m, axis=0))
        return c + jnp.where(upper_half, sibling_total, 0.0), t + sibling_total

    ssum_ref[...] = jnp.zeros_like(ssum_ref)
    c, t = la, la
    for lvl in range(GLA_LEVELS):
        if lvl > 0:
            c, t = widen(c, t, 1 << (lvl - 1))
        if forward:
            eq, ek = c, t - c
        else:
            eq, ek = t - c + la, c - la
        ql = (q * jnp.exp(eq)).astype(BF16)
        kl = (k * jnp.exp(ek)).astype(BF16)
        sel = lidx == lvl
        for h in range(GLA_HEADS):
            sc = lax.dot_general(ql * head_masks[h], kl, NT_DIMS, preferred_element_type=F32)
            ssum_ref[h] = jnp.where(sel, sc, ssum_ref[h])

    c, t = widen(c, t, L // 2)
    if forward:
        eq, ek = c, t - c
    else:
        eq, ek = t - c + la, c - la
    q_in = (q * jnp.exp(eq)).astype(BF16)
    k_out = (k * jnp.exp(ek)).astype(BF16)
    o = jnp.dot(q_in, state_ref[...].astype(BF16), preferred_element_type=F32)
    if forward:
        o = o + jnp.dot((q * k).astype(BF16), bexp_ref[...], preferred_element_type=F32) * v
    for h in range(GLA_HEADS):
        oh = jnp.dot(ssum_ref[h].astype(BF16), vb[:, h * GLA_DV:(h + 1) * GLA_DV],
                     preferred_element_type=F32)
        o_ref[:, h * GLA_DV:(h + 1) * GLA_DV] = o[:, h * GLA_DV:(h + 1) * GLA_DV] + oh

    ones = jnp.ones((L, LANES), BF16)
    tot_col = (lax.dot_general(la_hi, ones, TN_DIMS, preferred_element_type=F32)
               + lax.dot_general(la_lo, ones, TN_DIMS, preferred_element_type=F32))
    dec = jnp.exp(tot_col)
    upd = lax.dot_general(k_out, vb, TN_DIMS, preferred_element_type=F32)
    for h in range(GLA_HEADS):
        cols = slice(h * GLA_DV, (h + 1) * GLA_DV)
        state_ref[:, cols] = dec * state_ref[:, cols] + upd[:, cols] * bexp_ref[:, cols].astype(F32)


def _gla_kernel(qf_ref, kf_ref, vf_ref, laf_ref, qb_ref, kb_ref, vb_ref, lab_ref,
                lidxf_ref, lidxb_ref, bexp_ref, of_ref, ob_ref, sf_ref, sb_ref, ssum_ref):
    @pl.when(pl.program_id(1) == 0)
    def _():
        sf_ref[...] = jnp.zeros_like(sf_ref)
        sb_ref[...] = jnp.zeros_like(sb_ref)

    _gla_direction(qf_ref[...], kf_ref[...], vf_ref[...], laf_ref[...], lidxf_ref[...],
                   bexp_ref, ssum_ref, sf_ref, of_ref, True)
    _gla_direction(qb_ref[...], kb_ref[...], vb_ref[...], lab_ref[...], lidxb_ref[...],
                   bexp_ref, ssum_ref, sb_ref, ob_ref, False)


def _gla_call(gq, gk, gv, laf, lab):
    B, S, _ = gq.shape
    L = GLA_TILE
    n = S // L
    lidx_f, lidx_b, bexp = _gla_consts()
    fwd = lambda w: pl.BlockSpec((None, L, w), lambda b, i: (b, i, 0))
    bwd = lambda w: pl.BlockSpec((None, L, w), lambda b, i: (b, n - 1 - i, 0))
    const = lambda a: pl.BlockSpec(a.shape, lambda b, i, _nd=a.ndim: (0,) * _nd)
    return pl.pallas_call(
        _gla_kernel, grid=(B, n),
        in_specs=[fwd(GLA_KDIM), fwd(GLA_KDIM), fwd(GLA_WIDTH), fwd(GLA_KDIM),
                  bwd(GLA_KDIM), bwd(GLA_KDIM), bwd(GLA_WIDTH), bwd(GLA_KDIM),
                  const(lidx_f), const(lidx_b), const(bexp)],
        out_specs=[fwd(GLA_WIDTH), bwd(GLA_WIDTH)],
        out_shape=[jax.ShapeDtypeStruct((B, S, GLA_WIDTH), F32)] * 2,
        scratch_shapes=[pltpu.VMEM((GLA_KDIM, GLA_WIDTH), F32), pltpu.VMEM((GLA_KDIM, GLA_WIDTH), F32),
                        pltpu.VMEM((GLA_HEADS, L, L), F32)],
        compiler_params=_cparams(("parallel", "arbitrary")), name="gla",
    )(gq, gk, gv, laf, gq, gk, gv, lab, lidx_f, lidx_b, bexp)


def _post_kernel(x_ref, oT_ref, of_ref, ob_ref, gr_ref, gon_ref, woa_ref, wob_ref, fn_ref,
                 rwT_ref, rb_ref, upper_ref,
                 xmid_ref, h2_ref, meta_ref, gates_ref, counts_ref, carry_ref):
    first = (pl.program_id(0) == 0) & (pl.program_id(1) == 0)

    @pl.when(first)
    def _():
        carry_ref[...] = jnp.zeros_like(carry_ref)

    o = of_ref[...] + ob_ref[...]
    gr = gr_ref[...]
    parts = []
    for h in range(GLA_HEADS):
        cols = slice(h * GLA_DV, (h + 1) * GLA_DV)
        parts.append(_rms(o[:, cols], gon_ref[...]) * jax.nn.silu(gr[:, cols]))
    gla = jnp.concatenate(parts, axis=1).astype(BF16)

    x_mid = (x_ref[...]
             + lax.dot_general(oT_ref[...], woa_ref[...], TN_DIMS, preferred_element_type=F32)
             + jnp.dot(gla, wob_ref[...], preferred_element_type=F32))
    xmid_ref[...] = x_mid
    h2 = _rms(x_mid, fn_ref[...])
    h2_ref[...] = h2

    h_hi, h_lo = _split_bf16(h2)
    w_hi, w_lo = _split_bf16(rwT_ref[...])
    logits = (lax.dot_general(w_hi, h_hi, NT_DIMS, preferred_element_type=F32)
              + lax.dot_general(w_hi, h_lo, NT_DIMS, preferred_element_type=F32)
              + lax.dot_general(w_lo, h_hi, NT_DIMS, preferred_element_type=F32)
              + rb_ref[...])
    tm = logits.shape[1]
    eidx = lax.broadcasted_iota(jnp.int32, (N_EXPERTS, tm), 0).astype(F32)
    vals, idxs, sels = [], [], []
    cur = logits
    for _ in range(TOP_K):
        mk = jnp.max(cur, axis=0, keepdims=True)
        ik = jnp.min(jnp.where(cur == mk, eidx, float(N_EXPERTS)), axis=0, keepdims=True)
        sel = eidx == ik
        vals.append(mk)
        idxs.append(ik)
        sels.append(sel)
        cur = jnp.where(sel, -jnp.inf, cur)
    exps = [jnp.exp(vk - vals[0]) for vk in vals]
    denom = exps[0] + exps[1] + exps[2] + exps[3]
    gates = [e / denom for e in exps]

    cnt = (sels[0] | sels[1] | sels[2] | sels[3])
    before = jnp.dot(cnt.astype(BF16), upper_ref[...], preferred_element_type=F32) + carry_ref[:, 0:1]
    ranks = [jnp.sum(jnp.where(sel, before, 0.0), axis=0, keepdims=True) for sel in sels]
    carry_ref[...] = carry_ref[...] + jnp.sum(cnt.astype(F32), axis=1, keepdims=True)
    counts_ref[...] = carry_ref[...]

    meta_ref[...] = jnp.concatenate(idxs + ranks, axis=0).astype(jnp.int32)
    gates_ref[...] = jnp.concatenate(gates + [jnp.zeros((TOP_K, tm), F32)], axis=0)


def _post_call(x, batch0, oT, o_f, o_b, gr, prm, tm):
    B, S, D = x.shape
    nS = S // tm
    upper = jnp.asarray(np.triu(np.ones((tm, tm), np.float32), k=1), BF16)
    full = lambda a: pl.BlockSpec(a.shape, lambda b, i, _nd=a.ndim: (0,) * _nd)
    own = lambda w: pl.BlockSpec((None, tm, w), lambda b, i: (b, i, 0))
    tok = lambda w: pl.BlockSpec((None, tm, w), lambda b, i: (batch0 + b, i, 0))
    colblk = lambda r: pl.BlockSpec((None, r, tm), lambda b, i: (batch0 + b, 0, i))
    flat = lambda r: pl.BlockSpec((r, tm), lambda b, i: (0, b * nS + i))
    return pl.pallas_call(
        _post_kernel, grid=(B, nS),
        in_specs=[own(D), colblk(MLA_WIDTH), tok(GLA_WIDTH), tok(GLA_WIDTH), tok(GLA_WIDTH),
                  full(prm["gla_out_norm"]), full(prm["w_out_a"]), full(prm["w_out_b"]),
                  full(prm["ffn_norm"]), full(prm["router_wT"]), full(prm["router_b"]), full(upper)],
        out_specs=[own(D), own(D), flat(2 * TOP_K), flat(2 * TOP_K),
                   pl.BlockSpec((N_EXPERTS, LANES), lambda b, i: (0, 0))],
        out_shape=[jax.ShapeDtypeStruct((B, S, D), F32), jax.ShapeDtypeStruct((B, S, D), F32),
                   jax.ShapeDtypeStruct((2 * TOP_K, B * S), jnp.int32),
                   jax.ShapeDtypeStruct((2 * TOP_K, B * S), F32),
                   jax.ShapeDtypeStruct((N_EXPERTS, LANES), F32)],
        scratch_shapes=[pltpu.VMEM((N_EXPERTS, LANES), F32)],
        compiler_params=_cparams(("arbitrary", "arbitrary")), name="post",
    )(x, oT, o_f, o_b, gr, prm["gla_out_norm"], prm["w_out_a"], prm["w_out_b"], prm["ffn_norm"],
      prm["router_wT"], prm["router_b"], upper)


def _sc_mesh():
    return plsc.VectorSubcoreMesh(core_axis_name="core", subcore_axis_name="subcore")


def _sc_scatter_rows(x, slots, n_rows):
    T, D = x.shape
    K = slots.shape[0]

    @pl.kernel(out_type=jax.ShapeDtypeStruct((n_rows, D), x.dtype), mesh=_sc_mesh(), scratch_types=[])
    def scatter(x_hbm, slots_hbm, out_hbm):
        def window(x_vmem, slots_vmem):
            for kk in range(K):
                pltpu.sync_copy(x_vmem, out_hbm.at[slots_vmem.at[kk, 0, pl.ds(0, SC_WINDOW)]])

        pltpu.emit_pipeline(
            window, grid=(T // SC_WINDOW,),
            in_specs=[pl.BlockSpec((SC_WINDOW, D), lambda i: (i, 0)),
                      pl.BlockSpec((K, 1, SC_INDEX_PAD), lambda i: (0, i, 0))],
            out_specs=[], core_axis_name=("core", "subcore"),
            dimension_semantics=(pltpu.PARALLEL,))(x_hbm, slots_hbm)

    return scatter(x, slots)


def _sc_gather_rows(y, slots, T):
    D = y.shape[1]
    K = slots.shape[0]

    @pl.kernel(out_type=jax.ShapeDtypeStruct((K, T, D), y.dtype), mesh=_sc_mesh(), scratch_types=[])
    def gather(y_hbm, slots_hbm, out_hbm):
        def window(slots_vmem, out_vmem):
            pltpu.sync_copy(y_hbm.at[slots_vmem.at[0, 0, pl.ds(0, SC_WINDOW)]], out_vmem.at[0])

        pltpu.emit_pipeline(
            window, grid=(K, T // SC_WINDOW),
            in_specs=[pl.BlockSpec((1, 1, SC_INDEX_PAD), lambda kk, i: (kk, i, 0))],
            out_specs=[pl.BlockSpec((1, SC_WINDOW, D), lambda kk, i: (kk, i, 0))],
            core_axis_name=("core", "subcore"),
            dimension_semantics=(pltpu.PARALLEL, pltpu.PARALLEL))(slots_hbm, out_hbm)

    return gather(y, slots)


def _expert_weights_kernel(wgu_ref, wdn_ref, perm_ref, wg_ref, wl_ref, wd_ref):
    perm = perm_ref[...]
    half = PERM_GROUP // 2
    for c in range(2 * D_FF // PERM_GROUP):
        w = wgu_ref[:, c * PERM_GROUP:(c + 1) * PERM_GROUP].astype(BF16)
        sep = jnp.dot(w, perm, preferred_element_type=F32).astype(BF16)
        wg_ref[:, c * half:(c + 1) * half] = sep[:, :half]
        wl_ref[:, c * half:(c + 1) * half] = sep[:, half:]
    wd_ref[...] = wdn_ref[...].astype(BF16)


def _expert_weights_call(w_gu, w_dn):
    E, D, _ = w_gu.shape
    half = PERM_GROUP // 2
    src = np.concatenate([2 * np.arange(half), 2 * np.arange(half) + 1])
    perm = jnp.asarray(np.arange(PERM_GROUP)[:, None] == src[None, :], BF16)
    per_expert = lambda r, c: pl.BlockSpec((None, r, c), lambda e: (e, 0, 0))
    return pl.pallas_call(
        _expert_weights_kernel, grid=(E,),
        in_specs=[per_expert(D, 2 * D_FF), per_expert(D_FF, D),
                  pl.BlockSpec((PERM_GROUP, PERM_GROUP), lambda e: (0, 0))],
        out_specs=[per_expert(D, D_FF), per_expert(D, D_FF), per_expert(D_FF, D)],
        out_shape=[jax.ShapeDtypeStruct((E, D, D_FF), BF16), jax.ShapeDtypeStruct((E, D, D_FF), BF16),
                   jax.ShapeDtypeStruct((E, D_FF, D), BF16)],
        compiler_params=_cparams(("parallel",)), name="expert_weights",
    )(w_gu, w_dn, perm)


def _expert_kernel(be_ref, nv_ref, xs_ref, wg_ref, wl_ref, bg_ref, bl_ref, wd_ref, bd_ref, ys_ref):
    del be_ref
    valid = pl.program_id(0) < nv_ref[0]

    @pl.when(valid)
    def _():
        xb = xs_ref[...].astype(BF16)
        g = jnp.dot(xb, wg_ref[...], preferred_element_type=F32) + bg_ref[...]
        l = jnp.dot(xb, wl_ref[...], preferred_element_type=F32) + bl_ref[...]
        glu = jnp.minimum(g, SWIGLU_LIMIT)
        lin = jnp.clip(l, -SWIGLU_LIMIT, SWIGLU_LIMIT)
        act = glu * jax.nn.sigmoid(SWIGLU_ALPHA * glu) * (lin + 1.0)
        ys_ref[...] = jnp.dot(act.astype(BF16), wd_ref[...], preferred_element_type=F32) + bd_ref[...]

    @pl.when(jnp.logical_not(valid))
    def _():
        ys_ref[...] = jnp.zeros_like(ys_ref)


def _expert_call(block_expert, n_valid, xs, prm):
    n_rows, D = xs.shape
    n_blocks = n_rows // MOE_BLOCK
    rows = lambda i, be, nv: (jnp.minimum(i, nv[0] - 1), 0)
    wsel = lambda i, be, nv: (be[i], 0, 0)
    grid_spec = pltpu.PrefetchScalarGridSpec(
        num_scalar_prefetch=2, grid=(n_blocks,),
        in_specs=[pl.BlockSpec((MOE_BLOCK, D), rows),
                  pl.BlockSpec((None, D, D_FF), wsel), pl.BlockSpec((None, D, D_FF), wsel),
                  pl.BlockSpec((None, 1, D_FF), wsel), pl.BlockSpec((None, 1, D_FF), wsel),
                  pl.BlockSpec((None, D_FF, D), wsel), pl.BlockSpec((None, 1, D), wsel)],
        out_specs=pl.BlockSpec((MOE_BLOCK, D), lambda i, be, nv: (i, 0)))
    return pl.pallas_call(
        _expert_kernel, grid_spec=grid_spec,
        out_shape=jax.ShapeDtypeStruct((n_rows, D), F32),
        compiler_params=_cparams(("arbitrary",)), name="experts",
    )(block_expert, n_valid, xs, prm["w_glu"], prm["w_lin"], prm["b_glu"], prm["b_lin"],
      prm["w_dn"], prm["b_dn"])


def _combine_kernel(xmid_ref, gates_ref, fnorm_ref, y4_ref, out_ref):
    g = jnp.transpose(gates_ref[...])
    y = xmid_ref[...]
    for kk in range(TOP_K):
        y = y + y4_ref[kk] * g[:, kk:kk + 1]
    out_ref[...] = _rms(y, fnorm_ref[...])


def _combine_call(x_mid, gates, final_norm, y4, tc):
    T, D = x_mid.shape
    return pl.pallas_call(
        _combine_kernel, grid=(T // tc,),
        in_specs=[pl.BlockSpec((tc, D), lambda i: (i, 0)),
                  pl.BlockSpec((2 * TOP_K, tc), lambda i: (0, i)),
                  pl.BlockSpec(final_norm.shape, lambda i: (0, 0)),
                  pl.BlockSpec((TOP_K, tc, D), lambda i: (0, i, 0))],
        out_specs=pl.BlockSpec((tc, D), lambda i: (i, 0)),
        out_shape=jax.ShapeDtypeStruct((T, D), F32),
        compiler_params=_cparams(("parallel",)), name="combine",
    )(x_mid, gates, final_norm, y4)


def _prep_params(attn_norm, w_in, mla_q_norm, mla_w_uq, mla_kv_norm, mla_w_ukv, gla_w_gate_fwd,
                 gla_b_gate_fwd, gla_w_gate_bwd, gla_b_gate_bwd, gla_out_norm, w_out, ffn_norm,
                 router_w, router_b, w_gu, b_gu, w_dn, b_dn):
    D = D_MODEL
    o = np.cumsum((0, MLA_Q_LORA, MLA_KV_LORA, MLA_ROPE, GLA_KDIM, GLA_KDIM, GLA_WIDTH,
                   2 * GLA_GATE_RANK, GLA_WIDTH))
    seg = [w_in[:, o[n]:o[n + 1]] for n in range(8)]
    z = lambda n: jnp.zeros((D, n), w_in.dtype)
    w_in_p = jnp.concatenate(
        [seg[0], seg[1], z(MLA_NOPE), seg[2], z(LANES - MLA_NOPE - MLA_ROPE), seg[3], seg[4], seg[5],
         seg[6], z(LANES - 2 * GLA_GATE_RANK), seg[7]], axis=1).astype(BF16)

    wq = mla_w_uq.reshape(MLA_Q_LORA, MLA_HEADS, MLA_NOPE + MLA_ROPE)
    w_uq_p = jnp.concatenate(
        [wq[:, :, :MLA_NOPE].reshape(MLA_Q_LORA, -1),
         wq[:, :, MLA_NOPE:MLA_NOPE + HALF_ROPE].reshape(MLA_Q_LORA, -1),
         wq[:, :, MLA_NOPE + HALF_ROPE:].reshape(MLA_Q_LORA, -1)], axis=1)
    wkv = mla_w_ukv.reshape(MLA_KV_LORA, MLA_HEADS, MLA_NOPE + MLA_V)
    w_k = jnp.concatenate([wkv[:, :, :MLA_NOPE], jnp.zeros((MLA_KV_LORA, MLA_HEADS, QK_PAD - MLA_NOPE),
                                                            wkv.dtype)], axis=2)
    w_v = wkv[:, :, MLA_NOPE:].reshape(MLA_KV_LORA, -1)

    def gate_w(w, row0):
        full = jnp.zeros((LANES, GLA_KDIM), w.dtype)
        return full.at[row0:row0 + GLA_GATE_RANK].set(w).astype(BF16)

    E = N_EXPERTS
    w_glu, w_lin, w_dn_b = _expert_weights_call(w_gu, w_dn)
    return {
        "attn_norm": attn_norm.reshape(1, D), "w_in": w_in_p,
        "q_norm": mla_q_norm.reshape(1, -1), "w_uqT": w_uq_p.T.astype(BF16),
        "kv_norm": mla_kv_norm.reshape(1, -1),
        "w_k": w_k.reshape(MLA_KV_LORA, -1).astype(BF16), "w_vT": w_v.T.astype(BF16),
        "w_gf": gate_w(gla_w_gate_fwd, 0), "b_gf": gla_b_gate_fwd.reshape(1, -1),
        "w_gb": gate_w(gla_w_gate_bwd, GLA_GATE_RANK), "b_gb": gla_b_gate_bwd.reshape(1, -1),
        "gla_out_norm": gla_out_norm.reshape(1, -1),
        "w_out_a": w_out[:MLA_WIDTH].astype(BF16), "w_out_b": w_out[MLA_WIDTH:].astype(BF16),
        "ffn_norm": ffn_norm.reshape(1, D),
        "router_wT": router_w.T, "router_b": router_b.reshape(E, 1),
        "w_glu": w_glu, "w_lin": w_lin, "w_dn": w_dn_b,
        "b_glu": b_gu[:, 0::2].reshape(E, 1, D_FF), "b_lin": b_gu[:, 1::2].reshape(E, 1, D_FF),
        "b_dn": b_dn.reshape(E, 1, D),
    }


def _rope_tables(S):
    inv_freq = jnp.power(ROPE_THETA, -jnp.arange(0, MLA_ROPE, 2, dtype=F32) / MLA_ROPE)
    ang = jnp.arange(S, dtype=F32)[:, None] * inv_freq[None, :]
    cos, sin = jnp.cos(ang), jnp.sin(ang)
    z = lambda n: jnp.zeros((S, n), F32)
    tail = LANES - MLA_NOPE - MLA_ROPE
    return {
        "c": jnp.concatenate([z(MLA_NOPE), cos, cos, z(tail)], axis=1),
        "s1": jnp.concatenate([z(MLA_NOPE), -sin, z(HALF_ROPE), z(tail)], axis=1),
        "s2": jnp.concatenate([z(MLA_NOPE), z(HALF_ROPE), sin, z(tail)], axis=1),
        "cosT": jnp.tile(cos.T, (MLA_HEADS, 1)), "sinT": jnp.tile(sin.T, (MLA_HEADS, 1)),
    }


def _encoder(xa, xb, prm, final_norm):
    n_first, S, D = xa.shape
    B = n_first + xb.shape[0]
    rope = _rope_tables(S)
    qT, k, vT, gq, gk, gv, laf, lab, gr = _pre_call(xa, xb, prm, rope, min(TOKEN_TILE, S))
    oT = _attn_call(qT, k, vT, min(ATTN_QUERY_TILE, S), ATTN_KEY_TILE).reshape(B, MLA_WIDTH, S)
    o_f, o_b = _gla_call(gq, gk, gv, laf, lab)
    mixed = (oT, o_f, o_b, gr)
    return (_moe_block(xa, 0, mixed, prm, final_norm), _moe_block(xb, n_first, mixed, prm, final_norm))


def _moe_block(x, batch0, mixed, prm, final_norm):
    B, S, D = x.shape
    x_mid, h2, meta, gates, counts = _post_call(x, batch0, *mixed, prm, min(TOKEN_TILE, S))

    T = B * S
    n_rows = T * TOP_K + N_EXPERTS * MOE_BLOCK
    n_blocks = n_rows // MOE_BLOCK
    cnt = counts[:, 0].astype(jnp.int32)
    padded = ((cnt + MOE_BLOCK - 1) // MOE_BLOCK) * MOE_BLOCK
    pends = jnp.cumsum(padded)
    starts = (pends - padded).astype(jnp.int32)
    n_valid = (pends[-1] // MOE_BLOCK).astype(jnp.int32).reshape(1)
    blk = jnp.minimum(jnp.arange(n_blocks, dtype=jnp.int32), n_valid[0] - 1) * MOE_BLOCK
    block_expert = jnp.minimum(jnp.sum(pends[None, :] <= blk[:, None], axis=1), N_EXPERTS - 1).astype(jnp.int32)

    experts = jnp.arange(N_EXPERTS, dtype=jnp.int32)[:, None, None]
    slots = jnp.sum(jnp.where(meta[None, :TOP_K] == experts, starts[:, None, None], 0), axis=0) + meta[TOP_K:]
    slots = jnp.pad(slots.reshape(TOP_K, T // SC_WINDOW, SC_WINDOW),
                    ((0, 0), (0, 0), (0, SC_INDEX_PAD - SC_WINDOW)))

    xs = _sc_scatter_rows(h2.reshape(T, D), slots, n_rows)
    ys = _expert_call(block_expert, n_valid, xs, prm)
    y4 = _sc_gather_rows(ys, slots, T)
    y = _combine_call(x_mid.reshape(T, D), gates, final_norm.reshape(1, D), y4, min(256, S))
    return y.reshape(x.shape)


def kernel(x_prompt, x_sample, attn_norm, w_in, mla_q_norm, mla_w_uq, mla_kv_norm, mla_w_ukv,
           gla_w_gate_fwd, gla_b_gate_fwd, gla_w_gate_bwd, gla_b_gate_bwd, gla_out_norm, w_out, ffn_norm,
           router_w, router_b, expert_w_gate_up, expert_b_gate_up, expert_w_down, expert_b_down,
           final_norm):
    layer = (attn_norm, w_in, mla_q_norm, mla_w_uq, mla_kv_norm, mla_w_ukv, gla_w_gate_fwd,
             gla_b_gate_fwd, gla_w_gate_bwd, gla_b_gate_bwd, gla_out_norm, w_out, ffn_norm, router_w,
             router_b, expert_w_gate_up, expert_b_gate_up, expert_w_down, expert_b_down)
    assert all(p.shape[0] == 1 for p in layer), "single layer expected"
    assert x_prompt.shape[1:] == x_sample.shape[1:]
    prm = _prep_params(*[p[0] for p in layer])
    return _encoder(x_prompt, x_sample, prm, final_norm)
```

```python
import functools

import jax
import jax.numpy as jnp
import numpy as np
from jax import lax
from jax.experimental import pallas as pl
from jax.experimental.pallas import tpu as pltpu
from jax.experimental.pallas import tpu_sc as plsc

F32 = jnp.float32
BF16 = jnp.bfloat16

D_MODEL = 1024
MLA_HEADS = 8
MLA_NOPE = 64
MLA_ROPE = 32
MLA_V = 64
MLA_Q_LORA = 384
MLA_KV_LORA = 256
ROPE_THETA = 10000.0
GLA_HEADS = 4
GLA_DK = 64
GLA_DV = 128
GLA_GATE_RANK = 16
GLA_GATE_NORM = 16.0
N_EXPERTS = 32
TOP_K = 4
D_FF = 1024
SWIGLU_LIMIT = 7.0
SWIGLU_ALPHA = 1.702
MOE_BLOCK = 512
RMS_EPS = 1e-6

MLA_WIDTH = MLA_HEADS * MLA_V
GLA_WIDTH = GLA_HEADS * GLA_DV
GLA_KDIM = GLA_HEADS * GLA_DK
HALF_ROPE = MLA_ROPE // 2
QK_PAD = 128
LANES = 128
V_ONES_ROWS = 16

OFF_CQ = 0
OFF_CKV = OFF_CQ + MLA_Q_LORA
OFF_KR = OFF_CKV + MLA_KV_LORA
OFF_GQ = OFF_KR + LANES
OFF_GK = OFF_GQ + GLA_KDIM
OFF_GV = OFF_GK + GLA_KDIM
OFF_LR = OFF_GV + GLA_WIDTH
OFF_GR = OFF_LR + LANES
PROJ_COLS = OFF_GR + GLA_WIDTH

TOKEN_TILE = 512
ATTN_KEY_TILE = 256
ATTN_QUERY_TILE = 512
ATTN_LOOKAHEAD = 2
ATTN_SCORE_SLOTS = 4
GLA_TILE = 256
GLA_LEVELS = 8
SC_WINDOW = 32
SC_INDEX_PAD = 128
WEIGHT_PREP_ROWS = 256
PERM_GROUP = 256
VMEM_LIMIT = 56 * 1024 * 1024

LOG2_E = 1.4426950408889634
NT_DIMS = (((1,), (1,)), ((), ()))
TN_DIMS = (((0,), (0,)), ((), ()))


def _cparams(sem):
    return pltpu.CompilerParams(dimension_semantics=sem, vmem_limit_bytes=VMEM_LIMIT)


def _rms(x, gain):
    return x * lax.rsqrt(jnp.mean(x * x, axis=-1, keepdims=True) + RMS_EPS) * gain


def _split_bf16(x):
    hi = x.astype(BF16)
    lo = (x - hi.astype(F32)).astype(BF16)
    return hi, lo


def _pre_kernel(xa_ref, xb_ref, an_ref, win_ref, qn_ref, wuqT_ref, kvn_ref, wk_ref, wvT_ref,
                wgf_ref, bgf_ref, wgb_ref, bgb_ref, rc_ref, rs1_ref, rs2_ref, cosT_ref, sinT_ref,
                qT_ref, k_ref, vT_ref, gq_ref, gk_ref, gv_ref, laf_ref, lab_ref, gr_ref, *, n_first):
    x = jnp.where(pl.program_id(0) < n_first, xa_ref[...], xb_ref[...])
    h = _rms(x, an_ref[...]).astype(BF16)
    proj = jnp.dot(h, win_ref[...], preferred_element_type=F32)

    cqn = _rms(proj[:, OFF_CQ:OFF_CQ + MLA_Q_LORA], qn_ref[...]).astype(BF16)
    ckvn = _rms(proj[:, OFF_CKV:OFF_CKV + MLA_KV_LORA], kvn_ref[...]).astype(BF16)

    scale = (MLA_NOPE + MLA_ROPE) ** -0.5 * LOG2_E
    qT = lax.dot_general(wuqT_ref[...], cqn, NT_DIMS, preferred_element_type=F32) * scale
    n_nope = MLA_HEADS * MLA_NOPE
    n_half = MLA_HEADS * HALF_ROPE
    x1 = qT[n_nope:n_nope + n_half]
    x2 = qT[n_nope + n_half:n_nope + 2 * n_half]
    c = cosT_ref[...]
    s = sinT_ref[...]
    x1r = x1 * c - x2 * s
    x2r = x1 * s + x2 * c
    zpad = jnp.zeros((QK_PAD - MLA_NOPE - MLA_ROPE, qT.shape[1]), BF16)
    for hd in range(MLA_HEADS):
        qT_ref[hd, 0:MLA_NOPE, :] = qT[hd * MLA_NOPE:(hd + 1) * MLA_NOPE].astype(BF16)
        qT_ref[hd, MLA_NOPE:MLA_NOPE + HALF_ROPE, :] = x1r[hd * HALF_ROPE:(hd + 1) * HALF_ROPE].astype(BF16)
        qT_ref[hd, MLA_NOPE + HALF_ROPE:MLA_NOPE + MLA_ROPE, :] = (
            x2r[hd * HALF_ROPE:(hd + 1) * HALF_ROPE].astype(BF16))
        qT_ref[hd, MLA_NOPE + MLA_ROPE:QK_PAD, :] = zpad

    kr = proj[:, OFF_KR:OFF_KR + LANES]
    kr = (kr * rc_ref[...]
          + pltpu.roll(kr, LANES - HALF_ROPE, axis=1) * rs1_ref[...]
          + pltpu.roll(kr, HALF_ROPE, axis=1) * rs2_ref[...])
    kfull = jnp.dot(ckvn, wk_ref[...], preferred_element_type=F32)
    for hd in range(MLA_HEADS):
        k_ref[hd] = (kfull[:, hd * QK_PAD:(hd + 1) * QK_PAD] + kr).astype(BF16)

    vT = lax.dot_general(wvT_ref[...], ckvn, NT_DIMS, preferred_element_type=F32)
    for hd in range(MLA_HEADS):
        vT_ref[hd] = vT[hd * MLA_V:(hd + 1) * MLA_V].astype(BF16)

    gq_ref[...] = proj[:, OFF_GQ:OFF_GQ + GLA_KDIM] * (GLA_DK ** -0.5)
    gk_ref[...] = proj[:, OFF_GK:OFF_GK + GLA_KDIM]
    gv_ref[...] = proj[:, OFF_GV:OFF_GV + GLA_WIDTH]
    gr_ref[...] = proj[:, OFF_GR:OFF_GR + GLA_WIDTH]
    lr = proj[:, OFF_LR:OFF_LR + LANES].astype(BF16)

    def log_decay(w_ref, b_ref):
        z = jnp.dot(lr, w_ref[...], preferred_element_type=F32) + b_ref[...]
        return (jnp.minimum(z, 0.0) - jnp.log1p(jnp.exp(-jnp.abs(z)))) * (1.0 / GLA_GATE_NORM)

    laf_ref[...] = log_decay(wgf_ref, bgf_ref)
    lab_ref[...] = log_decay(wgb_ref, bgb_ref)


def _two_source_specs(n_first, tm, width):
    return [pl.BlockSpec((None, tm, width), lambda b, i: (jnp.minimum(b, n_first - 1), i, 0)),
            pl.BlockSpec((None, tm, width), lambda b, i: (jnp.maximum(b - n_first, 0), i, 0))]


def _pre_call(xa, xb, prm, rope, tm):
    n_first, S, D = xa.shape
    B = n_first + xb.shape[0]
    nS = S // tm
    H = MLA_HEADS

    def full(a):
        nd = a.ndim
        return pl.BlockSpec(a.shape, lambda b, i, _nd=nd: (0,) * _nd)

    tok = lambda w: pl.BlockSpec((None, tm, w), lambda b, i: (b, i, 0))
    in_specs = [
        *_two_source_specs(n_first, tm, D), full(prm["attn_norm"]), full(prm["w_in"]), full(prm["q_norm"]), full(prm["w_uqT"]),
        full(prm["kv_norm"]), full(prm["w_k"]), full(prm["w_vT"]),
        full(prm["w_gf"]), full(prm["b_gf"]), full(prm["w_gb"]), full(prm["b_gb"]),
        pl.BlockSpec((tm, LANES), lambda b, i: (i, 0)),
        pl.BlockSpec((tm, LANES), lambda b, i: (i, 0)),
        pl.BlockSpec((tm, LANES), lambda b, i: (i, 0)),
        pl.BlockSpec((H * HALF_ROPE, tm), lambda b, i: (0, i)),
        pl.BlockSpec((H * HALF_ROPE, tm), lambda b, i: (0, i)),
    ]
    out_shape = [
        jax.ShapeDtypeStruct((B, H, QK_PAD, S), BF16),
        jax.ShapeDtypeStruct((B, H, S, QK_PAD), BF16),
        jax.ShapeDtypeStruct((B, H, nS, MLA_V, tm), BF16),
        jax.ShapeDtypeStruct((B, S, GLA_KDIM), F32),
        jax.ShapeDtypeStruct((B, S, GLA_KDIM), F32),
        jax.ShapeDtypeStruct((B, S, GLA_WIDTH), F32),
        jax.ShapeDtypeStruct((B, S, GLA_KDIM), F32),
        jax.ShapeDtypeStruct((B, S, GLA_KDIM), F32),
        jax.ShapeDtypeStruct((B, S, GLA_WIDTH), F32),
    ]
    out_specs = [
        pl.BlockSpec((None, H, QK_PAD, tm), lambda b, i: (b, 0, 0, i)),
        pl.BlockSpec((None, H, tm, QK_PAD), lambda b, i: (b, 0, i, 0)),
        pl.BlockSpec((None, H, None, MLA_V, tm), lambda b, i: (b, 0, i, 0, 0)),
        tok(GLA_KDIM), tok(GLA_KDIM), tok(GLA_WIDTH), tok(GLA_KDIM), tok(GLA_KDIM), tok(GLA_WIDTH),
    ]
    return pl.pallas_call(
        functools.partial(_pre_kernel, n_first=n_first), grid=(B, nS), in_specs=in_specs, out_specs=out_specs, out_shape=out_shape,
        compiler_params=_cparams(("parallel", "parallel")), name="pre",
    )(xa, xb, prm["attn_norm"], prm["w_in"], prm["q_norm"], prm["w_uqT"], prm["kv_norm"], prm["w_k"],
      prm["w_vT"], prm["w_gf"], prm["b_gf"], prm["w_gb"], prm["b_gb"],
      rope["c"], rope["s1"], rope["s2"], rope["cosT"], rope["sinT"])


def _attn_kernel(qT_ref, k_ref, vT_ref, o_ref, s_ref, *, n_kblk):
    qT = qT_ref[...]
    tq = qT.shape[1]
    tk = k_ref.shape[0] // n_kblk

    def scores(j):
        sT = jnp.dot(k_ref[j * tk:(j + 1) * tk, :], qT, preferred_element_type=F32)
        s_ref[j % ATTN_SCORE_SLOTS] = sT
        return jnp.max(sT, axis=0, keepdims=True)

    def values(j, p):
        per_tile = vT_ref.shape[2] // tk
        v_blk = vT_ref[j // per_tile, :, (j % per_tile) * tk:(j % per_tile + 1) * tk]
        vext = jnp.concatenate([v_blk, ones], axis=0)
        return jnp.dot(vext, p, preferred_element_type=F32)

    def softmax(j, blk_max, m):
        m_new = jnp.maximum(m, blk_max)
        p = jnp.exp2(s_ref[j % ATTN_SCORE_SLOTS] - m_new).astype(BF16)
        return m_new, jnp.exp2(m - m_new), p

    ones = jnp.ones((V_ONES_ROWS, tk), BF16)
    m = jnp.full((1, tq), -jnp.inf, F32)
    acc = jnp.zeros((MLA_V + V_ONES_ROWS, tq), F32)
    blk_max = {j: scores(j) for j in range(ATTN_LOOKAHEAD)}
    for j in range(n_kblk):
        if j + ATTN_LOOKAHEAD < n_kblk:
            blk_max[j + ATTN_LOOKAHEAD] = scores(j + ATTN_LOOKAHEAD)
        m, alpha, p = softmax(j, blk_max.pop(j), m)
        acc = alpha * acc + values(j, p)
    o_ref[...] = (acc[0:MLA_V] / acc[MLA_V:MLA_V + 1]).astype(o_ref.dtype)


def _attn_call(qT, k, vT, tq, tk):
    B, H, _, S = qT.shape
    n_vblk, v_tile = vT.shape[2], vT.shape[4]
    n_kblk = S // tk
    assert n_kblk >= 2 and S % tq == 0 and v_tile % tk == 0
    return pl.pallas_call(
        functools.partial(_attn_kernel, n_kblk=n_kblk),
        grid=(B, H, S // tq),
        in_specs=[
            pl.BlockSpec((None, None, QK_PAD, tq), lambda b, h, i: (b, h, 0, i)),
            pl.BlockSpec((None, None, S, QK_PAD), lambda b, h, i: (b, h, 0, 0)),
            pl.BlockSpec((None, None, n_vblk, MLA_V, v_tile), lambda b, h, i: (b, h, 0, 0, 0)),
        ],
        out_specs=pl.BlockSpec((None, None, MLA_V, tq), lambda b, h, i: (b, h, 0, i)),
        out_shape=jax.ShapeDtypeStruct((B, H, MLA_V, S), BF16),
        scratch_shapes=[pltpu.VMEM((ATTN_SCORE_SLOTS, tk, tq), F32)],
        compiler_params=_cparams(("parallel", "parallel", "parallel")), name="attn",
    )(qT, k, vT)


def _gla_consts():
    L = GLA_TILE
    i = np.arange(L)[:, None]
    j = np.arange(L)[None, :]
    x = i ^ j
    lidx = np.where(x > 0, np.floor(np.log2(np.maximum(x, 1))), -1).astype(np.int32)
    lidx_f = np.where(i > j, lidx, -1).astype(np.int32)
    lidx_b = np.where(i < j, lidx, -1).astype(np.int32)
    hd = np.arange(GLA_KDIM)[:, None] // GLA_DK
    hv = np.arange(GLA_WIDTH)[None, :] // GLA_DV
    bexp = (hd == hv).astype(np.float32)
    return jnp.asarray(lidx_f), jnp.asarray(lidx_b), jnp.asarray(bexp, BF16)


def _gla_direction(q, k, v, la, lidx, bexp_ref, ssum_ref, state_ref, o_ref, forward):
    L = GLA_TILE
    la_hi, la_lo = _split_bf16(la)
    vb = v.astype(BF16)
    lane_head = lax.broadcasted_iota(jnp.int32, (1, GLA_KDIM), 1) // GLA_DK
    head_masks = [(lane_head == h).astype(BF16) for h in range(GLA_HEADS)]
    row = lax.broadcasted_iota(jnp.int32, la.shape, 0)

    def widen(c, t, m):
        upper_half = (row & m) != 0
        sibling_total = jnp.where(upper_half, pltpu.roll(t, m, axis=0), pltpu.roll(t, L - m, axis=0))
        return c + jnp.where(upper_half, sibling_total, 0.0), t + sibling_total

    ssum_ref[...] = jnp.zeros_like(ssum_ref)
    c, t = la, la
    for lvl in range(GLA_LEVELS):
        if lvl > 0:
            c, t = widen(c, t, 1 << (lvl - 1))
        if forward:
            eq, ek = c, t - c
        else:
            eq, ek = t - c + la, c - la
        ql = (q * jnp.exp(eq)).astype(BF16)
        kl = (k * jnp.exp(ek)).astype(BF16)
        sel = lidx == lvl
        for h in range(GLA_HEADS):
            sc = lax.dot_general(ql * head_masks[h], kl, NT_DIMS, preferred_element_type=F32)
            ssum_ref[h] = jnp.where(sel, sc, ssum_ref[h])

    c, t = widen(c, t, L // 2)
    if forward:
        eq, ek = c, t - c
    else:
        eq, ek = t - c + la, c - la
    q_in = (q * jnp.exp(eq)).astype(BF16)
    k_out = (k * jnp.exp(ek)).astype(BF16)
    o = jnp.dot(q_in, state_ref[...].astype(BF16), preferred_element_type=F32)
    if forward:
        o = o + jnp.dot((q * k).astype(BF16), bexp_ref[...], preferred_element_type=F32) * v
    for h in range(GLA_HEADS):
        oh = jnp.dot(ssum_ref[h].astype(BF16), vb[:, h * GLA_DV:(h + 1) * GLA_DV],
                     preferred_element_type=F32)
        o_ref[:, h * GLA_DV:(h + 1) * GLA_DV] = o[:, h * GLA_DV:(h + 1) * GLA_DV] + oh

    ones = jnp.ones((L, LANES), BF16)
    tot_col = (lax.dot_general(la_hi, ones, TN_DIMS, preferred_element_type=F32)
               + lax.dot_general(la_lo, ones, TN_DIMS, preferred_element_type=F32))
    dec = jnp.exp(tot_col)
    upd = lax.dot_general(k_out, vb, TN_DIMS, preferred_element_type=F32)
    for h in range(GLA_HEADS):
        cols = slice(h * GLA_DV, (h + 1) * GLA_DV)
        state_ref[:, cols] = dec * state_ref[:, cols] + upd[:, cols] * bexp_ref[:, cols].astype(F32)


def _gla_kernel(qf_ref, kf_ref, vf_ref, laf_ref, qb_ref, kb_ref, vb_ref, lab_ref,
                lidxf_ref, lidxb_ref, bexp_ref, of_ref, ob_ref, sf_ref, sb_ref, ssum_ref):
    @pl.when(pl.program_id(1) == 0)
    def _():
        sf_ref[...] = jnp.zeros_like(sf_ref)
        sb_ref[...] = jnp.zeros_like(sb_ref)

    _gla_direction(qf_ref[...], kf_ref[...], vf_ref[...], laf_ref[...], lidxf_ref[...],
                   bexp_ref, ssum_ref, sf_ref, of_ref, True)
    _gla_direction(qb_ref[...], kb_ref[...], vb_ref[...], lab_ref[...], lidxb_ref[...],
                   bexp_ref, ssum_ref, sb_ref, ob_ref, False)


def _gla_call(gq, gk, gv, laf, lab):
    B, S, _ = gq.shape
    L = GLA_TILE
    n = S // L
    lidx_f, lidx_b, bexp = _gla_consts()
    fwd = lambda w: pl.BlockSpec((None, L, w), lambda b, i: (b, i, 0))
    bwd = lambda w: pl.BlockSpec((None, L, w), lambda b, i: (b, n - 1 - i, 0))
    const = lambda a: pl.BlockSpec(a.shape, lambda b, i, _nd=a.ndim: (0,) * _nd)
    return pl.pallas_call(
        _gla_kernel, grid=(B, n),
        in_specs=[fwd(GLA_KDIM), fwd(GLA_KDIM), fwd(GLA_WIDTH), fwd(GLA_KDIM),
                  bwd(GLA_KDIM), bwd(GLA_KDIM), bwd(GLA_WIDTH), bwd(GLA_KDIM),
                  const(lidx_f), const(lidx_b), const(bexp)],
        out_specs=[fwd(GLA_WIDTH), bwd(GLA_WIDTH)],
        out_shape=[jax.ShapeDtypeStruct((B, S, GLA_WIDTH), F32)] * 2,
        scratch_shapes=[pltpu.VMEM((GLA_KDIM, GLA_WIDTH), F32), pltpu.VMEM((GLA_KDIM, GLA_WIDTH), F32),
                        pltpu.VMEM((GLA_HEADS, L, L), F32)],
        compiler_params=_cparams(("parallel", "arbitrary")), name="gla",
    )(gq, gk, gv, laf, gq, gk, gv, lab, lidx_f, lidx_b, bexp)


def _post_kernel(x_ref, oT_ref, of_ref, ob_ref, gr_ref, gon_ref, woa_ref, wob_ref, fn_ref,
                 rwT_ref, rb_ref, upper_ref,
                 xmid_ref, h2_ref, meta_ref, gates_ref, counts_ref, carry_ref):
    first = (pl.program_id(0) == 0) & (pl.program_id(1) == 0)

    @pl.when(first)
    def _():
        carry_ref[...] = jnp.zeros_like(carry_ref)

    o = of_ref[...] + ob_ref[...]
    gr = gr_ref[...]
    parts = []
    for h in range(GLA_HEADS):
        cols = slice(h * GLA_DV, (h + 1) * GLA_DV)
        parts.append(_rms(o[:, cols], gon_ref[...]) * jax.nn.silu(gr[:, cols]))
    gla = jnp.concatenate(parts, axis=1).astype(BF16)

    x_mid = (x_ref[...]
             + lax.dot_general(oT_ref[...], woa_ref[...], TN_DIMS, preferred_element_type=F32)
             + jnp.dot(gla, wob_ref[...], preferred_element_type=F32))
    xmid_ref[...] = x_mid
    h2 = _rms(x_mid, fn_ref[...])
    h2_ref[...] = h2

    h_hi, h_lo = _split_bf16(h2)
    w_hi, w_lo = _split_bf16(rwT_ref[...])
    by_h_hi = lax.dot_general(jnp.concatenate([w_hi, w_lo], axis=0), h_hi, NT_DIMS,
                              preferred_element_type=F32)
    logits = (by_h_hi[:N_EXPERTS] + by_h_hi[N_EXPERTS:]
              + lax.dot_general(w_hi, h_lo, NT_DIMS, preferred_element_type=F32)
              + rb_ref[...])
    tm = logits.shape[1]
    eidx = lax.broadcasted_iota(jnp.int32, (N_EXPERTS, tm), 0).astype(F32)
    vals, idxs, sels = [], [], []
    cur = logits
    for _ in range(TOP_K):
        mk = jnp.max(cur, axis=0, keepdims=True)
        ik = jnp.min(jnp.where(cur == mk, eidx, float(N_EXPERTS)), axis=0, keepdims=True)
        sel = eidx == ik
        vals.append(mk)
        idxs.append(ik)
        sels.append(sel)
        cur = jnp.where(sel, -jnp.inf, cur)
    exps = [jnp.exp(vk - vals[0]) for vk in vals]
    denom = exps[0] + exps[1] + exps[2] + exps[3]
    gates = [e / denom for e in exps]

    cnt = (sels[0] | sels[1] | sels[2] | sels[3])
    before = jnp.dot(cnt.astype(BF16), upper_ref[...], preferred_element_type=F32) + carry_ref[:, 0:1]
    ranks = [jnp.sum(jnp.where(sel, before, 0.0), axis=0, keepdims=True) for sel in sels]
    carry_ref[...] = carry_ref[...] + jnp.sum(cnt.astype(F32), axis=1, keepdims=True)
    counts_ref[...] = carry_ref[...]

    meta_ref[...] = jnp.concatenate(idxs + ranks, axis=0).astype(jnp.int32)
    gates_ref[...] = jnp.concatenate(gates + [jnp.zeros((TOP_K, tm), F32)], axis=0)


def _post_call(x, batch0, oT, o_f, o_b, gr, prm, tm):
    B, S, D = x.shape
    nS = S // tm
    upper = jnp.asarray(np.triu(np.ones((tm, tm), np.float32), k=1), BF16)
    full = lambda a: pl.BlockSpec(a.shape, lambda b, i, _nd=a.ndim: (0,) * _nd)
    own = lambda w: pl.BlockSpec((None, tm, w), lambda b, i: (b, i, 0))
    tok = lambda w: pl.BlockSpec((None, tm, w), lambda b, i: (batch0 + b, i, 0))
    colblk = lambda r: pl.BlockSpec((None, r, tm), lambda b, i: (batch0 + b, 0, i))
    flat = lambda r: pl.BlockSpec((r, tm), lambda b, i: (0, b * nS + i))
    return pl.pallas_call(
        _post_kernel, grid=(B, nS),
        in_specs=[own(D), colblk(MLA_WIDTH), tok(GLA_WIDTH), tok(GLA_WIDTH), tok(GLA_WIDTH),
                  full(prm["gla_out_norm"]), full(prm["w_out_a"]), full(prm["w_out_b"]),
                  full(prm["ffn_norm"]), full(prm["router_wT"]), full(prm["router_b"]), full(upper)],
        out_specs=[own(D), own(D), flat(2 * TOP_K), flat(2 * TOP_K),
                   pl.BlockSpec((N_EXPERTS, LANES), lambda b, i: (0, 0))],
        out_shape=[jax.ShapeDtypeStruct((B, S, D), F32), jax.ShapeDtypeStruct((B, S, D), F32),
                   jax.ShapeDtypeStruct((2 * TOP_K, B * S), jnp.int32),
                   jax.ShapeDtypeStruct((2 * TOP_K, B * S), F32),
                   jax.ShapeDtypeStruct((N_EXPERTS, LANES), F32)],
        scratch_shapes=[pltpu.VMEM((N_EXPERTS, LANES), F32)],
        compiler_params=_cparams(("arbitrary", "arbitrary")), name="post",
    )(x, oT, o_f, o_b, gr, prm["gla_out_norm"], prm["w_out_a"], prm["w_out_b"], prm["ffn_norm"],
      prm["router_wT"], prm["router_b"], upper)


def _sc_mesh():
    return plsc.VectorSubcoreMesh(core_axis_name="core", subcore_axis_name="subcore")


def _sc_scatter_rows(x, slots, n_rows):
    T, D = x.shape
    K = slots.shape[0]

    @pl.kernel(out_type=jax.ShapeDtypeStruct((n_rows, D), x.dtype), mesh=_sc_mesh(), scratch_types=[])
    def scatter(x_hbm, slots_hbm, out_hbm):
        def window(x_vmem, slots_vmem):
            for kk in range(K):
                pltpu.sync_copy(x_vmem, out_hbm.at[slots_vmem.at[kk, 0, pl.ds(0, SC_WINDOW)]])

        pltpu.emit_pipeline(
            window, grid=(T // SC_WINDOW,),
            in_specs=[pl.BlockSpec((SC_WINDOW, D), lambda i: (i, 0)),
                      pl.BlockSpec((K, 1, SC_INDEX_PAD), lambda i: (0, i, 0))],
            out_specs=[], core_axis_name=("core", "subcore"),
            dimension_semantics=(pltpu.PARALLEL,))(x_hbm, slots_hbm)

    return scatter(x, slots)


def _sc_gather_rows(y, slots, T):
    D = y.shape[1]
    K = slots.shape[0]

    @pl.kernel(out_type=jax.ShapeDtypeStruct((K, T, D), y.dtype), mesh=_sc_mesh(), scratch_types=[])
    def gather(y_hbm, slots_hbm, out_hbm):
        def window(slots_vmem, out_vmem):
            pltpu.sync_copy(y_hbm.at[slots_vmem.at[0, 0, pl.ds(0, SC_WINDOW)]], out_vmem.at[0])

        pltpu.emit_pipeline(
            window, grid=(K, T // SC_WINDOW),
            in_specs=[pl.BlockSpec((1, 1, SC_INDEX_PAD), lambda kk, i: (kk, i, 0))],
            out_specs=[pl.BlockSpec((1, SC_WINDOW, D), lambda kk, i: (kk, i, 0))],
            core_axis_name=("core", "subcore"),
            dimension_semantics=(pltpu.PARALLEL, pltpu.PARALLEL))(slots_hbm, out_hbm)

    return gather(y, slots)


def _expert_weights_kernel(wgu_ref, wdn_ref, perm_ref, wg_ref, wl_ref, wd_ref):
    perm = perm_ref[...]
    half = PERM_GROUP // 2
    for c in range(2 * D_FF // PERM_GROUP):
        w = wgu_ref[:, c * PERM_GROUP:(c + 1) * PERM_GROUP].astype(BF16)
        sep = jnp.dot(w, perm, preferred_element_type=F32).astype(BF16)
        wg_ref[:, c * half:(c + 1) * half] = sep[:, :half]
        wl_ref[:, c * half:(c + 1) * half] = sep[:, half:]
    wd_ref[...] = wdn_ref[...].astype(BF16)


def _expert_weights_call(w_gu, w_dn):
    E, D, _ = w_gu.shape
    half = PERM_GROUP // 2
    src = np.concatenate([2 * np.arange(half), 2 * np.arange(half) + 1])
    perm = jnp.asarray(np.arange(PERM_GROUP)[:, None] == src[None, :], BF16)
    rows = WEIGHT_PREP_ROWS
    slab = lambda c: pl.BlockSpec((None, rows, c), lambda e, r: (e, r, 0))
    return pl.pallas_call(
        _expert_weights_kernel, grid=(E, D // rows),
        in_specs=[slab(2 * D_FF), slab(D), pl.BlockSpec((PERM_GROUP, PERM_GROUP), lambda e, r: (0, 0))],
        out_specs=[slab(D_FF), slab(D_FF), slab(D)],
        out_shape=[jax.ShapeDtypeStruct((E, D, D_FF), BF16), jax.ShapeDtypeStruct((E, D, D_FF), BF16),
                   jax.ShapeDtypeStruct((E, D_FF, D), BF16)],
        compiler_params=_cparams(("parallel", "parallel")), name="expert_weights",
    )(w_gu, w_dn, perm)


def _expert_kernel(be_ref, nv_ref, xs_ref, wg_ref, wl_ref, bg_ref, bl_ref, wd_ref, bd_ref, ys_ref):
    del be_ref
    valid = pl.program_id(0) < nv_ref[0]

    @pl.when(valid)
    def _():
        xb = xs_ref[...].astype(BF16)
        g = jnp.dot(xb, wg_ref[...], preferred_element_type=F32) + bg_ref[...]
        l = jnp.dot(xb, wl_ref[...], preferred_element_type=F32) + bl_ref[...]
        glu = jnp.minimum(g, SWIGLU_LIMIT)
        lin = jnp.clip(l, -SWIGLU_LIMIT, SWIGLU_LIMIT)
        act = glu * jax.nn.sigmoid(SWIGLU_ALPHA * glu) * (lin + 1.0)
        ys_ref[...] = jnp.dot(act.astype(BF16), wd_ref[...], preferred_element_type=F32) + bd_ref[...]

    @pl.when(jnp.logical_not(valid))
    def _():
        ys_ref[...] = jnp.zeros_like(ys_ref)


def _expert_call(block_expert, n_valid, xs, prm):
    n_rows, D = xs.shape
    n_blocks = n_rows // MOE_BLOCK
    rows = lambda i, be, nv: (jnp.minimum(i, nv[0] - 1), 0)
    wsel = lambda i, be, nv: (be[i], 0, 0)
    grid_spec = pltpu.PrefetchScalarGridSpec(
        num_scalar_prefetch=2, grid=(n_blocks,),
        in_specs=[pl.BlockSpec((MOE_BLOCK, D), rows),
                  pl.BlockSpec((None, D, D_FF), wsel), pl.BlockSpec((None, D, D_FF), wsel),
                  pl.BlockSpec((None, 1, D_FF), wsel), pl.BlockSpec((None, 1, D_FF), wsel),
                  pl.BlockSpec((None, D_FF, D), wsel), pl.BlockSpec((None, 1, D), wsel)],
        out_specs=pl.BlockSpec((MOE_BLOCK, D), lambda i, be, nv: (i, 0)))
    return pl.pallas_call(
        _expert_kernel, grid_spec=grid_spec,
        out_shape=jax.ShapeDtypeStruct((n_rows, D), F32),
        compiler_params=_cparams(("arbitrary",)), name="experts",
    )(block_expert, n_valid, xs, prm["w_glu"], prm["w_lin"], prm["b_glu"], prm["b_lin"],
      prm["w_dn"], prm["b_dn"])


def _combine_kernel(xmid_ref, gates_ref, fnorm_ref, y4_ref, out_ref):
    g = jnp.transpose(gates_ref[...])
    y = xmid_ref[...]
    for kk in range(TOP_K):
        y = y + y4_ref[kk] * g[:, kk:kk + 1]
    out_ref[...] = _rms(y, fnorm_ref[...])


def _combine_call(x_mid, gates, final_norm, y4, tc):
    T, D = x_mid.shape
    return pl.pallas_call(
        _combine_kernel, grid=(T // tc,),
        in_specs=[pl.BlockSpec((tc, D), lambda i: (i, 0)),
                  pl.BlockSpec((2 * TOP_K, tc), lambda i: (0, i)),
                  pl.BlockSpec(final_norm.shape, lambda i: (0, 0)),
                  pl.BlockSpec((TOP_K, tc, D), lambda i: (0, i, 0))],
        out_specs=pl.BlockSpec((tc, D), lambda i: (i, 0)),
        out_shape=jax.ShapeDtypeStruct((T, D), F32),
        compiler_params=_cparams(("parallel",)), name="combine",
    )(x_mid, gates, final_norm, y4)


def _prep_params(attn_norm, w_in, mla_q_norm, mla_w_uq, mla_kv_norm, mla_w_ukv, gla_w_gate_fwd,
                 gla_b_gate_fwd, gla_w_gate_bwd, gla_b_gate_bwd, gla_out_norm, w_out, ffn_norm,
                 router_w, router_b, w_gu, b_gu, w_dn, b_dn):
    D = D_MODEL
    o = np.cumsum((0, MLA_Q_LORA, MLA_KV_LORA, MLA_ROPE, GLA_KDIM, GLA_KDIM, GLA_WIDTH,
                   2 * GLA_GATE_RANK, GLA_WIDTH))
    seg = [w_in[:, o[n]:o[n + 1]] for n in range(8)]
    z = lambda n: jnp.zeros((D, n), w_in.dtype)
    w_in_p = jnp.concatenate(
        [seg[0], seg[1], z(MLA_NOPE), seg[2], z(LANES - MLA_NOPE - MLA_ROPE), seg[3], seg[4], seg[5],
         seg[6], z(LANES - 2 * GLA_GATE_RANK), seg[7]], axis=1).astype(BF16)

    wq = mla_w_uq.reshape(MLA_Q_LORA, MLA_HEADS, MLA_NOPE + MLA_ROPE)
    w_uq_p = jnp.concatenate(
        [wq[:, :, :MLA_NOPE].reshape(MLA_Q_LORA, -1),
         wq[:, :, MLA_NOPE:MLA_NOPE + HALF_ROPE].reshape(MLA_Q_LORA, -1),
         wq[:, :, MLA_NOPE + HALF_ROPE:].reshape(MLA_Q_LORA, -1)], axis=1)
    wkv = mla_w_ukv.reshape(MLA_KV_LORA, MLA_HEADS, MLA_NOPE + MLA_V)
    w_k = jnp.concatenate([wkv[:, :, :MLA_NOPE], jnp.zeros((MLA_KV_LORA, MLA_HEADS, QK_PAD - MLA_NOPE),
                                                            wkv.dtype)], axis=2)
    w_v = wkv[:, :, MLA_NOPE:].reshape(MLA_KV_LORA, -1)

    def gate_w(w, row0):
        full = jnp.zeros((LANES, GLA_KDIM), w.dtype)
        return full.at[row0:row0 + GLA_GATE_RANK].set(w).astype(BF16)

    E = N_EXPERTS
    w_glu, w_lin, w_dn_b = _expert_weights_call(w_gu, w_dn)
    return {
        "attn_norm": attn_norm.reshape(1, D), "w_in": w_in_p,
        "q_norm": mla_q_norm.reshape(1, -1), "w_uqT": w_uq_p.T.astype(BF16),
        "kv_norm": mla_kv_norm.reshape(1, -1),
        "w_k": w_k.reshape(MLA_KV_LORA, -1).astype(BF16), "w_vT": w_v.T.astype(BF16),
        "w_gf": gate_w(gla_w_gate_fwd, 0), "b_gf": gla_b_gate_fwd.reshape(1, -1),
        "w_gb": gate_w(gla_w_gate_bwd, GLA_GATE_RANK), "b_gb": gla_b_gate_bwd.reshape(1, -1),
        "gla_out_norm": gla_out_norm.reshape(1, -1),
        "w_out_a": w_out[:MLA_WIDTH].astype(BF16), "w_out_b": w_out[MLA_WIDTH:].astype(BF16),
        "ffn_norm": ffn_norm.reshape(1, D),
        "router_wT": router_w.T, "router_b": router_b.reshape(E, 1),
        "w_glu": w_glu, "w_lin": w_lin, "w_dn": w_dn_b,
        "b_glu": b_gu[:, 0::2].reshape(E, 1, D_FF), "b_lin": b_gu[:, 1::2].reshape(E, 1, D_FF),
        "b_dn": b_dn.reshape(E, 1, D),
    }


def _rope_tables(S):
    inv_freq = jnp.power(ROPE_THETA, -jnp.arange(0, MLA_ROPE, 2, dtype=F32) / MLA_ROPE)
    ang = jnp.arange(S, dtype=F32)[:, None] * inv_freq[None, :]
    cos, sin = jnp.cos(ang), jnp.sin(ang)
    z = lambda n: jnp.zeros((S, n), F32)
    tail = LANES - MLA_NOPE - MLA_ROPE
    return {
        "c": jnp.concatenate([z(MLA_NOPE), cos, cos, z(tail)], axis=1),
        "s1": jnp.concatenate([z(MLA_NOPE), -sin, z(HALF_ROPE), z(tail)], axis=1),
        "s2": jnp.concatenate([z(MLA_NOPE), z(HALF_ROPE), sin, z(tail)], axis=1),
        "cosT": jnp.tile(cos.T, (MLA_HEADS, 1)), "sinT": jnp.tile(sin.T, (MLA_HEADS, 1)),
    }


def _encoder(xa, xb, prm, final_norm):
    n_first, S, D = xa.shape
    B = n_first + xb.shape[0]
    rope = _rope_tables(S)
    qT, k, vT, gq, gk, gv, laf, lab, gr = _pre_call(xa, xb, prm, rope, min(TOKEN_TILE, S))
    oT = _attn_call(qT, k, vT, min(ATTN_QUERY_TILE, S), ATTN_KEY_TILE).reshape(B, MLA_WIDTH, S)
    o_f, o_b = _gla_call(gq, gk, gv, laf, lab)
    mixed = (oT, o_f, o_b, gr)
    return (_moe_block(xa, 0, mixed, prm, final_norm), _moe_block(xb, n_first, mixed, prm, final_norm))


def _moe_block(x, batch0, mixed, prm, final_norm):
    B, S, D = x.shape
    x_mid, h2, meta, gates, counts = _post_call(x, batch0, *mixed, prm, min(TOKEN_TILE, S))

    T = B * S
    n_rows = T * TOP_K + N_EXPERTS * MOE_BLOCK
    n_blocks = n_rows // MOE_BLOCK
    cnt = counts[:, 0].astype(jnp.int32)
    padded = ((cnt + MOE_BLOCK - 1) // MOE_BLOCK) * MOE_BLOCK
    pends = jnp.cumsum(padded)
    starts = (pends - padded).astype(jnp.int32)
    n_valid = (pends[-1] // MOE_BLOCK).astype(jnp.int32).reshape(1)
    blk = jnp.minimum(jnp.arange(n_blocks, dtype=jnp.int32), n_valid[0] - 1) * MOE_BLOCK
    block_expert = jnp.minimum(jnp.sum(pends[None, :] <= blk[:, None], axis=1), N_EXPERTS - 1).astype(jnp.int32)

    experts = jnp.arange(N_EXPERTS, dtype=jnp.int32)[:, None, None]
    slots = jnp.sum(jnp.where(meta[None, :TOP_K] == experts, starts[:, None, None], 0), axis=0) + meta[TOP_K:]
    slots = jnp.pad(slots.reshape(TOP_K, T // SC_WINDOW, SC_WINDOW),
                    ((0, 0), (0, 0), (0, SC_INDEX_PAD - SC_WINDOW)))

    xs = _sc_scatter_rows(h2.reshape(T, D), slots, n_rows)
    ys = _expert_call(block_expert, n_valid, xs, prm)
    y4 = _sc_gather_rows(ys, slots, T)
    y = _combine_call(x_mid.reshape(T, D), gates, final_norm.reshape(1, D), y4, min(256, S))
    return y.reshape(x.shape)


def kernel(x_prompt, x_sample, attn_norm, w_in, mla_q_norm, mla_w_uq, mla_kv_norm, mla_w_ukv,
           gla_w_gate_fwd, gla_b_gate_fwd, gla_w_gate_bwd, gla_b_gate_bwd, gla_out_norm, w_out, ffn_norm,
           router_w, router_b, expert_w_gate_up, expert_b_gate_up, expert_w_down, expert_b_down,
           final_norm):
    layer = (attn_norm, w_in, mla_q_norm, mla_w_uq, mla_kv_norm, mla_w_ukv, gla_w_gate_fwd,
             gla_b_gate_fwd, gla_w_gate_bwd, gla_b_gate_bwd, gla_out_norm, w_out, ffn_norm, router_w,
             router_b, expert_w_gate_up, expert_b_gate_up, expert_w_down, expert_b_down)
    assert all(p.shape[0] == 1 for p in layer), "single layer expected"
    assert x_prompt.shape[1:] == x_sample.shape[1:]
    prm = _prep_params(*[p[0] for p in layer])
    return _encoder(x_prompt, x_sample, prm, final_norm)
```

```python
import functools

import jax
import jax.numpy as jnp
import numpy as np
from jax import lax
from jax.experimental import pallas as pl
from jax.experimental.pallas import tpu as pltpu
from jax.experimental.pallas import tpu_sc as plsc

F32 = jnp.float32
BF16 = jnp.bfloat16

D_MODEL = 1024
MLA_HEADS = 8
MLA_NOPE = 64
MLA_ROPE = 32
MLA_V = 64
MLA_Q_LORA = 384
MLA_KV_LORA = 256
ROPE_THETA = 10000.0
GLA_HEADS = 4
GLA_DK = 64
GLA_DV = 128
GLA_GATE_RANK = 16
GLA_GATE_NORM = 16.0
N_EXPERTS = 32
TOP_K = 4
D_FF = 1024
SWIGLU_LIMIT = 7.0
SWIGLU_ALPHA = 1.702
MOE_BLOCK = 512
RMS_EPS = 1e-6

MLA_WIDTH = MLA_HEADS * MLA_V
GLA_WIDTH = GLA_HEADS * GLA_DV
GLA_KDIM = GLA_HEADS * GLA_DK
HALF_ROPE = MLA_ROPE // 2
QK_PAD = 128
LANES = 128
V_ONES_ROWS = 16

OFF_CQ = 0
OFF_CKV = OFF_CQ + MLA_Q_LORA
OFF_KR = OFF_CKV + MLA_KV_LORA
OFF_GQ = OFF_KR + LANES
OFF_GK = OFF_GQ + GLA_KDIM
OFF_GV = OFF_GK + GLA_KDIM
OFF_LR = OFF_GV + GLA_WIDTH
OFF_GR = OFF_LR + LANES
PROJ_COLS = OFF_GR + GLA_WIDTH

TOKEN_TILE = 512
ATTN_KEY_TILE = 256
ATTN_QUERY_TILE = 512
ATTN_LOOKAHEAD = 2
ATTN_SCORE_SLOTS = 4
GLA_TILE = 256
GLA_LEVELS = 8
SC_WINDOW = 32
SC_INDEX_PAD = 128
ATTN_TILES_PER_STEP = 2
PERM_GROUP = 256
VMEM_LIMIT = 56 * 1024 * 1024

LOG2_E = 1.4426950408889634
NT_DIMS = (((1,), (1,)), ((), ()))
TN_DIMS = (((0,), (0,)), ((), ()))


def _cparams(sem):
    return pltpu.CompilerParams(dimension_semantics=sem, vmem_limit_bytes=VMEM_LIMIT)


def _rms(x, gain):
    return x * lax.rsqrt(jnp.mean(x * x, axis=-1, keepdims=True) + RMS_EPS) * gain


def _split_bf16(x):
    hi = x.astype(BF16)
    lo = (x - hi.astype(F32)).astype(BF16)
    return hi, lo


def _pre_kernel(xa_ref, xb_ref, an_ref, win_ref, qn_ref, wuqT_ref, kvn_ref, wk_ref, wvT_ref,
                wgf_ref, bgf_ref, wgb_ref, bgb_ref, rc_ref, rs1_ref, rs2_ref, cosT_ref, sinT_ref,
                qT_ref, k_ref, vT_ref, gq_ref, gk_ref, gv_ref, laf_ref, lab_ref, gr_ref, *, n_first):
    x = jnp.where(pl.program_id(0) < n_first, xa_ref[...], xb_ref[...])
    h = _rms(x, an_ref[...]).astype(BF16)
    proj = jnp.dot(h, win_ref[...], preferred_element_type=F32)

    cqn = _rms(proj[:, OFF_CQ:OFF_CQ + MLA_Q_LORA], qn_ref[...]).astype(BF16)
    ckvn = _rms(proj[:, OFF_CKV:OFF_CKV + MLA_KV_LORA], kvn_ref[...]).astype(BF16)

    scale = (MLA_NOPE + MLA_ROPE) ** -0.5 * LOG2_E
    qT = lax.dot_general(wuqT_ref[...], cqn, NT_DIMS, preferred_element_type=F32) * scale
    n_nope = MLA_HEADS * MLA_NOPE
    n_half = MLA_HEADS * HALF_ROPE
    x1 = qT[n_nope:n_nope + n_half]
    x2 = qT[n_nope + n_half:n_nope + 2 * n_half]
    c = cosT_ref[...]
    s = sinT_ref[...]
    x1r = x1 * c - x2 * s
    x2r = x1 * s + x2 * c
    zpad = jnp.zeros((QK_PAD - MLA_NOPE - MLA_ROPE, qT.shape[1]), BF16)
    for hd in range(MLA_HEADS):
        qT_ref[hd, 0:MLA_NOPE, :] = qT[hd * MLA_NOPE:(hd + 1) * MLA_NOPE].astype(BF16)
        qT_ref[hd, MLA_NOPE:MLA_NOPE + HALF_ROPE, :] = x1r[hd * HALF_ROPE:(hd + 1) * HALF_ROPE].astype(BF16)
        qT_ref[hd, MLA_NOPE + HALF_ROPE:MLA_NOPE + MLA_ROPE, :] = (
            x2r[hd * HALF_ROPE:(hd + 1) * HALF_ROPE].astype(BF16))
        qT_ref[hd, MLA_NOPE + MLA_ROPE:QK_PAD, :] = zpad

    kr = proj[:, OFF_KR:OFF_KR + LANES]
    kr = (kr * rc_ref[...]
          + pltpu.roll(kr, LANES - HALF_ROPE, axis=1) * rs1_ref[...]
          + pltpu.roll(kr, HALF_ROPE, axis=1) * rs2_ref[...])
    kfull = jnp.dot(ckvn, wk_ref[...], preferred_element_type=F32)
    for hd in range(MLA_HEADS):
        k_ref[hd] = (kfull[:, hd * QK_PAD:(hd + 1) * QK_PAD] + kr).astype(BF16)

    vT = lax.dot_general(wvT_ref[...], ckvn, NT_DIMS, preferred_element_type=F32)
    for hd in range(MLA_HEADS):
        vT_ref[hd] = vT[hd * MLA_V:(hd + 1) * MLA_V].astype(BF16)

    gq_ref[...] = proj[:, OFF_GQ:OFF_GQ + GLA_KDIM] * (GLA_DK ** -0.5)
    gk_ref[...] = proj[:, OFF_GK:OFF_GK + GLA_KDIM]
    gv_ref[...] = proj[:, OFF_GV:OFF_GV + GLA_WIDTH]
    gr_ref[...] = proj[:, OFF_GR:OFF_GR + GLA_WIDTH]
    lr = proj[:, OFF_LR:OFF_LR + LANES].astype(BF16)

    def log_decay(w_ref, b_ref):
        z = jnp.dot(lr, w_ref[...], preferred_element_type=F32) + b_ref[...]
        return (jnp.minimum(z, 0.0) - jnp.log1p(jnp.exp(-jnp.abs(z)))) * (1.0 / GLA_GATE_NORM)

    laf_ref[...] = log_decay(wgf_ref, bgf_ref)
    lab_ref[...] = log_decay(wgb_ref, bgb_ref)


def _two_source_specs(n_first, tm, width):
    return [pl.BlockSpec((None, tm, width), lambda b, i: (jnp.minimum(b, n_first - 1), i, 0)),
            pl.BlockSpec((None, tm, width), lambda b, i: (jnp.maximum(b - n_first, 0), i, 0))]


def _pre_call(xa, xb, prm, rope, tm):
    n_first, S, D = xa.shape
    B = n_first + xb.shape[0]
    nS = S // tm
    H = MLA_HEADS

    def full(a):
        nd = a.ndim
        return pl.BlockSpec(a.shape, lambda b, i, _nd=nd: (0,) * _nd)

    tok = lambda w: pl.BlockSpec((None, tm, w), lambda b, i: (b, i, 0))
    in_specs = [
        *_two_source_specs(n_first, tm, D), full(prm["attn_norm"]), full(prm["w_in"]), full(prm["q_norm"]),
        full(prm["w_uqT"]), full(prm["kv_norm"]), full(prm["w_k"]), full(prm["w_vT"]),
        full(prm["w_gf"]), full(prm["b_gf"]), full(prm["w_gb"]), full(prm["b_gb"]),
        pl.BlockSpec((tm, LANES), lambda b, i: (i, 0)),
        pl.BlockSpec((tm, LANES), lambda b, i: (i, 0)),
        pl.BlockSpec((tm, LANES), lambda b, i: (i, 0)),
        pl.BlockSpec((H * HALF_ROPE, tm), lambda b, i: (0, i)),
        pl.BlockSpec((H * HALF_ROPE, tm), lambda b, i: (0, i)),
    ]
    out_shape = [
        jax.ShapeDtypeStruct((B, H, QK_PAD, S), BF16),
        jax.ShapeDtypeStruct((B, H, S, QK_PAD), BF16),
        jax.ShapeDtypeStruct((B, H, nS, MLA_V, tm), BF16),
        jax.ShapeDtypeStruct((B, S, GLA_KDIM), F32),
        jax.ShapeDtypeStruct((B, S, GLA_KDIM), F32),
        jax.ShapeDtypeStruct((B, S, GLA_WIDTH), F32),
        jax.ShapeDtypeStruct((B, S, GLA_KDIM), F32),
        jax.ShapeDtypeStruct((B, S, GLA_KDIM), F32),
        jax.ShapeDtypeStruct((B, S, GLA_WIDTH), F32),
    ]
    out_specs = [
        pl.BlockSpec((None, H, QK_PAD, tm), lambda b, i: (b, 0, 0, i)),
        pl.BlockSpec((None, H, tm, QK_PAD), lambda b, i: (b, 0, i, 0)),
        pl.BlockSpec((None, H, None, MLA_V, tm), lambda b, i: (b, 0, i, 0, 0)),
        tok(GLA_KDIM), tok(GLA_KDIM), tok(GLA_WIDTH), tok(GLA_KDIM), tok(GLA_KDIM), tok(GLA_WIDTH),
    ]
    return pl.pallas_call(
        functools.partial(_pre_kernel, n_first=n_first), grid=(B, nS),
        in_specs=in_specs, out_specs=out_specs, out_shape=out_shape,
        compiler_params=_cparams(("parallel", "parallel")), name="pre",
    )(xa, xb, prm["attn_norm"], prm["w_in"], prm["q_norm"], prm["w_uqT"], prm["kv_norm"], prm["w_k"],
      prm["w_vT"], prm["w_gf"], prm["b_gf"], prm["w_gb"], prm["b_gb"],
      rope["c"], rope["s1"], rope["s2"], rope["cosT"], rope["sinT"])


def _attn_kernel(qT_ref, k_ref, vT_ref, o_ref, s_ref, *, n_kblk, tiles):
    tq = qT_ref.shape[1] // tiles
    tk = k_ref.shape[0] // n_kblk
    per_tile = vT_ref.shape[2] // tk
    ones = jnp.ones((V_ONES_ROWS, tk), BF16)

    def values(j, p):
        v_blk = vT_ref[j // per_tile, :, (j % per_tile) * tk:(j % per_tile + 1) * tk]
        vext = jnp.concatenate([v_blk, ones], axis=0)
        return jnp.dot(vext, p, preferred_element_type=F32)

    for t in range(tiles):
        qT = qT_ref[:, t * tq:(t + 1) * tq]

        def scores(n):
            j = n % n_kblk
            sT = jnp.dot(k_ref[j * tk:(j + 1) * tk, :], qT, preferred_element_type=F32)
            s_ref[n % ATTN_SCORE_SLOTS] = sT
            return jnp.max(sT, axis=0, keepdims=True)

        first = t * n_kblk
        m = jnp.full((1, tq), -jnp.inf, F32)
        acc = jnp.zeros((MLA_V + V_ONES_ROWS, tq), F32)
        blk_max = {first + j: scores(first + j) for j in range(ATTN_LOOKAHEAD)}
        for n in range(first, first + n_kblk):
            if n + ATTN_LOOKAHEAD < first + n_kblk:
                blk_max[n + ATTN_LOOKAHEAD] = scores(n + ATTN_LOOKAHEAD)
            m_new = jnp.maximum(m, blk_max.pop(n))
            p = jnp.exp2(s_ref[n % ATTN_SCORE_SLOTS] - m_new).astype(BF16)
            acc = jnp.exp2(m - m_new) * acc + values(n % n_kblk, p)
            m = m_new
        o_ref[:, t * tq:(t + 1) * tq] = (acc[0:MLA_V] / acc[MLA_V:MLA_V + 1]).astype(o_ref.dtype)


def _attn_call(qT, k, vT, tq, tk):
    B, H, _, S = qT.shape
    n_vblk, v_tile = vT.shape[2], vT.shape[4]
    n_kblk = S // tk
    tiles = ATTN_TILES_PER_STEP if S % (tq * ATTN_TILES_PER_STEP) == 0 else 1
    assert n_kblk >= ATTN_LOOKAHEAD and S % (tq * tiles) == 0 and v_tile % tk == 0
    return pl.pallas_call(
        functools.partial(_attn_kernel, n_kblk=n_kblk, tiles=tiles),
        grid=(B, H, S // (tq * tiles)),
        in_specs=[
            pl.BlockSpec((None, None, QK_PAD, tq * tiles), lambda b, h, i: (b, h, 0, i)),
            pl.BlockSpec((None, None, S, QK_PAD), lambda b, h, i: (b, h, 0, 0)),
            pl.BlockSpec((None, None, n_vblk, MLA_V, v_tile), lambda b, h, i: (b, h, 0, 0, 0)),
        ],
        out_specs=pl.BlockSpec((None, None, MLA_V, tq * tiles), lambda b, h, i: (b, h, 0, i)),
        out_shape=jax.ShapeDtypeStruct((B, H, MLA_V, S), BF16),
        scratch_shapes=[pltpu.VMEM((ATTN_SCORE_SLOTS, tk, tq), F32)],
        compiler_params=_cparams(("parallel", "parallel", "parallel")), name="attn",
    )(qT, k, vT)


def _gla_consts():
    L = GLA_TILE
    i = np.arange(L)[:, None]
    j = np.arange(L)[None, :]
    x = i ^ j
    lidx = np.where(x > 0, np.floor(np.log2(np.maximum(x, 1))), -1).astype(np.int32)
    lidx_f = np.where(i > j, lidx, -1).astype(np.int32)
    lidx_b = np.where(i < j, lidx, -1).astype(np.int32)
    hd = np.arange(GLA_KDIM)[:, None] // GLA_DK
    hv = np.arange(GLA_WIDTH)[None, :] // GLA_DV
    bexp = (hd == hv).astype(np.float32)
    return jnp.asarray(lidx_f), jnp.asarray(lidx_b), jnp.asarray(bexp, BF16)


def _gla_direction(q, k, v, la, lidx, bexp_ref, ssum_ref, state_ref, o_ref, forward):
    L = GLA_TILE
    la_hi, la_lo = _split_bf16(la)
    vb = v.astype(BF16)
    lane_head = lax.broadcasted_iota(jnp.int32, (1, GLA_KDIM), 1) // GLA_DK
    head_masks = [(lane_head == h).astype(BF16) for h in range(GLA_HEADS)]
    row = lax.broadcasted_iota(jnp.int32, la.shape, 0)

    def widen(c, t, m):
        upper_half = (row & m) != 0
        sibling_total = jnp.where(upper_half, pltpu.roll(t, m, axis=0), pltpu.roll(t, L - m, axis=0))
        return c + jnp.where(upper_half, sibling_total, 0.0), t + sibling_total

    c, t = la, la
    for lvl in range(GLA_LEVELS):
        if lvl > 0:
            c, t = widen(c, t, 1 << (lvl - 1))
        if forward:
            eq, ek = c, t - c
        else:
            eq, ek = t - c + la, c - la
        ql = (q * jnp.exp(eq)).astype(BF16)
        kl = (k * jnp.exp(ek)).astype(BF16)
        sel = lidx == lvl
        for h in range(GLA_HEADS):
            sc = lax.dot_general(ql * head_masks[h], kl, NT_DIMS, preferred_element_type=F32)
            ssum_ref[h] = jnp.where(sel, sc, ssum_ref[h] if lvl > 0 else 0.0)

    c, t = widen(c, t, L // 2)
    if forward:
        eq, ek = c, t - c
    else:
        eq, ek = t - c + la, c - la
    q_in = (q * jnp.exp(eq)).astype(BF16)
    k_out = (k * jnp.exp(ek)).astype(BF16)
    o = jnp.dot(q_in, state_ref[...].astype(BF16), preferred_element_type=F32)
    if forward:
        o = o + jnp.dot((q * k).astype(BF16), bexp_ref[...], preferred_element_type=F32) * v
    for h in range(GLA_HEADS):
        oh = jnp.dot(ssum_ref[h].astype(BF16), vb[:, h * GLA_DV:(h + 1) * GLA_DV],
                     preferred_element_type=F32)
        o_ref[:, h * GLA_DV:(h + 1) * GLA_DV] = o[:, h * GLA_DV:(h + 1) * GLA_DV] + oh

    ones = jnp.ones((L, LANES), BF16)
    tot_col = (lax.dot_general(la_hi, ones, TN_DIMS, preferred_element_type=F32)
               + lax.dot_general(la_lo, ones, TN_DIMS, preferred_element_type=F32))
    dec = jnp.exp(tot_col)
    upd = lax.dot_general(k_out, vb, TN_DIMS, preferred_element_type=F32)
    for h in range(GLA_HEADS):
        cols = slice(h * GLA_DV, (h + 1) * GLA_DV)
        state_ref[:, cols] = dec * state_ref[:, cols] + upd[:, cols] * bexp_ref[:, cols].astype(F32)


def _gla_kernel(qf_ref, kf_ref, vf_ref, laf_ref, qb_ref, kb_ref, vb_ref, lab_ref,
                lidxf_ref, lidxb_ref, bexp_ref, of_ref, ob_ref, sf_ref, sb_ref, ssum_ref):
    @pl.when(pl.program_id(1) == 0)
    def _():
        sf_ref[...] = jnp.zeros_like(sf_ref)
        sb_ref[...] = jnp.zeros_like(sb_ref)

    _gla_direction(qf_ref[...], kf_ref[...], vf_ref[...], laf_ref[...], lidxf_ref[...],
                   bexp_ref, ssum_ref, sf_ref, of_ref, True)
    _gla_direction(qb_ref[...], kb_ref[...], vb_ref[...], lab_ref[...], lidxb_ref[...],
                   bexp_ref, ssum_ref, sb_ref, ob_ref, False)


def _gla_call(gq, gk, gv, laf, lab):
    B, S, _ = gq.shape
    L = GLA_TILE
    n = S // L
    lidx_f, lidx_b, bexp = _gla_consts()
    fwd = lambda w: pl.BlockSpec((None, L, w), lambda b, i: (b, i, 0))
    bwd = lambda w: pl.BlockSpec((None, L, w), lambda b, i: (b, n - 1 - i, 0))
    const = lambda a: pl.BlockSpec(a.shape, lambda b, i, _nd=a.ndim: (0,) * _nd)
    return pl.pallas_call(
        _gla_kernel, grid=(B, n),
        in_specs=[fwd(GLA_KDIM), fwd(GLA_KDIM), fwd(GLA_WIDTH), fwd(GLA_KDIM),
                  bwd(GLA_KDIM), bwd(GLA_KDIM), bwd(GLA_WIDTH), bwd(GLA_KDIM),
                  const(lidx_f), const(lidx_b), const(bexp)],
        out_specs=[fwd(GLA_WIDTH), bwd(GLA_WIDTH)],
        out_shape=[jax.ShapeDtypeStruct((B, S, GLA_WIDTH), F32)] * 2,
        scratch_shapes=[pltpu.VMEM((GLA_KDIM, GLA_WIDTH), F32), pltpu.VMEM((GLA_KDIM, GLA_WIDTH), F32),
                        pltpu.VMEM((GLA_HEADS, L, L), F32)],
        compiler_params=_cparams(("parallel", "arbitrary")), name="gla",
    )(gq, gk, gv, laf, gq, gk, gv, lab, lidx_f, lidx_b, bexp)


def _post_kernel(x_ref, oT_ref, of_ref, ob_ref, gr_ref, gon_ref, woa_ref, wob_ref, fn_ref,
                 rwT_ref, rb_ref, upper_ref,
                 xmid_ref, h2_ref, meta_ref, gates_ref, counts_ref, carry_ref):
    first = (pl.program_id(0) == 0) & (pl.program_id(1) == 0)

    @pl.when(first)
    def _():
        carry_ref[...] = jnp.zeros_like(carry_ref)

    o = of_ref[...] + ob_ref[...]
    gr = gr_ref[...]
    parts = []
    for h in range(GLA_HEADS):
        cols = slice(h * GLA_DV, (h + 1) * GLA_DV)
        parts.append(_rms(o[:, cols], gon_ref[...]) * jax.nn.silu(gr[:, cols]))
    gla = jnp.concatenate(parts, axis=1).astype(BF16)

    x_mid = (x_ref[...]
             + lax.dot_general(oT_ref[...], woa_ref[...], TN_DIMS, preferred_element_type=F32)
             + jnp.dot(gla, wob_ref[...], preferred_element_type=F32))
    xmid_ref[...] = x_mid
    h2 = _rms(x_mid, fn_ref[...])
    h2_ref[...] = h2

    h_hi, h_lo = _split_bf16(h2)
    w_hi, w_lo = _split_bf16(rwT_ref[...])
    by_h_hi = lax.dot_general(jnp.concatenate([w_hi, w_lo], axis=0), h_hi, NT_DIMS,
                              preferred_element_type=F32)
    logits = (by_h_hi[:N_EXPERTS] + by_h_hi[N_EXPERTS:]
              + lax.dot_general(w_hi, h_lo, NT_DIMS, preferred_element_type=F32)
              + rb_ref[...])
    tm = logits.shape[1]
    eidx = lax.broadcasted_iota(jnp.int32, (N_EXPERTS, tm), 0).astype(F32)
    vals, idxs, sels = [], [], []
    cur = logits
    for _ in range(TOP_K):
        mk = jnp.max(cur, axis=0, keepdims=True)
        ik = jnp.min(jnp.where(cur == mk, eidx, float(N_EXPERTS)), axis=0, keepdims=True)
        sel = eidx == ik
        vals.append(mk)
        idxs.append(ik)
        sels.append(sel)
        cur = jnp.where(sel, -jnp.inf, cur)
    exps = [jnp.exp(vk - vals[0]) for vk in vals]
    denom = exps[0] + exps[1] + exps[2] + exps[3]
    gates = [e / denom for e in exps]

    cnt = (sels[0] | sels[1] | sels[2] | sels[3])
    before = jnp.dot(cnt.astype(BF16), upper_ref[...], preferred_element_type=F32) + carry_ref[:, 0:1]
    ranks = [jnp.sum(jnp.where(sel, before, 0.0), axis=0, keepdims=True) for sel in sels]
    carry_ref[...] = carry_ref[...] + jnp.sum(cnt.astype(F32), axis=1, keepdims=True)
    counts_ref[...] = carry_ref[...]

    meta_ref[...] = jnp.concatenate(idxs + ranks, axis=0).astype(jnp.int32)
    gates_ref[...] = jnp.concatenate(gates + [jnp.zeros((TOP_K, tm), F32)], axis=0)


def _post_call(x, batch0, oT, o_f, o_b, gr, prm, tm):
    B, S, D = x.shape
    nS = S // tm
    upper = jnp.asarray(np.triu(np.ones((tm, tm), np.float32), k=1), BF16)
    full = lambda a: pl.BlockSpec(a.shape, lambda b, i, _nd=a.ndim: (0,) * _nd)
    own = lambda w: pl.BlockSpec((None, tm, w), lambda b, i: (b, i, 0))
    tok = lambda w: pl.BlockSpec((None, tm, w), lambda b, i: (batch0 + b, i, 0))
    colblk = lambda r: pl.BlockSpec((None, r, tm), lambda b, i: (batch0 + b, 0, i))
    flat = lambda r: pl.BlockSpec((r, tm), lambda b, i: (0, b * nS + i))
    return pl.pallas_call(
        _post_kernel, grid=(B, nS),
        in_specs=[own(D), colblk(MLA_WIDTH), tok(GLA_WIDTH), tok(GLA_WIDTH), tok(GLA_WIDTH),
                  full(prm["gla_out_norm"]), full(prm["w_out_a"]), full(prm["w_out_b"]),
                  full(prm["ffn_norm"]), full(prm["router_wT"]), full(prm["router_b"]), full(upper)],
        out_specs=[own(D), own(D), flat(2 * TOP_K), flat(2 * TOP_K),
                   pl.BlockSpec((N_EXPERTS, LANES), lambda b, i: (0, 0))],
        out_shape=[jax.ShapeDtypeStruct((B, S, D), F32), jax.ShapeDtypeStruct((B, S, D), F32),
                   jax.ShapeDtypeStruct((2 * TOP_K, B * S), jnp.int32),
                   jax.ShapeDtypeStruct((2 * TOP_K, B * S), F32),
                   jax.ShapeDtypeStruct((N_EXPERTS, LANES), F32)],
        scratch_shapes=[pltpu.VMEM((N_EXPERTS, LANES), F32)],
        compiler_params=_cparams(("arbitrary", "arbitrary")), name="post",
    )(x, oT, o_f, o_b, gr, prm["gla_out_norm"], prm["w_out_a"], prm["w_out_b"], prm["ffn_norm"],
      prm["router_wT"], prm["router_b"], upper)


def _sc_mesh():
    return plsc.VectorSubcoreMesh(core_axis_name="core", subcore_axis_name="subcore")


def _sc_scatter_rows(x, slots, n_rows):
    T, D = x.shape
    K = slots.shape[0]

    @pl.kernel(out_type=jax.ShapeDtypeStruct((n_rows, D), x.dtype), mesh=_sc_mesh(), scratch_types=[])
    def scatter(x_hbm, slots_hbm, out_hbm):
        def window(x_vmem, slots_vmem):
            for kk in range(K):
                pltpu.sync_copy(x_vmem, out_hbm.at[slots_vmem.at[kk, 0, pl.ds(0, SC_WINDOW)]])

        pltpu.emit_pipeline(
            window, grid=(T // SC_WINDOW,),
            in_specs=[pl.BlockSpec((SC_WINDOW, D), lambda i: (i, 0)),
                      pl.BlockSpec((K, 1, SC_INDEX_PAD), lambda i: (0, i, 0))],
            out_specs=[], core_axis_name=("core", "subcore"),
            dimension_semantics=(pltpu.PARALLEL,))(x_hbm, slots_hbm)

    return scatter(x, slots)


def _sc_gather_rows(y, slots, T):
    D = y.shape[1]
    K = slots.shape[0]

    @pl.kernel(out_type=jax.ShapeDtypeStruct((K, T, D), y.dtype), mesh=_sc_mesh(), scratch_types=[])
    def gather(y_hbm, slots_hbm, out_hbm):
        def window(slots_vmem, out_vmem):
            pltpu.sync_copy(y_hbm.at[slots_vmem.at[0, 0, pl.ds(0, SC_WINDOW)]], out_vmem.at[0])

        pltpu.emit_pipeline(
            window, grid=(K, T // SC_WINDOW),
            in_specs=[pl.BlockSpec((1, 1, SC_INDEX_PAD), lambda kk, i: (kk, i, 0))],
            out_specs=[pl.BlockSpec((1, SC_WINDOW, D), lambda kk, i: (kk, i, 0))],
            core_axis_name=("core", "subcore"),
            dimension_semantics=(pltpu.PARALLEL, pltpu.PARALLEL))(slots_hbm, out_hbm)

    return gather(y, slots)


def _expert_weights_kernel(wgu_ref, wdn_ref, perm_ref, wg_ref, wl_ref, wd_ref):
    perm = perm_ref[...]
    half = PERM_GROUP // 2
    for c in range(2 * D_FF // PERM_GROUP):
        w = wgu_ref[:, c * PERM_GROUP:(c + 1) * PERM_GROUP].astype(BF16)
        sep = jnp.dot(w, perm, preferred_element_type=F32).astype(BF16)
        wg_ref[:, c * half:(c + 1) * half] = sep[:, :half]
        wl_ref[:, c * half:(c + 1) * half] = sep[:, half:]
    wd_ref[...] = wdn_ref[...].astype(BF16)


def _expert_weights_call(w_gu, w_dn):
    E, D, _ = w_gu.shape
    half = PERM_GROUP // 2
    src = np.concatenate([2 * np.arange(half), 2 * np.arange(half) + 1])
    perm = jnp.asarray(np.arange(PERM_GROUP)[:, None] == src[None, :], BF16)
    per_expert = lambda r, c: pl.BlockSpec((None, r, c), lambda e: (e, 0, 0))
    return pl.pallas_call(
        _expert_weights_kernel, grid=(E,),
        in_specs=[per_expert(D, 2 * D_FF), per_expert(D_FF, D),
                  pl.BlockSpec((PERM_GROUP, PERM_GROUP), lambda e: (0, 0))],
        out_specs=[per_expert(D, D_FF), per_expert(D, D_FF), per_expert(D_FF, D)],
        out_shape=[jax.ShapeDtypeStruct((E, D, D_FF), BF16), jax.ShapeDtypeStruct((E, D, D_FF), BF16),
                   jax.ShapeDtypeStruct((E, D_FF, D), BF16)],
        compiler_params=_cparams(("parallel",)), name="expert_weights",
    )(w_gu, w_dn, perm)


def _expert_kernel(be_ref, nv_ref, xs_ref, wg_ref, wl_ref, bg_ref, bl_ref, wd_ref, bd_ref, ys_ref):
    del be_ref
    valid = pl.program_id(0) < nv_ref[0]

    @pl.when(valid)
    def _():
        xb = xs_ref[...].astype(BF16)
        g = jnp.dot(xb, wg_ref[...], preferred_element_type=F32) + bg_ref[...]
        l = jnp.dot(xb, wl_ref[...], preferred_element_type=F32) + bl_ref[...]
        glu = jnp.minimum(g, SWIGLU_LIMIT)
        lin = jnp.clip(l, -SWIGLU_LIMIT, SWIGLU_LIMIT)
        act = glu * jax.nn.sigmoid(SWIGLU_ALPHA * glu) * (lin + 1.0)
        ys_ref[...] = jnp.dot(act.astype(BF16), wd_ref[...], preferred_element_type=F32) + bd_ref[...]

    @pl.when(jnp.logical_not(valid))
    def _():
        ys_ref[...] = jnp.zeros_like(ys_ref)


def _expert_call(block_expert, n_valid, xs, prm):
    n_rows, D = xs.shape
    n_blocks = n_rows // MOE_BLOCK
    rows = lambda i, be, nv: (jnp.minimum(i, nv[0] - 1), 0)
    wsel = lambda i, be, nv: (be[i], 0, 0)
    grid_spec = pltpu.PrefetchScalarGridSpec(
        num_scalar_prefetch=2, grid=(n_blocks,),
        in_specs=[pl.BlockSpec((MOE_BLOCK, D), rows),
                  pl.BlockSpec((None, D, D_FF), wsel), pl.BlockSpec((None, D, D_FF), wsel),
                  pl.BlockSpec((None, 1, D_FF), wsel), pl.BlockSpec((None, 1, D_FF), wsel),
                  pl.BlockSpec((None, D_FF, D), wsel), pl.BlockSpec((None, 1, D), wsel)],
        out_specs=pl.BlockSpec((MOE_BLOCK, D), lambda i, be, nv: (i, 0)))
    return pl.pallas_call(
        _expert_kernel, grid_spec=grid_spec,
        out_shape=jax.ShapeDtypeStruct((n_rows, D), F32),
        compiler_params=_cparams(("arbitrary",)), name="experts",
    )(block_expert, n_valid, xs, prm["w_glu"], prm["w_lin"], prm["b_glu"], prm["b_lin"],
      prm["w_dn"], prm["b_dn"])


def _combine_kernel(xmid_ref, gates_ref, fnorm_ref, y4_ref, out_ref):
    g = jnp.transpose(gates_ref[...])
    y = xmid_ref[...]
    for kk in range(TOP_K):
        y = y + y4_ref[kk] * g[:, kk:kk + 1]
    out_ref[...] = _rms(y, fnorm_ref[...])


def _combine_call(x_mid, gates, final_norm, y4, tc):
    T, D = x_mid.shape
    return pl.pallas_call(
        _combine_kernel, grid=(T // tc,),
        in_specs=[pl.BlockSpec((tc, D), lambda i: (i, 0)),
                  pl.BlockSpec((2 * TOP_K, tc), lambda i: (0, i)),
                  pl.BlockSpec(final_norm.shape, lambda i: (0, 0)),
                  pl.BlockSpec((TOP_K, tc, D), lambda i: (0, i, 0))],
        out_specs=pl.BlockSpec((tc, D), lambda i: (i, 0)),
        out_shape=jax.ShapeDtypeStruct((T, D), F32),
        compiler_params=_cparams(("parallel",)), name="combine",
    )(x_mid, gates, final_norm, y4)


def _prep_params(attn_norm, w_in, mla_q_norm, mla_w_uq, mla_kv_norm, mla_w_ukv, gla_w_gate_fwd,
                 gla_b_gate_fwd, gla_w_gate_bwd, gla_b_gate_bwd, gla_out_norm, w_out, ffn_norm,
                 router_w, router_b, w_gu, b_gu, w_dn, b_dn):
    D = D_MODEL
    o = np.cumsum((0, MLA_Q_LORA, MLA_KV_LORA, MLA_ROPE, GLA_KDIM, GLA_KDIM, GLA_WIDTH,
                   2 * GLA_GATE_RANK, GLA_WIDTH))
    seg = [w_in[:, o[n]:o[n + 1]] for n in range(8)]
    z = lambda n: jnp.zeros((D, n), w_in.dtype)
    w_in_p = jnp.concatenate(
        [seg[0], seg[1], z(MLA_NOPE), seg[2], z(LANES - MLA_NOPE - MLA_ROPE), seg[3], seg[4], seg[5],
         seg[6], z(LANES - 2 * GLA_GATE_RANK), seg[7]], axis=1).astype(BF16)

    wq = mla_w_uq.reshape(MLA_Q_LORA, MLA_HEADS, MLA_NOPE + MLA_ROPE)
    w_uq_p = jnp.concatenate(
        [wq[:, :, :MLA_NOPE].reshape(MLA_Q_LORA, -1),
         wq[:, :, MLA_NOPE:MLA_NOPE + HALF_ROPE].reshape(MLA_Q_LORA, -1),
         wq[:, :, MLA_NOPE + HALF_ROPE:].reshape(MLA_Q_LORA, -1)], axis=1)
    wkv = mla_w_ukv.reshape(MLA_KV_LORA, MLA_HEADS, MLA_NOPE + MLA_V)
    w_k = jnp.concatenate([wkv[:, :, :MLA_NOPE], jnp.zeros((MLA_KV_LORA, MLA_HEADS, QK_PAD - MLA_NOPE),
                                                            wkv.dtype)], axis=2)
    w_v = wkv[:, :, MLA_NOPE:].reshape(MLA_KV_LORA, -1)

    def gate_w(w, row0):
        full = jnp.zeros((LANES, GLA_KDIM), w.dtype)
        return full.at[row0:row0 + GLA_GATE_RANK].set(w).astype(BF16)

    E = N_EXPERTS
    w_glu, w_lin, w_dn_b = _expert_weights_call(w_gu, w_dn)
    return {
        "attn_norm": attn_norm.reshape(1, D), "w_in": w_in_p,
        "q_norm": mla_q_norm.reshape(1, -1), "w_uqT": w_uq_p.T.astype(BF16),
        "kv_norm": mla_kv_norm.reshape(1, -1),
        "w_k": w_k.reshape(MLA_KV_LORA, -1).astype(BF16), "w_vT": w_v.T.astype(BF16),
        "w_gf": gate_w(gla_w_gate_fwd, 0), "b_gf": gla_b_gate_fwd.reshape(1, -1),
        "w_gb": gate_w(gla_w_gate_bwd, GLA_GATE_RANK), "b_gb": gla_b_gate_bwd.reshape(1, -1),
        "gla_out_norm": gla_out_norm.reshape(1, -1),
        "w_out_a": w_out[:MLA_WIDTH].astype(BF16), "w_out_b": w_out[MLA_WIDTH:].astype(BF16),
        "ffn_norm": ffn_norm.reshape(1, D),
        "router_wT": router_w.T, "router_b": router_b.reshape(E, 1),
        "w_glu": w_glu, "w_lin": w_lin, "w_dn": w_dn_b,
        "b_glu": b_gu[:, 0::2].reshape(E, 1, D_FF), "b_lin": b_gu[:, 1::2].reshape(E, 1, D_FF),
        "b_dn": b_dn.reshape(E, 1, D),
    }


def _rope_tables(S):
    inv_freq = jnp.power(ROPE_THETA, -jnp.arange(0, MLA_ROPE, 2, dtype=F32) / MLA_ROPE)
    ang = jnp.arange(S, dtype=F32)[:, None] * inv_freq[None, :]
    cos, sin = jnp.cos(ang), jnp.sin(ang)
    z = lambda n: jnp.zeros((S, n), F32)
    tail = LANES - MLA_NOPE - MLA_ROPE
    return {
        "c": jnp.concatenate([z(MLA_NOPE), cos, cos, z(tail)], axis=1),
        "s1": jnp.concatenate([z(MLA_NOPE), -sin, z(HALF_ROPE), z(tail)], axis=1),
        "s2": jnp.concatenate([z(MLA_NOPE), z(HALF_ROPE), sin, z(tail)], axis=1),
        "cosT": jnp.tile(cos.T, (MLA_HEADS, 1)), "sinT": jnp.tile(sin.T, (MLA_HEADS, 1)),
    }


def _encoder(xa, xb, prm, final_norm):
    n_first, S, D = xa.shape
    B = n_first + xb.shape[0]
    rope = _rope_tables(S)
    qT, k, vT, gq, gk, gv, laf, lab, gr = _pre_call(xa, xb, prm, rope, min(TOKEN_TILE, S))
    oT = _attn_call(qT, k, vT, min(ATTN_QUERY_TILE, S), ATTN_KEY_TILE).reshape(B, MLA_WIDTH, S)
    o_f, o_b = _gla_call(gq, gk, gv, laf, lab)
    mixed = (oT, o_f, o_b, gr)
    return (_moe_block(xa, 0, mixed, prm, final_norm), _moe_block(xb, n_first, mixed, prm, final_norm))


def _moe_block(x, batch0, mixed, prm, final_norm):
    B, S, D = x.shape
    x_mid, h2, meta, gates, counts = _post_call(x, batch0, *mixed, prm, min(TOKEN_TILE, S))

    T = B * S
    n_rows = T * TOP_K + N_EXPERTS * MOE_BLOCK
    n_blocks = n_rows // MOE_BLOCK
    cnt = counts[:, 0].astype(jnp.int32)
    padded = ((cnt + MOE_BLOCK - 1) // MOE_BLOCK) * MOE_BLOCK
    pends = jnp.cumsum(padded)
    starts = (pends - padded).astype(jnp.int32)
    n_valid = (pends[-1] // MOE_BLOCK).astype(jnp.int32).reshape(1)
    blk = jnp.minimum(jnp.arange(n_blocks, dtype=jnp.int32), n_valid[0] - 1) * MOE_BLOCK
    block_expert = jnp.minimum(jnp.sum(pends[None, :] <= blk[:, None], axis=1), N_EXPERTS - 1).astype(jnp.int32)

    experts = jnp.arange(N_EXPERTS, dtype=jnp.int32)[:, None, None]
    slots = jnp.sum(jnp.where(meta[None, :TOP_K] == experts, starts[:, None, None], 0), axis=0) + meta[TOP_K:]
    slots = jnp.pad(slots.reshape(TOP_K, T // SC_WINDOW, SC_WINDOW),
                    ((0, 0), (0, 0), (0, SC_INDEX_PAD - SC_WINDOW)))

    xs = _sc_scatter_rows(h2.reshape(T, D), slots, n_rows)
    ys = _expert_call(block_expert, n_valid, xs, prm)
    y4 = _sc_gather_rows(ys, slots, T)
    y = _combine_call(x_mid.reshape(T, D), gates, final_norm.reshape(1, D), y4, min(256, S))
    return y.reshape(x.shape)


def kernel(x_prompt, x_sample, attn_norm, w_in, mla_q_norm, mla_w_uq, mla_kv_norm, mla_w_ukv,
           gla_w_gate_fwd, gla_b_gate_fwd, gla_w_gate_bwd, gla_b_gate_bwd, gla_out_norm, w_out, ffn_norm,
           router_w, router_b, expert_w_gate_up, expert_b_gate_up, expert_w_down, expert_b_down,
           final_norm):
    layer = (attn_norm, w_in, mla_q_norm, mla_w_uq, mla_kv_norm, mla_w_ukv, gla_w_gate_fwd,
             gla_b_gate_fwd, gla_w_gate_bwd, gla_b_gate_bwd, gla_out_norm, w_out, ffn_norm, router_w,
             router_b, expert_w_gate_up, expert_b_gate_up, expert_w_down, expert_b_down)
    assert all(p.shape[0] == 1 for p in layer), "single layer expected"
    assert x_prompt.shape[1:] == x_sample.shape[1:]
    prm = _prep_params(*[p[0] for p in layer])
    return _encoder(x_prompt, x_sample, prm, final_norm)
```

```python
import functools

import jax
import jax.numpy as jnp
import numpy as np
from jax import lax
from jax.experimental import pallas as pl
from jax.experimental.pallas import tpu as pltpu
from jax.experimental.pallas import tpu_sc as plsc

F32 = jnp.float32
BF16 = jnp.bfloat16

D_MODEL = 1024
MLA_HEADS = 8
MLA_NOPE = 64
MLA_ROPE = 32
MLA_V = 64
MLA_Q_LORA = 384
MLA_KV_LORA = 256
ROPE_THETA = 10000.0
GLA_HEADS = 4
GLA_DK = 64
GLA_DV = 128
GLA_GATE_RANK = 16
GLA_GATE_NORM = 16.0
N_EXPERTS = 32
TOP_K = 4
D_FF = 1024
SWIGLU_LIMIT = 7.0
SWIGLU_ALPHA = 1.702
MOE_BLOCK = 512
RMS_EPS = 1e-6

MLA_WIDTH = MLA_HEADS * MLA_V
GLA_WIDTH = GLA_HEADS * GLA_DV
GLA_KDIM = GLA_HEADS * GLA_DK
HALF_ROPE = MLA_ROPE // 2
QK_PAD = 128
LANES = 128
V_ONES_ROWS = 16

OFF_CQ = 0
OFF_CKV = OFF_CQ + MLA_Q_LORA
OFF_KR = OFF_CKV + MLA_KV_LORA
OFF_GQ = OFF_KR + LANES
OFF_GK = OFF_GQ + GLA_KDIM
OFF_GV = OFF_GK + GLA_KDIM
OFF_LR = OFF_GV + GLA_WIDTH
OFF_GR = OFF_LR + LANES
PROJ_COLS = OFF_GR + GLA_WIDTH

TOKEN_TILE = 512
ATTN_KEY_TILE = 256
ATTN_QUERY_TILE = 512
ATTN_LOOKAHEAD = 2
ATTN_SCORE_SLOTS = 4
GLA_TILE = 256
GLA_LEVELS = 8
SC_WINDOW = 32
SC_INDEX_PAD = 128
ATTN_TILES_PER_STEP = 2
PERM_GROUP = 256
VMEM_LIMIT = 56 * 1024 * 1024

LOG2_E = 1.4426950408889634
NT_DIMS = (((1,), (1,)), ((), ()))
TN_DIMS = (((0,), (0,)), ((), ()))


def _cparams(sem):
    return pltpu.CompilerParams(dimension_semantics=sem, vmem_limit_bytes=VMEM_LIMIT)


def _rms(x, gain):
    return x * lax.rsqrt(jnp.mean(x * x, axis=-1, keepdims=True) + RMS_EPS) * gain


def _split_bf16(x):
    hi = x.astype(BF16)
    lo = (x - hi.astype(F32)).astype(BF16)
    return hi, lo


def _pre_kernel(xa_ref, xb_ref, an_ref, win_ref, qn_ref, wuqT_ref, kvn_ref, wk_ref, wvT_ref,
                wgf_ref, bgf_ref, wgb_ref, bgb_ref, rc_ref, rs1_ref, rs2_ref, cosT_ref, sinT_ref,
                qT_ref, k_ref, vT_ref, gq_ref, gk_ref, gv_ref, laf_ref, lab_ref, gr_ref, *, n_first):
    x = jnp.where(pl.program_id(0) < n_first, xa_ref[...], xb_ref[...])
    h = _rms(x, an_ref[...]).astype(BF16)
    proj = jnp.dot(h, win_ref[...], preferred_element_type=F32)

    cqn = _rms(proj[:, OFF_CQ:OFF_CQ + MLA_Q_LORA], qn_ref[...]).astype(BF16)
    ckvn = _rms(proj[:, OFF_CKV:OFF_CKV + MLA_KV_LORA], kvn_ref[...]).astype(BF16)

    scale = (MLA_NOPE + MLA_ROPE) ** -0.5 * LOG2_E
    qT = lax.dot_general(wuqT_ref[...], cqn, NT_DIMS, preferred_element_type=F32) * scale
    n_nope = MLA_HEADS * MLA_NOPE
    n_half = MLA_HEADS * HALF_ROPE
    x1 = qT[n_nope:n_nope + n_half]
    x2 = qT[n_nope + n_half:n_nope + 2 * n_half]
    c = cosT_ref[...]
    s = sinT_ref[...]
    x1r = x1 * c - x2 * s
    x2r = x1 * s + x2 * c
    zpad = jnp.zeros((QK_PAD - MLA_NOPE - MLA_ROPE, qT.shape[1]), BF16)
    for hd in range(MLA_HEADS):
        qT_ref[hd, 0:MLA_NOPE, :] = qT[hd * MLA_NOPE:(hd + 1) * MLA_NOPE].astype(BF16)
        qT_ref[hd, MLA_NOPE:MLA_NOPE + HALF_ROPE, :] = x1r[hd * HALF_ROPE:(hd + 1) * HALF_ROPE].astype(BF16)
        qT_ref[hd, MLA_NOPE + HALF_ROPE:MLA_NOPE + MLA_ROPE, :] = (
            x2r[hd * HALF_ROPE:(hd + 1) * HALF_ROPE].astype(BF16))
        qT_ref[hd, MLA_NOPE + MLA_ROPE:QK_PAD, :] = zpad

    kr = proj[:, OFF_KR:OFF_KR + LANES]
    kr = (kr * rc_ref[...]
          + pltpu.roll(kr, LANES - HALF_ROPE, axis=1) * rs1_ref[...]
          + pltpu.roll(kr, HALF_ROPE, axis=1) * rs2_ref[...])
    kfull = jnp.dot(ckvn, wk_ref[...], preferred_element_type=F32)
    for hd in range(MLA_HEADS):
        k_ref[hd] = (kfull[:, hd * QK_PAD:(hd + 1) * QK_PAD] + kr).astype(BF16)

    vT = lax.dot_general(wvT_ref[...], ckvn, NT_DIMS, preferred_element_type=F32)
    for hd in range(MLA_HEADS):
        vT_ref[hd] = vT[hd * MLA_V:(hd + 1) * MLA_V].astype(BF16)

    gq_ref[...] = proj[:, OFF_GQ:OFF_GQ + GLA_KDIM] * (GLA_DK ** -0.5)
    gk_ref[...] = proj[:, OFF_GK:OFF_GK + GLA_KDIM]
    gv_ref[...] = proj[:, OFF_GV:OFF_GV + GLA_WIDTH]
    gr_ref[...] = proj[:, OFF_GR:OFF_GR + GLA_WIDTH]
    lr = proj[:, OFF_LR:OFF_LR + LANES].astype(BF16)

    def log_decay(w_ref, b_ref):
        z = jnp.dot(lr, w_ref[...], preferred_element_type=F32) + b_ref[...]
        return (jnp.minimum(z, 0.0) - jnp.log1p(jnp.exp(-jnp.abs(z)))) * (LOG2_E / GLA_GATE_NORM)

    laf_ref[...] = log_decay(wgf_ref, bgf_ref)
    lab_ref[...] = log_decay(wgb_ref, bgb_ref)


def _two_source_specs(n_first, tm, width):
    return [pl.BlockSpec((None, tm, width), lambda b, i: (jnp.minimum(b, n_first - 1), i, 0)),
            pl.BlockSpec((None, tm, width), lambda b, i: (jnp.maximum(b - n_first, 0), i, 0))]


def _pre_call(xa, xb, prm, rope, tm):
    n_first, S, D = xa.shape
    B = n_first + xb.shape[0]
    nS = S // tm
    H = MLA_HEADS

    def full(a):
        nd = a.ndim
        return pl.BlockSpec(a.shape, lambda b, i, _nd=nd: (0,) * _nd)

    tok = lambda w: pl.BlockSpec((None, tm, w), lambda b, i: (b, i, 0))
    in_specs = [
        *_two_source_specs(n_first, tm, D), full(prm["attn_norm"]), full(prm["w_in"]), full(prm["q_norm"]),
        full(prm["w_uqT"]), full(prm["kv_norm"]), full(prm["w_k"]), full(prm["w_vT"]),
        full(prm["w_gf"]), full(prm["b_gf"]), full(prm["w_gb"]), full(prm["b_gb"]),
        pl.BlockSpec((tm, LANES), lambda b, i: (i, 0)),
        pl.BlockSpec((tm, LANES), lambda b, i: (i, 0)),
        pl.BlockSpec((tm, LANES), lambda b, i: (i, 0)),
        pl.BlockSpec((H * HALF_ROPE, tm), lambda b, i: (0, i)),
        pl.BlockSpec((H * HALF_ROPE, tm), lambda b, i: (0, i)),
    ]
    out_shape = [
        jax.ShapeDtypeStruct((B, H, QK_PAD, S), BF16),
        jax.ShapeDtypeStruct((B, H, S, QK_PAD), BF16),
        jax.ShapeDtypeStruct((B, H, nS, MLA_V, tm), BF16),
        jax.ShapeDtypeStruct((B, S, GLA_KDIM), F32),
        jax.ShapeDtypeStruct((B, S, GLA_KDIM), F32),
        jax.ShapeDtypeStruct((B, S, GLA_WIDTH), F32),
        jax.ShapeDtypeStruct((B, S, GLA_KDIM), F32),
        jax.ShapeDtypeStruct((B, S, GLA_KDIM), F32),
        jax.ShapeDtypeStruct((B, S, GLA_WIDTH), F32),
    ]
    out_specs = [
        pl.BlockSpec((None, H, QK_PAD, tm), lambda b, i: (b, 0, 0, i)),
        pl.BlockSpec((None, H, tm, QK_PAD), lambda b, i: (b, 0, i, 0)),
        pl.BlockSpec((None, H, None, MLA_V, tm), lambda b, i: (b, 0, i, 0, 0)),
        tok(GLA_KDIM), tok(GLA_KDIM), tok(GLA_WIDTH), tok(GLA_KDIM), tok(GLA_KDIM), tok(GLA_WIDTH),
    ]
    return pl.pallas_call(
        functools.partial(_pre_kernel, n_first=n_first), grid=(B, nS),
        in_specs=in_specs, out_specs=out_specs, out_shape=out_shape,
        compiler_params=_cparams(("parallel", "parallel")), name="pre",
    )(xa, xb, prm["attn_norm"], prm["w_in"], prm["q_norm"], prm["w_uqT"], prm["kv_norm"], prm["w_k"],
      prm["w_vT"], prm["w_gf"], prm["b_gf"], prm["w_gb"], prm["b_gb"],
      rope["c"], rope["s1"], rope["s2"], rope["cosT"], rope["sinT"])


def _attn_kernel(qT_ref, k_ref, vT_ref, o_ref, s_ref, *, n_kblk, tiles):
    tq = qT_ref.shape[1] // tiles
    tk = k_ref.shape[0] // n_kblk
    per_tile = vT_ref.shape[2] // tk
    ones = jnp.ones((V_ONES_ROWS, tk), BF16)

    def values(j, p):
        v_blk = vT_ref[j // per_tile, :, (j % per_tile) * tk:(j % per_tile + 1) * tk]
        vext = jnp.concatenate([v_blk, ones], axis=0)
        return jnp.dot(vext, p, preferred_element_type=F32)

    for t in range(tiles):
        qT = qT_ref[:, t * tq:(t + 1) * tq]

        def scores(n):
            j = n % n_kblk
            sT = jnp.dot(k_ref[j * tk:(j + 1) * tk, :], qT, preferred_element_type=F32)
            s_ref[n % ATTN_SCORE_SLOTS] = sT
            return jnp.max(sT, axis=0, keepdims=True)

        first = t * n_kblk
        m = jnp.full((1, tq), -jnp.inf, F32)
        acc = jnp.zeros((MLA_V + V_ONES_ROWS, tq), F32)
        blk_max = {first + j: scores(first + j) for j in range(ATTN_LOOKAHEAD)}
        for n in range(first, first + n_kblk):
            if n + ATTN_LOOKAHEAD < first + n_kblk:
                blk_max[n + ATTN_LOOKAHEAD] = scores(n + ATTN_LOOKAHEAD)
            m_new = jnp.maximum(m, blk_max.pop(n))
            p = jnp.exp2(s_ref[n % ATTN_SCORE_SLOTS] - m_new).astype(BF16)
            acc = jnp.exp2(m - m_new) * acc + values(n % n_kblk, p)
            m = m_new
        o_ref[:, t * tq:(t + 1) * tq] = (acc[0:MLA_V] / acc[MLA_V:MLA_V + 1]).astype(o_ref.dtype)


def _attn_call(qT, k, vT, tq, tk):
    B, H, _, S = qT.shape
    n_vblk, v_tile = vT.shape[2], vT.shape[4]
    n_kblk = S // tk
    tiles = ATTN_TILES_PER_STEP if S % (tq * ATTN_TILES_PER_STEP) == 0 else 1
    assert n_kblk >= ATTN_LOOKAHEAD and S % (tq * tiles) == 0 and v_tile % tk == 0
    return pl.pallas_call(
        functools.partial(_attn_kernel, n_kblk=n_kblk, tiles=tiles),
        grid=(B, H, S // (tq * tiles)),
        in_specs=[
            pl.BlockSpec((None, None, QK_PAD, tq * tiles), lambda b, h, i: (b, h, 0, i)),
            pl.BlockSpec((None, None, S, QK_PAD), lambda b, h, i: (b, h, 0, 0)),
            pl.BlockSpec((None, None, n_vblk, MLA_V, v_tile), lambda b, h, i: (b, h, 0, 0, 0)),
        ],
        out_specs=pl.BlockSpec((None, None, MLA_V, tq * tiles), lambda b, h, i: (b, h, 0, i)),
        out_shape=jax.ShapeDtypeStruct((B, H, MLA_V, S), BF16),
        scratch_shapes=[pltpu.VMEM((ATTN_SCORE_SLOTS, tk, tq), F32)],
        compiler_params=_cparams(("parallel", "parallel", "parallel")), name="attn",
    )(qT, k, vT)


def _gla_consts():
    L = GLA_TILE
    i = np.arange(L)[:, None]
    j = np.arange(L)[None, :]
    x = i ^ j
    lidx = np.where(x > 0, np.floor(np.log2(np.maximum(x, 1))), -1).astype(np.int32)
    lidx_f = np.where(i > j, lidx, -1).astype(np.int32)
    lidx_b = np.where(i < j, lidx, -1).astype(np.int32)
    hd = np.arange(GLA_KDIM)[:, None] // GLA_DK
    hv = np.arange(GLA_WIDTH)[None, :] // GLA_DV
    bexp = (hd == hv).astype(np.float32)
    return jnp.asarray(lidx_f), jnp.asarray(lidx_b), jnp.asarray(bexp, BF16)


def _gla_direction(q, k, v, la, lidx, bexp_ref, ssum_ref, state_ref, o_ref, forward):
    L = GLA_TILE
    la_hi, la_lo = _split_bf16(la)
    vb = v.astype(BF16)
    lane_head = lax.broadcasted_iota(jnp.int32, (1, GLA_KDIM), 1) // GLA_DK
    head_masks = [(lane_head == h).astype(BF16) for h in range(GLA_HEADS)]
    row = lax.broadcasted_iota(jnp.int32, la.shape, 0)

    def widen(c, t, m):
        upper_half = (row & m) != 0
        sibling_total = jnp.where(upper_half, pltpu.roll(t, m, axis=0), pltpu.roll(t, L - m, axis=0))
        return c + jnp.where(upper_half, sibling_total, 0.0), t + sibling_total

    c, t = la, la
    for lvl in range(GLA_LEVELS):
        if lvl > 0:
            c, t = widen(c, t, 1 << (lvl - 1))
        if forward:
            eq, ek = c, t - c
        else:
            eq, ek = t - c + la, c - la
        ql = (q * jnp.exp2(eq)).astype(BF16)
        kl = (k * jnp.exp2(ek)).astype(BF16)
        sel = lidx == lvl
        for h in range(GLA_HEADS):
            sc = lax.dot_general(ql * head_masks[h], kl, NT_DIMS, preferred_element_type=F32)
            ssum_ref[h] = jnp.where(sel, sc, ssum_ref[h] if lvl > 0 else 0.0)

    c, t = widen(c, t, L // 2)
    if forward:
        eq, ek = c, t - c
    else:
        eq, ek = t - c + la, c - la
    q_in = (q * jnp.exp2(eq)).astype(BF16)
    k_out = (k * jnp.exp2(ek)).astype(BF16)
    o = jnp.dot(q_in, state_ref[...].astype(BF16), preferred_element_type=F32)
    if forward:
        o = o + jnp.dot((q * k).astype(BF16), bexp_ref[...], preferred_element_type=F32) * v
    for h in range(GLA_HEADS):
        oh = jnp.dot(ssum_ref[h].astype(BF16), vb[:, h * GLA_DV:(h + 1) * GLA_DV],
                     preferred_element_type=F32)
        o_ref[:, h * GLA_DV:(h + 1) * GLA_DV] = o[:, h * GLA_DV:(h + 1) * GLA_DV] + oh

    ones = jnp.ones((L, LANES), BF16)
    tot_col = (lax.dot_general(la_hi, ones, TN_DIMS, preferred_element_type=F32)
               + lax.dot_general(la_lo, ones, TN_DIMS, preferred_element_type=F32))
    dec = jnp.exp2(tot_col)
    upd = lax.dot_general(k_out, vb, TN_DIMS, preferred_element_type=F32)
    for h in range(GLA_HEADS):
        cols = slice(h * GLA_DV, (h + 1) * GLA_DV)
        state_ref[:, cols] = dec * state_ref[:, cols] + upd[:, cols] * bexp_ref[:, cols].astype(F32)


def _gla_kernel(qf_ref, kf_ref, vf_ref, laf_ref, qb_ref, kb_ref, vb_ref, lab_ref,
                lidxf_ref, lidxb_ref, bexp_ref, of_ref, ob_ref, sf_ref, sb_ref, ssum_ref):
    @pl.when(pl.program_id(1) == 0)
    def _():
        sf_ref[...] = jnp.zeros_like(sf_ref)
        sb_ref[...] = jnp.zeros_like(sb_ref)

    _gla_direction(qf_ref[...], kf_ref[...], vf_ref[...], laf_ref[...], lidxf_ref[...],
                   bexp_ref, ssum_ref, sf_ref, of_ref, True)
    _gla_direction(qb_ref[...], kb_ref[...], vb_ref[...], lab_ref[...], lidxb_ref[...],
                   bexp_ref, ssum_ref, sb_ref, ob_ref, False)


def _gla_call(gq, gk, gv, laf, lab):
    B, S, _ = gq.shape
    L = GLA_TILE
    n = S // L
    lidx_f, lidx_b, bexp = _gla_consts()
    fwd = lambda w: pl.BlockSpec((None, L, w), lambda b, i: (b, i, 0))
    bwd = lambda w: pl.BlockSpec((None, L, w), lambda b, i: (b, n - 1 - i, 0))
    const = lambda a: pl.BlockSpec(a.shape, lambda b, i, _nd=a.ndim: (0,) * _nd)
    return pl.pallas_call(
        _gla_kernel, grid=(B, n),
        in_specs=[fwd(GLA_KDIM), fwd(GLA_KDIM), fwd(GLA_WIDTH), fwd(GLA_KDIM),
                  bwd(GLA_KDIM), bwd(GLA_KDIM), bwd(GLA_WIDTH), bwd(GLA_KDIM),
                  const(lidx_f), const(lidx_b), const(bexp)],
        out_specs=[fwd(GLA_WIDTH), bwd(GLA_WIDTH)],
        out_shape=[jax.ShapeDtypeStruct((B, S, GLA_WIDTH), F32)] * 2,
        scratch_shapes=[pltpu.VMEM((GLA_KDIM, GLA_WIDTH), F32), pltpu.VMEM((GLA_KDIM, GLA_WIDTH), F32),
                        pltpu.VMEM((GLA_HEADS, L, L), F32)],
        compiler_params=_cparams(("parallel", "arbitrary")), name="gla",
    )(gq, gk, gv, laf, gq, gk, gv, lab, lidx_f, lidx_b, bexp)


def _post_kernel(x_ref, oT_ref, of_ref, ob_ref, gr_ref, gon_ref, woa_ref, wob_ref, fn_ref,
                 rwT_ref, rb_ref, upper_ref,
                 xmid_ref, h2_ref, meta_ref, gates_ref, counts_ref, carry_ref):
    first = (pl.program_id(0) == 0) & (pl.program_id(1) == 0)

    @pl.when(first)
    def _():
        carry_ref[...] = jnp.zeros_like(carry_ref)

    o = of_ref[...] + ob_ref[...]
    gr = gr_ref[...]
    parts = []
    for h in range(GLA_HEADS):
        cols = slice(h * GLA_DV, (h + 1) * GLA_DV)
        parts.append(_rms(o[:, cols], gon_ref[...]) * jax.nn.silu(gr[:, cols]))
    gla = jnp.concatenate(parts, axis=1).astype(BF16)

    x_mid = (x_ref[...]
             + lax.dot_general(oT_ref[...], woa_ref[...], TN_DIMS, preferred_element_type=F32)
             + jnp.dot(gla, wob_ref[...], preferred_element_type=F32))
    xmid_ref[...] = x_mid
    h2 = _rms(x_mid, fn_ref[...])
    h2_ref[...] = h2

    h_hi, h_lo = _split_bf16(h2)
    w_hi, w_lo = _split_bf16(rwT_ref[...])
    by_h_hi = lax.dot_general(jnp.concatenate([w_hi, w_lo], axis=0), h_hi, NT_DIMS,
                              preferred_element_type=F32)
    logits = (by_h_hi[:N_EXPERTS] + by_h_hi[N_EXPERTS:]
              + lax.dot_general(w_hi, h_lo, NT_DIMS, preferred_element_type=F32)
              + rb_ref[...])
    tm = logits.shape[1]
    eidx = lax.broadcasted_iota(jnp.int32, (N_EXPERTS, tm), 0).astype(F32)
    vals, idxs, sels = [], [], []
    cur = logits
    for _ in range(TOP_K):
        mk = jnp.max(cur, axis=0, keepdims=True)
        ik = jnp.min(jnp.where(cur == mk, eidx, float(N_EXPERTS)), axis=0, keepdims=True)
        sel = eidx == ik
        vals.append(mk)
        idxs.append(ik)
        sels.append(sel)
        cur = jnp.where(sel, -jnp.inf, cur)
    exps = [jnp.exp(vk - vals[0]) for vk in vals]
    denom = exps[0] + exps[1] + exps[2] + exps[3]
    gates = [e / denom for e in exps]

    cnt = (sels[0] | sels[1] | sels[2] | sels[3])
    before = jnp.dot(cnt.astype(BF16), upper_ref[...], preferred_element_type=F32) + carry_ref[:, 0:1]
    ranks = [jnp.sum(jnp.where(sel, before, 0.0), axis=0, keepdims=True) for sel in sels]
    carry_ref[...] = carry_ref[...] + jnp.sum(cnt.astype(F32), axis=1, keepdims=True)
    counts_ref[...] = carry_ref[...]

    meta_ref[...] = jnp.concatenate(idxs + ranks, axis=0).astype(jnp.int32)
    gates_ref[...] = jnp.concatenate(gates + [jnp.zeros((TOP_K, tm), F32)], axis=0)


def _post_call(x, batch0, oT, o_f, o_b, gr, prm, tm):
    B, S, D = x.shape
    nS = S // tm
    upper = jnp.asarray(np.triu(np.ones((tm, tm), np.float32), k=1), BF16)
    full = lambda a: pl.BlockSpec(a.shape, lambda b, i, _nd=a.ndim: (0,) * _nd)
    own = lambda w: pl.BlockSpec((None, tm, w), lambda b, i: (b, i, 0))
    tok = lambda w: pl.BlockSpec((None, tm, w), lambda b, i: (batch0 + b, i, 0))
    colblk = lambda r: pl.BlockSpec((None, r, tm), lambda b, i: (batch0 + b, 0, i))
    flat = lambda r: pl.BlockSpec((r, tm), lambda b, i: (0, b * nS + i))
    return pl.pallas_call(
        _post_kernel, grid=(B, nS),
        in_specs=[own(D), colblk(MLA_WIDTH), tok(GLA_WIDTH), tok(GLA_WIDTH), tok(GLA_WIDTH),
                  full(prm["gla_out_norm"]), full(prm["w_out_a"]), full(prm["w_out_b"]),
                  full(prm["ffn_norm"]), full(prm["router_wT"]), full(prm["router_b"]), full(upper)],
        out_specs=[own(D), own(D), flat(2 * TOP_K), flat(2 * TOP_K),
                   pl.BlockSpec((N_EXPERTS, LANES), lambda b, i: (0, 0))],
        out_shape=[jax.ShapeDtypeStruct((B, S, D), F32), jax.ShapeDtypeStruct((B, S, D), F32),
                   jax.ShapeDtypeStruct((2 * TOP_K, B * S), jnp.int32),
                   jax.ShapeDtypeStruct((2 * TOP_K, B * S), F32),
                   jax.ShapeDtypeStruct((N_EXPERTS, LANES), F32)],
        scratch_shapes=[pltpu.VMEM((N_EXPERTS, LANES), F32)],
        compiler_params=_cparams(("arbitrary", "arbitrary")), name="post",
    )(x, oT, o_f, o_b, gr, prm["gla_out_norm"], prm["w_out_a"], prm["w_out_b"], prm["ffn_norm"],
      prm["router_wT"], prm["router_b"], upper)


def _sc_mesh():
    return plsc.VectorSubcoreMesh(core_axis_name="core", subcore_axis_name="subcore")


def _sc_scatter_rows(x, slots, n_rows):
    T, D = x.shape
    K = slots.shape[0]

    @pl.kernel(out_type=jax.ShapeDtypeStruct((n_rows, D), x.dtype), mesh=_sc_mesh(), scratch_types=[])
    def scatter(x_hbm, slots_hbm, out_hbm):
        def window(x_vmem, slots_vmem):
            for kk in range(K):
                pltpu.sync_copy(x_vmem, out_hbm.at[slots_vmem.at[kk, 0, pl.ds(0, SC_WINDOW)]])

        pltpu.emit_pipeline(
            window, grid=(T // SC_WINDOW,),
            in_specs=[pl.BlockSpec((SC_WINDOW, D), lambda i: (i, 0)),
                      pl.BlockSpec((K, 1, SC_INDEX_PAD), lambda i: (0, i, 0))],
            out_specs=[], core_axis_name=("core", "subcore"),
            dimension_semantics=(pltpu.PARALLEL,))(x_hbm, slots_hbm)

    return scatter(x, slots)


def _sc_gather_rows(y, slots, T):
    D = y.shape[1]
    K = slots.shape[0]

    @pl.kernel(out_type=jax.ShapeDtypeStruct((K, T, D), y.dtype), mesh=_sc_mesh(), scratch_types=[])
    def gather(y_hbm, slots_hbm, out_hbm):
        def window(slots_vmem, out_vmem):
            pltpu.sync_copy(y_hbm.at[slots_vmem.at[0, 0, pl.ds(0, SC_WINDOW)]], out_vmem.at[0])

        pltpu.emit_pipeline(
            window, grid=(K, T // SC_WINDOW),
            in_specs=[pl.BlockSpec((1, 1, SC_INDEX_PAD), lambda kk, i: (kk, i, 0))],
            out_specs=[pl.BlockSpec((1, SC_WINDOW, D), lambda kk, i: (kk, i, 0))],
            core_axis_name=("core", "subcore"),
            dimension_semantics=(pltpu.PARALLEL, pltpu.PARALLEL))(slots_hbm, out_hbm)

    return gather(y, slots)


def _expert_weights_kernel(wgu_ref, wdn_ref, perm_ref, wg_ref, wl_ref, wd_ref):
    perm = perm_ref[...]
    half = PERM_GROUP // 2
    for c in range(2 * D_FF // PERM_GROUP):
        w = wgu_ref[:, c * PERM_GROUP:(c + 1) * PERM_GROUP].astype(BF16)
        sep = jnp.dot(w, perm, preferred_element_type=F32).astype(BF16)
        wg_ref[:, c * half:(c + 1) * half] = sep[:, :half]
        wl_ref[:, c * half:(c + 1) * half] = sep[:, half:]
    wd_ref[...] = wdn_ref[...].astype(BF16)


def _expert_weights_call(w_gu, w_dn):
    E, D, _ = w_gu.shape
    half = PERM_GROUP // 2
    src = np.concatenate([2 * np.arange(half), 2 * np.arange(half) + 1])
    perm = jnp.asarray(np.arange(PERM_GROUP)[:, None] == src[None, :], BF16)
    per_expert = lambda r, c: pl.BlockSpec((None, r, c), lambda e: (e, 0, 0))
    return pl.pallas_call(
        _expert_weights_kernel, grid=(E,),
        in_specs=[per_expert(D, 2 * D_FF), per_expert(D_FF, D),
                  pl.BlockSpec((PERM_GROUP, PERM_GROUP), lambda e: (0, 0))],
        out_specs=[per_expert(D, D_FF), per_expert(D, D_FF), per_expert(D_FF, D)],
        out_shape=[jax.ShapeDtypeStruct((E, D, D_FF), BF16), jax.ShapeDtypeStruct((E, D, D_FF), BF16),
                   jax.ShapeDtypeStruct((E, D_FF, D), BF16)],
        compiler_params=_cparams(("parallel",)), name="expert_weights",
    )(w_gu, w_dn, perm)


def _expert_kernel(be_ref, nv_ref, xs_ref, wg_ref, wl_ref, bg_ref, bl_ref, wd_ref, bd_ref, ys_ref):
    del be_ref
    valid = pl.program_id(0) < nv_ref[0]

    @pl.when(valid)
    def _():
        xb = xs_ref[...].astype(BF16)
        g = jnp.dot(xb, wg_ref[...], preferred_element_type=F32) + bg_ref[...]
        l = jnp.dot(xb, wl_ref[...], preferred_element_type=F32) + bl_ref[...]
        glu = jnp.minimum(g, SWIGLU_LIMIT)
        lin = jnp.clip(l, -SWIGLU_LIMIT, SWIGLU_LIMIT)
        act = glu * jax.nn.sigmoid(SWIGLU_ALPHA * glu) * (lin + 1.0)
        ys_ref[...] = jnp.dot(act.astype(BF16), wd_ref[...], preferred_element_type=F32) + bd_ref[...]

    @pl.when(jnp.logical_not(valid))
    def _():
        ys_ref[...] = jnp.zeros_like(ys_ref)


def _expert_call(block_expert, n_valid, xs, prm):
    n_rows, D = xs.shape
    n_blocks = n_rows // MOE_BLOCK
    rows = lambda i, be, nv: (jnp.minimum(i, nv[0] - 1), 0)
    wsel = lambda i, be, nv: (be[i], 0, 0)
    grid_spec = pltpu.PrefetchScalarGridSpec(
        num_scalar_prefetch=2, grid=(n_blocks,),
        in_specs=[pl.BlockSpec((MOE_BLOCK, D), rows),
                  pl.BlockSpec((None, D, D_FF), wsel), pl.BlockSpec((None, D, D_FF), wsel),
                  pl.BlockSpec((None, 1, D_FF), wsel), pl.BlockSpec((None, 1, D_FF), wsel),
                  pl.BlockSpec((None, D_FF, D), wsel), pl.BlockSpec((None, 1, D), wsel)],
        out_specs=pl.BlockSpec((MOE_BLOCK, D), lambda i, be, nv: (i, 0)))
    return pl.pallas_call(
        _expert_kernel, grid_spec=grid_spec,
        out_shape=jax.ShapeDtypeStruct((n_rows, D), F32),
        compiler_params=_cparams(("arbitrary",)), name="experts",
    )(block_expert, n_valid, xs, prm["w_glu"], prm["w_lin"], prm["b_glu"], prm["b_lin"],
      prm["w_dn"], prm["b_dn"])


def _combine_kernel(xmid_ref, gates_ref, fnorm_ref, y4_ref, out_ref):
    g = jnp.transpose(gates_ref[...])
    y = xmid_ref[...]
    for kk in range(TOP_K):
        y = y + y4_ref[kk] * g[:, kk:kk + 1]
    out_ref[...] = _rms(y, fnorm_ref[...])


def _combine_call(x_mid, gates, final_norm, y4, tc):
    T, D = x_mid.shape
    return pl.pallas_call(
        _combine_kernel, grid=(T // tc,),
        in_specs=[pl.BlockSpec((tc, D), lambda i: (i, 0)),
                  pl.BlockSpec((2 * TOP_K, tc), lambda i: (0, i)),
                  pl.BlockSpec(final_norm.shape, lambda i: (0, 0)),
                  pl.BlockSpec((TOP_K, tc, D), lambda i: (0, i, 0))],
        out_specs=pl.BlockSpec((tc, D), lambda i: (i, 0)),
        out_shape=jax.ShapeDtypeStruct((T, D), F32),
        compiler_params=_cparams(("parallel",)), name="combine",
    )(x_mid, gates, final_norm, y4)


def _prep_params(attn_norm, w_in, mla_q_norm, mla_w_uq, mla_kv_norm, mla_w_ukv, gla_w_gate_fwd,
                 gla_b_gate_fwd, gla_w_gate_bwd, gla_b_gate_bwd, gla_out_norm, w_out, ffn_norm,
                 router_w, router_b, w_gu, b_gu, w_dn, b_dn):
    D = D_MODEL
    o = np.cumsum((0, MLA_Q_LORA, MLA_KV_LORA, MLA_ROPE, GLA_KDIM, GLA_KDIM, GLA_WIDTH,
                   2 * GLA_GATE_RANK, GLA_WIDTH))
    seg = [w_in[:, o[n]:o[n + 1]] for n in range(8)]
    z = lambda n: jnp.zeros((D, n), w_in.dtype)
    w_in_p = jnp.concatenate(
        [seg[0], seg[1], z(MLA_NOPE), seg[2], z(LANES - MLA_NOPE - MLA_ROPE), seg[3], seg[4], seg[5],
         seg[6], z(LANES - 2 * GLA_GATE_RANK), seg[7]], axis=1).astype(BF16)

    wq = mla_w_uq.reshape(MLA_Q_LORA, MLA_HEADS, MLA_NOPE + MLA_ROPE)
    w_uq_p = jnp.concatenate(
        [wq[:, :, :MLA_NOPE].reshape(MLA_Q_LORA, -1),
         wq[:, :, MLA_NOPE:MLA_NOPE + HALF_ROPE].reshape(MLA_Q_LORA, -1),
         wq[:, :, MLA_NOPE + HALF_ROPE:].reshape(MLA_Q_LORA, -1)], axis=1)
    wkv = mla_w_ukv.reshape(MLA_KV_LORA, MLA_HEADS, MLA_NOPE + MLA_V)
    w_k = jnp.concatenate([wkv[:, :, :MLA_NOPE], jnp.zeros((MLA_KV_LORA, MLA_HEADS, QK_PAD - MLA_NOPE),
                                                            wkv.dtype)], axis=2)
    w_v = wkv[:, :, MLA_NOPE:].reshape(MLA_KV_LORA, -1)

    def gate_w(w, row0):
        full = jnp.zeros((LANES, GLA_KDIM), w.dtype)
        return full.at[row0:row0 + GLA_GATE_RANK].set(w).astype(BF16)

    E = N_EXPERTS
    w_glu, w_lin, w_dn_b = _expert_weights_call(w_gu, w_dn)
    return {
        "attn_norm": attn_norm.reshape(1, D), "w_in": w_in_p,
        "q_norm": mla_q_norm.reshape(1, -1), "w_uqT": w_uq_p.T.astype(BF16),
        "kv_norm": mla_kv_norm.reshape(1, -1),
        "w_k": w_k.reshape(MLA_KV_LORA, -1).astype(BF16), "w_vT": w_v.T.astype(BF16),
        "w_gf": gate_w(gla_w_gate_fwd, 0), "b_gf": gla_b_gate_fwd.reshape(1, -1),
        "w_gb": gate_w(gla_w_gate_bwd, GLA_GATE_RANK), "b_gb": gla_b_gate_bwd.reshape(1, -1),
        "gla_out_norm": gla_out_norm.reshape(1, -1),
        "w_out_a": w_out[:MLA_WIDTH].astype(BF16), "w_out_b": w_out[MLA_WIDTH:].astype(BF16),
        "ffn_norm": ffn_norm.reshape(1, D),
        "router_wT": router_w.T, "router_b": router_b.reshape(E, 1),
        "w_glu": w_glu, "w_lin": w_lin, "w_dn": w_dn_b,
        "b_glu": b_gu[:, 0::2].reshape(E, 1, D_FF), "b_lin": b_gu[:, 1::2].reshape(E, 1, D_FF),
        "b_dn": b_dn.reshape(E, 1, D),
    }


def _rope_tables(S):
    inv_freq = jnp.power(ROPE_THETA, -jnp.arange(0, MLA_ROPE, 2, dtype=F32) / MLA_ROPE)
    ang = jnp.arange(S, dtype=F32)[:, None] * inv_freq[None, :]
    cos, sin = jnp.cos(ang), jnp.sin(ang)
    z = lambda n: jnp.zeros((S, n), F32)
    tail = LANES - MLA_NOPE - MLA_ROPE
    return {
        "c": jnp.concatenate([z(MLA_NOPE), cos, cos, z(tail)], axis=1),
        "s1": jnp.concatenate([z(MLA_NOPE), -sin, z(HALF_ROPE), z(tail)], axis=1),
        "s2": jnp.concatenate([z(MLA_NOPE), z(HALF_ROPE), sin, z(tail)], axis=1),
        "cosT": jnp.tile(cos.T, (MLA_HEADS, 1)), "sinT": jnp.tile(sin.T, (MLA_HEADS, 1)),
    }


def _encoder(xa, xb, prm, final_norm):
    n_first, S, D = xa.shape
    B = n_first + xb.shape[0]
    rope = _rope_tables(S)
    qT, k, vT, gq, gk, gv, laf, lab, gr = _pre_call(xa, xb, prm, rope, min(TOKEN_TILE, S))
    oT = _attn_call(qT, k, vT, min(ATTN_QUERY_TILE, S), ATTN_KEY_TILE).reshape(B, MLA_WIDTH, S)
    o_f, o_b = _gla_call(gq, gk, gv, laf, lab)
    mixed = (oT, o_f, o_b, gr)
    return (_moe_block(xa, 0, mixed, prm, final_norm), _moe_block(xb, n_first, mixed, prm, final_norm))


def _moe_block(x, batch0, mixed, prm, final_norm):
    B, S, D = x.shape
    x_mid, h2, meta, gates, counts = _post_call(x, batch0, *mixed, prm, min(TOKEN_TILE, S))

    T = B * S
    n_rows = T * TOP_K + N_EXPERTS * MOE_BLOCK
    n_blocks = n_rows // MOE_BLOCK
    cnt = counts[:, 0].astype(jnp.int32)
    padded = ((cnt + MOE_BLOCK - 1) // MOE_BLOCK) * MOE_BLOCK
    pends = jnp.cumsum(padded)
    starts = (pends - padded).astype(jnp.int32)
    n_valid = (pends[-1] // MOE_BLOCK).astype(jnp.int32).reshape(1)
    blk = jnp.minimum(jnp.arange(n_blocks, dtype=jnp.int32), n_valid[0] - 1) * MOE_BLOCK
    block_expert = jnp.minimum(jnp.sum(pends[None, :] <= blk[:, None], axis=1), N_EXPERTS - 1).astype(jnp.int32)

    experts = jnp.arange(N_EXPERTS, dtype=jnp.int32)[:, None, None]
    slots = jnp.sum(jnp.where(meta[None, :TOP_K] == experts, starts[:, None, None], 0), axis=0) + meta[TOP_K:]
    slots = jnp.pad(slots.reshape(TOP_K, T // SC_WINDOW, SC_WINDOW),
                    ((0, 0), (0, 0), (0, SC_INDEX_PAD - SC_WINDOW)))

    xs = _sc_scatter_rows(h2.reshape(T, D), slots, n_rows)
    ys = _expert_call(block_expert, n_valid, xs, prm)
    y4 = _sc_gather_rows(ys, slots, T)
    y = _combine_call(x_mid.reshape(T, D), gates, final_norm.reshape(1, D), y4, min(256, S))
    return y.reshape(x.shape)


def kernel(x_prompt, x_sample, attn_norm, w_in, mla_q_norm, mla_w_uq, mla_kv_norm, mla_w_ukv,
           gla_w_gate_fwd, gla_b_gate_fwd, gla_w_gate_bwd, gla_b_gate_bwd, gla_out_norm, w_out, ffn_norm,
           router_w, router_b, expert_w_gate_up, expert_b_gate_up, expert_w_down, expert_b_down,
           final_norm):
    layer = (attn_norm, w_in, mla_q_norm, mla_w_uq, mla_kv_norm, mla_w_ukv, gla_w_gate_fwd,
             gla_b_gate_fwd, gla_w_gate_bwd, gla_b_gate_bwd, gla_out_norm, w_out, ffn_norm, router_w,
             router_b, expert_w_gate_up, expert_b_gate_up, expert_w_down, expert_b_down)
    assert all(p.shape[0] == 1 for p in layer), "single layer expected"
    assert x_prompt.shape[1:] == x_sample.shape[1:]
    prm = _prep_params(*[p[0] for p in layer])
    return _encoder(x_prompt, x_sample, prm, final_norm)
```

```python
import functools

import jax
import jax.numpy as jnp
import numpy as np
from jax import lax
from jax.experimental import pallas as pl
from jax.experimental.pallas import tpu as pltpu
from jax.experimental.pallas import tpu_sc as plsc

F32 = jnp.float32
BF16 = jnp.bfloat16

D_MODEL = 1024
MLA_HEADS = 8
MLA_NOPE = 64
MLA_ROPE = 32
MLA_V = 64
MLA_Q_LORA = 384
MLA_KV_LORA = 256
ROPE_THETA = 10000.0
GLA_HEADS = 4
GLA_DK = 64
GLA_DV = 128
GLA_GATE_RANK = 16
GLA_GATE_NORM = 16.0
N_EXPERTS = 32
TOP_K = 4
D_FF = 1024
SWIGLU_LIMIT = 7.0
SWIGLU_ALPHA = 1.702
MOE_BLOCK = 512
RMS_EPS = 1e-6

MLA_WIDTH = MLA_HEADS * MLA_V
GLA_WIDTH = GLA_HEADS * GLA_DV
GLA_KDIM = GLA_HEADS * GLA_DK
HALF_ROPE = MLA_ROPE // 2
QK_PAD = 128
LANES = 128
V_ONES_ROWS = 16

OFF_CQ = 0
OFF_CKV = OFF_CQ + MLA_Q_LORA
OFF_KR = OFF_CKV + MLA_KV_LORA
OFF_GQ = OFF_KR + LANES
OFF_GK = OFF_GQ + GLA_KDIM
OFF_GV = OFF_GK + GLA_KDIM
OFF_LR = OFF_GV + GLA_WIDTH
OFF_GR = OFF_LR + LANES
PROJ_COLS = OFF_GR + GLA_WIDTH

TOKEN_TILE = 512
ATTN_KEY_TILE = 256
ATTN_QUERY_TILE = 512
ATTN_LOOKAHEAD = 2
ATTN_SCORE_SLOTS = 4
GLA_TILE = 256
GLA_LEVELS = 8
SC_WINDOW = 32
SC_INDEX_PAD = 128
ATTN_TILES_PER_STEP = 2
PERM_GROUP = 256
VMEM_LIMIT = 56 * 1024 * 1024

LOG2_E = 1.4426950408889634
NT_DIMS = (((1,), (1,)), ((), ()))
TN_DIMS = (((0,), (0,)), ((), ()))


def _cparams(sem):
    return pltpu.CompilerParams(dimension_semantics=sem, vmem_limit_bytes=VMEM_LIMIT)


def _rms(x, gain):
    return x * lax.rsqrt(jnp.mean(x * x, axis=-1, keepdims=True) + RMS_EPS) * gain


def _split_bf16(x):
    hi = x.astype(BF16)
    lo = (x - hi.astype(F32)).astype(BF16)
    return hi, lo


def _pre_kernel(xa_ref, xb_ref, an_ref, win_ref, qn_ref, wuqT_ref, kvn_ref, wk_ref, wvT_ref,
                wgf_ref, bgf_ref, wgb_ref, bgb_ref, rc_ref, rs1_ref, rs2_ref, cosT_ref, sinT_ref,
                qT_ref, k_ref, vT_ref, gq_ref, gk_ref, gv_ref, laf_ref, lab_ref, gr_ref, *, n_first):
    x = jnp.where(pl.program_id(0) < n_first, xa_ref[...], xb_ref[...])
    h = _rms(x, an_ref[...]).astype(BF16)
    proj = jnp.dot(h, win_ref[...], preferred_element_type=F32)

    cqn = _rms(proj[:, OFF_CQ:OFF_CQ + MLA_Q_LORA], qn_ref[...]).astype(BF16)
    ckvn = _rms(proj[:, OFF_CKV:OFF_CKV + MLA_KV_LORA], kvn_ref[...]).astype(BF16)

    scale = (MLA_NOPE + MLA_ROPE) ** -0.5 * LOG2_E
    qT = lax.dot_general(wuqT_ref[...], cqn, NT_DIMS, preferred_element_type=F32) * scale
    n_nope = MLA_HEADS * MLA_NOPE
    n_half = MLA_HEADS * HALF_ROPE
    x1 = qT[n_nope:n_nope + n_half]
    x2 = qT[n_nope + n_half:n_nope + 2 * n_half]
    c = cosT_ref[...]
    s = sinT_ref[...]
    x1r = x1 * c - x2 * s
    x2r = x1 * s + x2 * c
    zpad = jnp.zeros((QK_PAD - MLA_NOPE - MLA_ROPE, qT.shape[1]), BF16)
    for hd in range(MLA_HEADS):
        qT_ref[hd, 0:MLA_NOPE, :] = qT[hd * MLA_NOPE:(hd + 1) * MLA_NOPE].astype(BF16)
        qT_ref[hd, MLA_NOPE:MLA_NOPE + HALF_ROPE, :] = x1r[hd * HALF_ROPE:(hd + 1) * HALF_ROPE].astype(BF16)
        qT_ref[hd, MLA_NOPE + HALF_ROPE:MLA_NOPE + MLA_ROPE, :] = (
            x2r[hd * HALF_ROPE:(hd + 1) * HALF_ROPE].astype(BF16))
        qT_ref[hd, MLA_NOPE + MLA_ROPE:QK_PAD, :] = zpad

    kr = proj[:, OFF_KR:OFF_KR + LANES]
    kr = (kr * rc_ref[...]
          + pltpu.roll(kr, LANES - HALF_ROPE, axis=1) * rs1_ref[...]
          + pltpu.roll(kr, HALF_ROPE, axis=1) * rs2_ref[...])
    kfull = jnp.dot(ckvn, wk_ref[...], preferred_element_type=F32)
    for hd in range(MLA_HEADS):
        k_ref[hd] = (kfull[:, hd * QK_PAD:(hd + 1) * QK_PAD] + kr).astype(BF16)

    vT = lax.dot_general(wvT_ref[...], ckvn, NT_DIMS, preferred_element_type=F32)
    for hd in range(MLA_HEADS):
        vT_ref[hd] = vT[hd * MLA_V:(hd + 1) * MLA_V].astype(BF16)

    gq_ref[...] = proj[:, OFF_GQ:OFF_GQ + GLA_KDIM] * (GLA_DK ** -0.5)
    gk_ref[...] = proj[:, OFF_GK:OFF_GK + GLA_KDIM]
    gv_ref[...] = proj[:, OFF_GV:OFF_GV + GLA_WIDTH]
    gr_ref[...] = proj[:, OFF_GR:OFF_GR + GLA_WIDTH]
    lr = proj[:, OFF_LR:OFF_LR + LANES].astype(BF16)

    def log_decay(w_ref, b_ref):
        z = jnp.dot(lr, w_ref[...], preferred_element_type=F32) + b_ref[...]
        return (jnp.minimum(z, 0.0) - jnp.log1p(jnp.exp(-jnp.abs(z)))) * (LOG2_E / GLA_GATE_NORM)

    laf_ref[...] = log_decay(wgf_ref, bgf_ref)
    lab_ref[...] = log_decay(wgb_ref, bgb_ref)


def _two_source_specs(n_first, tm, width):
    return [pl.BlockSpec((None, tm, width), lambda b, i: (jnp.minimum(b, n_first - 1), i, 0)),
            pl.BlockSpec((None, tm, width), lambda b, i: (jnp.maximum(b - n_first, 0), i, 0))]


def _pre_call(xa, xb, prm, rope, tm):
    n_first, S, D = xa.shape
    B = n_first + xb.shape[0]
    nS = S // tm
    H = MLA_HEADS

    def full(a):
        nd = a.ndim
        return pl.BlockSpec(a.shape, lambda b, i, _nd=nd: (0,) * _nd)

    tok = lambda w: pl.BlockSpec((None, tm, w), lambda b, i: (b, i, 0))
    in_specs = [
        *_two_source_specs(n_first, tm, D), full(prm["attn_norm"]), full(prm["w_in"]), full(prm["q_norm"]),
        full(prm["w_uqT"]), full(prm["kv_norm"]), full(prm["w_k"]), full(prm["w_vT"]),
        full(prm["w_gf"]), full(prm["b_gf"]), full(prm["w_gb"]), full(prm["b_gb"]),
        pl.BlockSpec((tm, LANES), lambda b, i: (i, 0)),
        pl.BlockSpec((tm, LANES), lambda b, i: (i, 0)),
        pl.BlockSpec((tm, LANES), lambda b, i: (i, 0)),
        pl.BlockSpec((H * HALF_ROPE, tm), lambda b, i: (0, i)),
        pl.BlockSpec((H * HALF_ROPE, tm), lambda b, i: (0, i)),
    ]
    out_shape = [
        jax.ShapeDtypeStruct((B, H, QK_PAD, S), BF16),
        jax.ShapeDtypeStruct((B, H, S, QK_PAD), BF16),
        jax.ShapeDtypeStruct((B, H, nS, MLA_V, tm), BF16),
        jax.ShapeDtypeStruct((B, S, GLA_KDIM), F32),
        jax.ShapeDtypeStruct((B, S, GLA_KDIM), F32),
        jax.ShapeDtypeStruct((B, S, GLA_WIDTH), F32),
        jax.ShapeDtypeStruct((B, S, GLA_KDIM), F32),
        jax.ShapeDtypeStruct((B, S, GLA_KDIM), F32),
        jax.ShapeDtypeStruct((B, S, GLA_WIDTH), F32),
    ]
    out_specs = [
        pl.BlockSpec((None, H, QK_PAD, tm), lambda b, i: (b, 0, 0, i)),
        pl.BlockSpec((None, H, tm, QK_PAD), lambda b, i: (b, 0, i, 0)),
        pl.BlockSpec((None, H, None, MLA_V, tm), lambda b, i: (b, 0, i, 0, 0)),
        tok(GLA_KDIM), tok(GLA_KDIM), tok(GLA_WIDTH), tok(GLA_KDIM), tok(GLA_KDIM), tok(GLA_WIDTH),
    ]
    return pl.pallas_call(
        functools.partial(_pre_kernel, n_first=n_first), grid=(B, nS),
        in_specs=in_specs, out_specs=out_specs, out_shape=out_shape,
        compiler_params=_cparams(("parallel", "parallel")), name="pre",
    )(xa, xb, prm["attn_norm"], prm["w_in"], prm["q_norm"], prm["w_uqT"], prm["kv_norm"], prm["w_k"],
      prm["w_vT"], prm["w_gf"], prm["b_gf"], prm["w_gb"], prm["b_gb"],
      rope["c"], rope["s1"], rope["s2"], rope["cosT"], rope["sinT"])


def _attn_kernel(qT_ref, k_ref, vT_ref, o_ref, s_ref, *, n_kblk, tiles):
    tq = qT_ref.shape[1] // tiles
    tk = k_ref.shape[0] // n_kblk
    per_tile = vT_ref.shape[2] // tk
    ones = jnp.ones((V_ONES_ROWS, tk), BF16)

    def values(j, p):
        v_blk = vT_ref[j // per_tile, :, (j % per_tile) * tk:(j % per_tile + 1) * tk]
        vext = jnp.concatenate([v_blk, ones], axis=0)
        return jnp.dot(vext, p, preferred_element_type=F32)

    for t in range(tiles):
        qT = qT_ref[:, t * tq:(t + 1) * tq]

        def scores(n):
            j = n % n_kblk
            sT = jnp.dot(k_ref[j * tk:(j + 1) * tk, :], qT, preferred_element_type=F32)
            s_ref[n % ATTN_SCORE_SLOTS] = sT
            return jnp.max(sT, axis=0, keepdims=True)

        first = t * n_kblk
        m = jnp.full((1, tq), -jnp.inf, F32)
        acc = jnp.zeros((MLA_V + V_ONES_ROWS, tq), F32)
        blk_max = {first + j: scores(first + j) for j in range(ATTN_LOOKAHEAD)}
        for n in range(first, first + n_kblk):
            if n + ATTN_LOOKAHEAD < first + n_kblk:
                blk_max[n + ATTN_LOOKAHEAD] = scores(n + ATTN_LOOKAHEAD)
            m_new = jnp.maximum(m, blk_max.pop(n))
            p = jnp.exp2(s_ref[n % ATTN_SCORE_SLOTS] - m_new).astype(BF16)
            acc = jnp.exp2(m - m_new) * acc + values(n % n_kblk, p)
            m = m_new
        o_ref[:, t * tq:(t + 1) * tq] = (acc[0:MLA_V] / acc[MLA_V:MLA_V + 1]).astype(o_ref.dtype)


def _attn_call(qT, k, vT, tq, tk):
    B, H, _, S = qT.shape
    n_vblk, v_tile = vT.shape[2], vT.shape[4]
    n_kblk = S // tk
    tiles = ATTN_TILES_PER_STEP if S % (tq * ATTN_TILES_PER_STEP) == 0 else 1
    assert n_kblk >= ATTN_LOOKAHEAD and S % (tq * tiles) == 0 and v_tile % tk == 0
    return pl.pallas_call(
        functools.partial(_attn_kernel, n_kblk=n_kblk, tiles=tiles),
        grid=(B, H, S // (tq * tiles)),
        in_specs=[
            pl.BlockSpec((None, None, QK_PAD, tq * tiles), lambda b, h, i: (b, h, 0, i)),
            pl.BlockSpec((None, None, S, QK_PAD), lambda b, h, i: (b, h, 0, 0)),
            pl.BlockSpec((None, None, n_vblk, MLA_V, v_tile), lambda b, h, i: (b, h, 0, 0, 0)),
        ],
        out_specs=pl.BlockSpec((None, None, MLA_V, tq * tiles), lambda b, h, i: (b, h, 0, i)),
        out_shape=jax.ShapeDtypeStruct((B, H, MLA_V, S), BF16),
        scratch_shapes=[pltpu.VMEM((ATTN_SCORE_SLOTS, tk, tq), F32)],
        compiler_params=_cparams(("parallel", "parallel", "parallel")), name="attn",
    )(qT, k, vT)


def _gla_consts():
    L = GLA_TILE
    i = np.arange(L)[:, None]
    j = np.arange(L)[None, :]
    x = i ^ j
    lidx = np.where(x > 0, np.floor(np.log2(np.maximum(x, 1))), -1).astype(np.int32)
    lidx_f = np.where(i > j, lidx, -1).astype(np.int32)
    lidx_b = np.where(i < j, lidx, -1).astype(np.int32)
    hd = np.arange(GLA_KDIM)[:, None] // GLA_DK
    hv = np.arange(GLA_WIDTH)[None, :] // GLA_DV
    bexp = (hd == hv).astype(np.float32)
    return jnp.asarray(lidx_f), jnp.asarray(lidx_b), jnp.asarray(bexp, BF16)


def _gla_direction(q, k, v, la, lidx, bexp_ref, ssum_ref, state_ref, o_ref, forward):
    L = GLA_TILE
    la_hi, la_lo = _split_bf16(la)
    vb = v.astype(BF16)
    lane_head = lax.broadcasted_iota(jnp.int32, (1, GLA_KDIM), 1) // GLA_DK
    head_masks = [(lane_head == h).astype(BF16) for h in range(GLA_HEADS)]
    row = lax.broadcasted_iota(jnp.int32, la.shape, 0)

    def widen(c, t, m):
        upper_half = (row & m) != 0
        sibling_total = jnp.where(upper_half, pltpu.roll(t, m, axis=0), pltpu.roll(t, L - m, axis=0))
        return c + jnp.where(upper_half, sibling_total, 0.0), t + sibling_total

    c, t = (la if forward else jnp.zeros_like(la)), la
    for lvl in range(GLA_LEVELS):
        if lvl > 0:
            c, t = widen(c, t, 1 << (lvl - 1))
        if forward:
            eq, ek = c, t - c
        else:
            eq, ek = t - c, c
        ql = (q * jnp.exp2(eq)).astype(BF16)
        kl = (k * jnp.exp2(ek)).astype(BF16)
        sel = lidx == lvl
        for h in range(GLA_HEADS):
            sc = lax.dot_general(ql * head_masks[h], kl, NT_DIMS, preferred_element_type=F32)
            ssum_ref[h] = jnp.where(sel, sc, ssum_ref[h] if lvl > 0 else 0.0)

    c, t = widen(c, t, L // 2)
    if forward:
        eq, ek = c, t - c
    else:
        eq, ek = t - c, c
    q_in = (q * jnp.exp2(eq)).astype(BF16)
    k_out = (k * jnp.exp2(ek)).astype(BF16)
    o = jnp.dot(q_in, state_ref[...].astype(BF16), preferred_element_type=F32)
    if forward:
        o = o + jnp.dot((q * k).astype(BF16), bexp_ref[...], preferred_element_type=F32) * v
    for h in range(GLA_HEADS):
        oh = jnp.dot(ssum_ref[h].astype(BF16), vb[:, h * GLA_DV:(h + 1) * GLA_DV],
                     preferred_element_type=F32)
        o_ref[:, h * GLA_DV:(h + 1) * GLA_DV] = o[:, h * GLA_DV:(h + 1) * GLA_DV] + oh

    ones = jnp.ones((L, LANES), BF16)
    tot_col = (lax.dot_general(la_hi, ones, TN_DIMS, preferred_element_type=F32)
               + lax.dot_general(la_lo, ones, TN_DIMS, preferred_element_type=F32))
    dec = jnp.exp2(tot_col)
    upd = lax.dot_general(k_out, vb, TN_DIMS, preferred_element_type=F32)
    for h in range(GLA_HEADS):
        cols = slice(h * GLA_DV, (h + 1) * GLA_DV)
        state_ref[:, cols] = dec * state_ref[:, cols] + upd[:, cols] * bexp_ref[:, cols].astype(F32)


def _gla_kernel(qf_ref, kf_ref, vf_ref, laf_ref, qb_ref, kb_ref, vb_ref, lab_ref,
                lidxf_ref, lidxb_ref, bexp_ref, of_ref, ob_ref, sf_ref, sb_ref, ssum_ref):
    @pl.when(pl.program_id(1) == 0)
    def _():
        sf_ref[...] = jnp.zeros_like(sf_ref)
        sb_ref[...] = jnp.zeros_like(sb_ref)

    _gla_direction(qf_ref[...], kf_ref[...], vf_ref[...], laf_ref[...], lidxf_ref[...],
                   bexp_ref, ssum_ref, sf_ref, of_ref, True)
    _gla_direction(qb_ref[...], kb_ref[...], vb_ref[...], lab_ref[...], lidxb_ref[...],
                   bexp_ref, ssum_ref, sb_ref, ob_ref, False)


def _gla_call(gq, gk, gv, laf, lab):
    B, S, _ = gq.shape
    L = GLA_TILE
    n = S // L
    lidx_f, lidx_b, bexp = _gla_consts()
    fwd = lambda w: pl.BlockSpec((None, L, w), lambda b, i: (b, i, 0))
    bwd = lambda w: pl.BlockSpec((None, L, w), lambda b, i: (b, n - 1 - i, 0))
    const = lambda a: pl.BlockSpec(a.shape, lambda b, i, _nd=a.ndim: (0,) * _nd)
    return pl.pallas_call(
        _gla_kernel, grid=(B, n),
        in_specs=[fwd(GLA_KDIM), fwd(GLA_KDIM), fwd(GLA_WIDTH), fwd(GLA_KDIM),
                  bwd(GLA_KDIM), bwd(GLA_KDIM), bwd(GLA_WIDTH), bwd(GLA_KDIM),
                  const(lidx_f), const(lidx_b), const(bexp)],
        out_specs=[fwd(GLA_WIDTH), bwd(GLA_WIDTH)],
        out_shape=[jax.ShapeDtypeStruct((B, S, GLA_WIDTH), F32)] * 2,
        scratch_shapes=[pltpu.VMEM((GLA_KDIM, GLA_WIDTH), F32), pltpu.VMEM((GLA_KDIM, GLA_WIDTH), F32),
                        pltpu.VMEM((GLA_HEADS, L, L), F32)],
        compiler_params=_cparams(("parallel", "arbitrary")), name="gla",
    )(gq, gk, gv, laf, gq, gk, gv, lab, lidx_f, lidx_b, bexp)


def _post_kernel(x_ref, oT_ref, of_ref, ob_ref, gr_ref, gon_ref, woa_ref, wob_ref, fn_ref,
                 rwT_ref, rb_ref, upper_ref,
                 xmid_ref, h2_ref, meta_ref, gates_ref, counts_ref, carry_ref):
    first = (pl.program_id(0) == 0) & (pl.program_id(1) == 0)

    @pl.when(first)
    def _():
        carry_ref[...] = jnp.zeros_like(carry_ref)

    o = of_ref[...] + ob_ref[...]
    gr = gr_ref[...]
    parts = []
    for h in range(GLA_HEADS):
        cols = slice(h * GLA_DV, (h + 1) * GLA_DV)
        parts.append(_rms(o[:, cols], gon_ref[...]) * jax.nn.silu(gr[:, cols]))
    gla = jnp.concatenate(parts, axis=1).astype(BF16)

    x_mid = (x_ref[...]
             + lax.dot_general(oT_ref[...], woa_ref[...], TN_DIMS, preferred_element_type=F32)
             + jnp.dot(gla, wob_ref[...], preferred_element_type=F32))
    xmid_ref[...] = x_mid
    h2 = _rms(x_mid, fn_ref[...])
    h2_ref[...] = h2

    h_hi, h_lo = _split_bf16(h2)
    w_hi, w_lo = _split_bf16(rwT_ref[...])
    by_h_hi = lax.dot_general(jnp.concatenate([w_hi, w_lo], axis=0), h_hi, NT_DIMS,
                              preferred_element_type=F32)
    logits = (by_h_hi[:N_EXPERTS] + by_h_hi[N_EXPERTS:]
              + lax.dot_general(w_hi, h_lo, NT_DIMS, preferred_element_type=F32)
              + rb_ref[...])
    tm = logits.shape[1]
    eidx = lax.broadcasted_iota(jnp.int32, (N_EXPERTS, tm), 0).astype(F32)
    vals, idxs, sels = [], [], []
    cur = logits
    for _ in range(TOP_K):
        mk = jnp.max(cur, axis=0, keepdims=True)
        ik = jnp.min(jnp.where(cur == mk, eidx, float(N_EXPERTS)), axis=0, keepdims=True)
        sel = eidx == ik
        vals.append(mk)
        idxs.append(ik)
        sels.append(sel)
        cur = jnp.where(sel, -jnp.inf, cur)
    exps = [jnp.exp(vk - vals[0]) for vk in vals]
    denom = exps[0] + exps[1] + exps[2] + exps[3]
    gates = [e / denom for e in exps]

    cnt = (sels[0] | sels[1] | sels[2] | sels[3])
    before = jnp.dot(cnt.astype(BF16), upper_ref[...], preferred_element_type=F32) + carry_ref[:, 0:1]
    ranks = [jnp.sum(jnp.where(sel, before, 0.0), axis=0, keepdims=True) for sel in sels]
    carry_ref[...] = carry_ref[...] + jnp.sum(cnt.astype(F32), axis=1, keepdims=True)
    counts_ref[...] = carry_ref[...]

    meta_ref[...] = jnp.concatenate(idxs + ranks, axis=0).astype(jnp.int32)
    gates_ref[...] = jnp.concatenate(gates + [jnp.zeros((TOP_K, tm), F32)], axis=0)


def _post_call(x, batch0, oT, o_f, o_b, gr, prm, tm):
    B, S, D = x.shape
    nS = S // tm
    upper = jnp.asarray(np.triu(np.ones((tm, tm), np.float32), k=1), BF16)
    full = lambda a: pl.BlockSpec(a.shape, lambda b, i, _nd=a.ndim: (0,) * _nd)
    own = lambda w: pl.BlockSpec((None, tm, w), lambda b, i: (b, i, 0))
    tok = lambda w: pl.BlockSpec((None, tm, w), lambda b, i: (batch0 + b, i, 0))
    colblk = lambda r: pl.BlockSpec((None, r, tm), lambda b, i: (batch0 + b, 0, i))
    flat = lambda r: pl.BlockSpec((r, tm), lambda b, i: (0, b * nS + i))
    return pl.pallas_call(
        _post_kernel, grid=(B, nS),
        in_specs=[own(D), colblk(MLA_WIDTH), tok(GLA_WIDTH), tok(GLA_WIDTH), tok(GLA_WIDTH),
                  full(prm["gla_out_norm"]), full(prm["w_out_a"]), full(prm["w_out_b"]),
                  full(prm["ffn_norm"]), full(prm["router_wT"]), full(prm["router_b"]), full(upper)],
        out_specs=[own(D), own(D), flat(2 * TOP_K), flat(2 * TOP_K),
                   pl.BlockSpec((N_EXPERTS, LANES), lambda b, i: (0, 0))],
        out_shape=[jax.ShapeDtypeStruct((B, S, D), F32), jax.ShapeDtypeStruct((B, S, D), F32),
                   jax.ShapeDtypeStruct((2 * TOP_K, B * S), jnp.int32),
                   jax.ShapeDtypeStruct((2 * TOP_K, B * S), F32),
                   jax.ShapeDtypeStruct((N_EXPERTS, LANES), F32)],
        scratch_shapes=[pltpu.VMEM((N_EXPERTS, LANES), F32)],
        compiler_params=_cparams(("arbitrary", "arbitrary")), name="post",
    )(x, oT, o_f, o_b, gr, prm["gla_out_norm"], prm["w_out_a"], prm["w_out_b"], prm["ffn_norm"],
      prm["router_wT"], prm["router_b"], upper)


def _sc_mesh():
    return plsc.VectorSubcoreMesh(core_axis_name="core", subcore_axis_name="subcore")


def _sc_scatter_rows(x, slots, n_rows):
    T, D = x.shape
    K = slots.shape[0]

    @pl.kernel(out_type=jax.ShapeDtypeStruct((n_rows, D), x.dtype), mesh=_sc_mesh(), scratch_types=[])
    def scatter(x_hbm, slots_hbm, out_hbm):
        def window(x_vmem, slots_vmem):
            for kk in range(K):
                pltpu.sync_copy(x_vmem, out_hbm.at[slots_vmem.at[kk, 0, pl.ds(0, SC_WINDOW)]])

        pltpu.emit_pipeline(
            window, grid=(T // SC_WINDOW,),
            in_specs=[pl.BlockSpec((SC_WINDOW, D), lambda i: (i, 0)),
                      pl.BlockSpec((K, 1, SC_INDEX_PAD), lambda i: (0, i, 0))],
            out_specs=[], core_axis_name=("core", "subcore"),
            dimension_semantics=(pltpu.PARALLEL,))(x_hbm, slots_hbm)

    return scatter(x, slots)


def _sc_gather_rows(y, slots, T):
    D = y.shape[1]
    K = slots.shape[0]

    @pl.kernel(out_type=jax.ShapeDtypeStruct((K, T, D), y.dtype), mesh=_sc_mesh(), scratch_types=[])
    def gather(y_hbm, slots_hbm, out_hbm):
        def window(slots_vmem, out_vmem):
            pltpu.sync_copy(y_hbm.at[slots_vmem.at[0, 0, pl.ds(0, SC_WINDOW)]], out_vmem.at[0])

        pltpu.emit_pipeline(
            window, grid=(K, T // SC_WINDOW),
            in_specs=[pl.BlockSpec((1, 1, SC_INDEX_PAD), lambda kk, i: (kk, i, 0))],
            out_specs=[pl.BlockSpec((1, SC_WINDOW, D), lambda kk, i: (kk, i, 0))],
            core_axis_name=("core", "subcore"),
            dimension_semantics=(pltpu.PARALLEL, pltpu.PARALLEL))(slots_hbm, out_hbm)

    return gather(y, slots)


def _expert_weights_kernel(wgu_ref, wdn_ref, perm_ref, wg_ref, wl_ref, wd_ref):
    perm = perm_ref[...]
    half = PERM_GROUP // 2
    for c in range(2 * D_FF // PERM_GROUP):
        w = wgu_ref[:, c * PERM_GROUP:(c + 1) * PERM_GROUP].astype(BF16)
        sep = jnp.dot(w, perm, preferred_element_type=F32).astype(BF16)
        wg_ref[:, c * half:(c + 1) * half] = sep[:, :half]
        wl_ref[:, c * half:(c + 1) * half] = sep[:, half:]
    wd_ref[...] = wdn_ref[...].astype(BF16)


def _expert_weights_call(w_gu, w_dn):
    E, D, _ = w_gu.shape
    half = PERM_GROUP // 2
    src = np.concatenate([2 * np.arange(half), 2 * np.arange(half) + 1])
    perm = jnp.asarray(np.arange(PERM_GROUP)[:, None] == src[None, :], BF16)
    per_expert = lambda r, c: pl.BlockSpec((None, r, c), lambda e: (e, 0, 0))
    return pl.pallas_call(
        _expert_weights_kernel, grid=(E,),
        in_specs=[per_expert(D, 2 * D_FF), per_expert(D_FF, D),
                  pl.BlockSpec((PERM_GROUP, PERM_GROUP), lambda e: (0, 0))],
        out_specs=[per_expert(D, D_FF), per_expert(D, D_FF), per_expert(D_FF, D)],
        out_shape=[jax.ShapeDtypeStruct((E, D, D_FF), BF16), jax.ShapeDtypeStruct((E, D, D_FF), BF16),
                   jax.ShapeDtypeStruct((E, D_FF, D), BF16)],
        compiler_params=_cparams(("parallel",)), name="expert_weights",
    )(w_gu, w_dn, perm)


def _expert_kernel(be_ref, nv_ref, xs_ref, wg_ref, wl_ref, bg_ref, bl_ref, wd_ref, bd_ref, ys_ref):
    del be_ref
    valid = pl.program_id(0) < nv_ref[0]

    @pl.when(valid)
    def _():
        xb = xs_ref[...].astype(BF16)
        g = jnp.dot(xb, wg_ref[...], preferred_element_type=F32) + bg_ref[...]
        l = jnp.dot(xb, wl_ref[...], preferred_element_type=F32) + bl_ref[...]
        glu = jnp.minimum(g, SWIGLU_LIMIT)
        lin = jnp.clip(l, -SWIGLU_LIMIT, SWIGLU_LIMIT)
        act = glu * jax.nn.sigmoid(SWIGLU_ALPHA * glu) * (lin + 1.0)
        ys_ref[...] = jnp.dot(act.astype(BF16), wd_ref[...], preferred_element_type=F32) + bd_ref[...]

    @pl.when(jnp.logical_not(valid))
    def _():
        ys_ref[...] = jnp.zeros_like(ys_ref)


def _expert_call(block_expert, n_valid, xs, prm):
    n_rows, D = xs.shape
    n_blocks = n_rows // MOE_BLOCK
    rows = lambda i, be, nv: (jnp.minimum(i, nv[0] - 1), 0)
    wsel = lambda i, be, nv: (be[i], 0, 0)
    grid_spec = pltpu.PrefetchScalarGridSpec(
        num_scalar_prefetch=2, grid=(n_blocks,),
        in_specs=[pl.BlockSpec((MOE_BLOCK, D), rows),
                  pl.BlockSpec((None, D, D_FF), wsel), pl.BlockSpec((None, D, D_FF), wsel),
                  pl.BlockSpec((None, 1, D_FF), wsel), pl.BlockSpec((None, 1, D_FF), wsel),
                  pl.BlockSpec((None, D_FF, D), wsel), pl.BlockSpec((None, 1, D), wsel)],
        out_specs=pl.BlockSpec((MOE_BLOCK, D), lambda i, be, nv: (i, 0)))
    return pl.pallas_call(
        _expert_kernel, grid_spec=grid_spec,
        out_shape=jax.ShapeDtypeStruct((n_rows, D), F32),
        compiler_params=_cparams(("arbitrary",)), name="experts",
    )(block_expert, n_valid, xs, prm["w_glu"], prm["w_lin"], prm["b_glu"], prm["b_lin"],
      prm["w_dn"], prm["b_dn"])


def _combine_kernel(xmid_ref, gates_ref, fnorm_ref, y4_ref, out_ref):
    g = jnp.transpose(gates_ref[...])
    y = xmid_ref[...]
    for kk in range(TOP_K):
        y = y + y4_ref[kk] * g[:, kk:kk + 1]
    out_ref[...] = _rms(y, fnorm_ref[...])


def _combine_call(x_mid, gates, final_norm, y4, tc):
    T, D = x_mid.shape
    return pl.pallas_call(
        _combine_kernel, grid=(T // tc,),
        in_specs=[pl.BlockSpec((tc, D), lambda i: (i, 0)),
                  pl.BlockSpec((2 * TOP_K, tc), lambda i: (0, i)),
                  pl.BlockSpec(final_norm.shape, lambda i: (0, 0)),
                  pl.BlockSpec((TOP_K, tc, D), lambda i: (0, i, 0))],
        out_specs=pl.BlockSpec((tc, D), lambda i: (i, 0)),
        out_shape=jax.ShapeDtypeStruct((T, D), F32),
        compiler_params=_cparams(("parallel",)), name="combine",
    )(x_mid, gates, final_norm, y4)


def _prep_params(attn_norm, w_in, mla_q_norm, mla_w_uq, mla_kv_norm, mla_w_ukv, gla_w_gate_fwd,
                 gla_b_gate_fwd, gla_w_gate_bwd, gla_b_gate_bwd, gla_out_norm, w_out, ffn_norm,
                 router_w, router_b, w_gu, b_gu, w_dn, b_dn):
    D = D_MODEL
    o = np.cumsum((0, MLA_Q_LORA, MLA_KV_LORA, MLA_ROPE, GLA_KDIM, GLA_KDIM, GLA_WIDTH,
                   2 * GLA_GATE_RANK, GLA_WIDTH))
    seg = [w_in[:, o[n]:o[n + 1]] for n in range(8)]
    z = lambda n: jnp.zeros((D, n), w_in.dtype)
    w_in_p = jnp.concatenate(
        [seg[0], seg[1], z(MLA_NOPE), seg[2], z(LANES - MLA_NOPE - MLA_ROPE), seg[3], seg[4], seg[5],
         seg[6], z(LANES - 2 * GLA_GATE_RANK), seg[7]], axis=1).astype(BF16)

    wq = mla_w_uq.reshape(MLA_Q_LORA, MLA_HEADS, MLA_NOPE + MLA_ROPE)
    w_uq_p = jnp.concatenate(
        [wq[:, :, :MLA_NOPE].reshape(MLA_Q_LORA, -1),
         wq[:, :, MLA_NOPE:MLA_NOPE + HALF_ROPE].reshape(MLA_Q_LORA, -1),
         wq[:, :, MLA_NOPE + HALF_ROPE:].reshape(MLA_Q_LORA, -1)], axis=1)
    wkv = mla_w_ukv.reshape(MLA_KV_LORA, MLA_HEADS, MLA_NOPE + MLA_V)
    w_k = jnp.concatenate([wkv[:, :, :MLA_NOPE], jnp.zeros((MLA_KV_LORA, MLA_HEADS, QK_PAD - MLA_NOPE),
                                                            wkv.dtype)], axis=2)
    w_v = wkv[:, :, MLA_NOPE:].reshape(MLA_KV_LORA, -1)

    def gate_w(w, row0):
        full = jnp.zeros((LANES, GLA_KDIM), w.dtype)
        return full.at[row0:row0 + GLA_GATE_RANK].set(w).astype(BF16)

    E = N_EXPERTS
    w_glu, w_lin, w_dn_b = _expert_weights_call(w_gu, w_dn)
    return {
        "attn_norm": attn_norm.reshape(1, D), "w_in": w_in_p,
        "q_norm": mla_q_norm.reshape(1, -1), "w_uqT": w_uq_p.T.astype(BF16),
        "kv_norm": mla_kv_norm.reshape(1, -1),
        "w_k": w_k.reshape(MLA_KV_LORA, -1).astype(BF16), "w_vT": w_v.T.astype(BF16),
        "w_gf": gate_w(gla_w_gate_fwd, 0), "b_gf": gla_b_gate_fwd.reshape(1, -1),
        "w_gb": gate_w(gla_w_gate_bwd, GLA_GATE_RANK), "b_gb": gla_b_gate_bwd.reshape(1, -1),
        "gla_out_norm": gla_out_norm.reshape(1, -1),
        "w_out_a": w_out[:MLA_WIDTH].astype(BF16), "w_out_b": w_out[MLA_WIDTH:].astype(BF16),
        "ffn_norm": ffn_norm.reshape(1, D),
        "router_wT": router_w.T, "router_b": router_b.reshape(E, 1),
        "w_glu": w_glu, "w_lin": w_lin, "w_dn": w_dn_b,
        "b_glu": b_gu[:, 0::2].reshape(E, 1, D_FF), "b_lin": b_gu[:, 1::2].reshape(E, 1, D_FF),
        "b_dn": b_dn.reshape(E, 1, D),
    }


def _rope_tables(S):
    inv_freq = jnp.power(ROPE_THETA, -jnp.arange(0, MLA_ROPE, 2, dtype=F32) / MLA_ROPE)
    ang = jnp.arange(S, dtype=F32)[:, None] * inv_freq[None, :]
    cos, sin = jnp.cos(ang), jnp.sin(ang)
    z = lambda n: jnp.zeros((S, n), F32)
    tail = LANES - MLA_NOPE - MLA_ROPE
    return {
        "c": jnp.concatenate([z(MLA_NOPE), cos, cos, z(tail)], axis=1),
        "s1": jnp.concatenate([z(MLA_NOPE), -sin, z(HALF_ROPE), z(tail)], axis=1),
        "s2": jnp.concatenate([z(MLA_NOPE), z(HALF_ROPE), sin, z(tail)], axis=1),
        "cosT": jnp.tile(cos.T, (MLA_HEADS, 1)), "sinT": jnp.tile(sin.T, (MLA_HEADS, 1)),
    }


def _encoder(xa, xb, prm, final_norm):
    n_first, S, D = xa.shape
    B = n_first + xb.shape[0]
    rope = _rope_tables(S)
    qT, k, vT, gq, gk, gv, laf, lab, gr = _pre_call(xa, xb, prm, rope, min(TOKEN_TILE, S))
    oT = _attn_call(qT, k, vT, min(ATTN_QUERY_TILE, S), ATTN_KEY_TILE).reshape(B, MLA_WIDTH, S)
    o_f, o_b = _gla_call(gq, gk, gv, laf, lab)
    mixed = (oT, o_f, o_b, gr)
    if xb.shape[0] > n_first:
        yb, done = _moe_block(xb, n_first, mixed, prm, final_norm)
        ya, _ = _moe_block(xa, 0, mixed, prm, final_norm, after=done)
    else:
        ya, done = _moe_block(xa, 0, mixed, prm, final_norm)
        yb, _ = _moe_block(xb, n_first, mixed, prm, final_norm, after=done)
    return ya, yb


def _moe_block(x, batch0, mixed, prm, final_norm, after=None):
    B, S, D = x.shape
    if after is not None:
        gon, _ = lax.optimization_barrier((prm["gla_out_norm"], after[0]))
        prm = dict(prm, gla_out_norm=gon)
    x_mid, h2, meta, gates, counts = _post_call(x, batch0, *mixed, prm, min(TOKEN_TILE, S))

    T = B * S
    n_rows = T * TOP_K + N_EXPERTS * MOE_BLOCK
    n_blocks = n_rows // MOE_BLOCK
    cnt = counts[:, 0].astype(jnp.int32)
    padded = ((cnt + MOE_BLOCK - 1) // MOE_BLOCK) * MOE_BLOCK
    pends = jnp.cumsum(padded)
    starts = (pends - padded).astype(jnp.int32)
    n_valid = (pends[-1] // MOE_BLOCK).astype(jnp.int32).reshape(1)
    blk = jnp.minimum(jnp.arange(n_blocks, dtype=jnp.int32), n_valid[0] - 1) * MOE_BLOCK
    block_expert = jnp.minimum(jnp.sum(pends[None, :] <= blk[:, None], axis=1), N_EXPERTS - 1).astype(jnp.int32)

    experts = jnp.arange(N_EXPERTS, dtype=jnp.int32)[:, None, None]
    slots = jnp.sum(jnp.where(meta[None, :TOP_K] == experts, starts[:, None, None], 0), axis=0) + meta[TOP_K:]
    slots = jnp.pad(slots.reshape(TOP_K, T // SC_WINDOW, SC_WINDOW),
                    ((0, 0), (0, 0), (0, SC_INDEX_PAD - SC_WINDOW)))

    xs = _sc_scatter_rows(h2.reshape(T, D), slots, n_rows)
    if after is not None:
        n_valid, _ = lax.optimization_barrier((n_valid, after[1]))
    ys = _expert_call(block_expert, n_valid, xs, prm)
    y4 = _sc_gather_rows(ys, slots, T)
    y = _combine_call(x_mid.reshape(T, D), gates, final_norm.reshape(1, D), y4, min(256, S))
    return y.reshape(x.shape), (counts, ys)


def kernel(x_prompt, x_sample, attn_norm, w_in, mla_q_norm, mla_w_uq, mla_kv_norm, mla_w_ukv,
           gla_w_gate_fwd, gla_b_gate_fwd, gla_w_gate_bwd, gla_b_gate_bwd, gla_out_norm, w_out, ffn_norm,
           router_w, router_b, expert_w_gate_up, expert_b_gate_up, expert_w_down, expert_b_down,
           final_norm):
    layer = (attn_norm, w_in, mla_q_norm, mla_w_uq, mla_kv_norm, mla_w_ukv, gla_w_gate_fwd,
             gla_b_gate_fwd, gla_w_gate_bwd, gla_b_gate_bwd, gla_out_norm, w_out, ffn_norm, router_w,
             router_b, expert_w_gate_up, expert_b_gate_up, expert_w_down, expert_b_down)
    assert all(p.shape[0] == 1 for p in layer), "single layer expected"
    assert x_prompt.shape[1:] == x_sample.shape[1:]
    prm = _prep_params(*[p[0] for p in layer])
    return _encoder(x_prompt, x_sample, prm, final_norm)
```

```python
import functools

import jax
import jax.numpy as jnp
import numpy as np
from jax import lax
from jax.experimental import pallas as pl
from jax.experimental.pallas import tpu as pltpu
from jax.experimental.pallas import tpu_sc as plsc

F32 = jnp.float32
BF16 = jnp.bfloat16

D_MODEL = 1024
MLA_HEADS = 8
MLA_NOPE = 64
MLA_ROPE = 32
MLA_V = 64
MLA_Q_LORA = 384
MLA_KV_LORA = 256
ROPE_THETA = 10000.0
GLA_HEADS = 4
GLA_DK = 64
GLA_DV = 128
GLA_GATE_RANK = 16
GLA_GATE_NORM = 16.0
N_EXPERTS = 32
TOP_K = 4
D_FF = 1024
SWIGLU_LIMIT = 7.0
SWIGLU_ALPHA = 1.702
MOE_BLOCK = 512
RMS_EPS = 1e-6

MLA_WIDTH = MLA_HEADS * MLA_V
GLA_WIDTH = GLA_HEADS * GLA_DV
GLA_KDIM = GLA_HEADS * GLA_DK
HALF_ROPE = MLA_ROPE // 2
QK_PAD = 128
LANES = 128
V_ONES_ROWS = 16

OFF_CQ = 0
OFF_CKV = OFF_CQ + MLA_Q_LORA
OFF_KR = OFF_CKV + MLA_KV_LORA
OFF_GQ = OFF_KR + LANES
OFF_GK = OFF_GQ + GLA_KDIM
OFF_GV = OFF_GK + GLA_KDIM
OFF_LR = OFF_GV + GLA_WIDTH
OFF_GR = OFF_LR + LANES
PROJ_COLS = OFF_GR + GLA_WIDTH

TOKEN_TILE = 512
ATTN_KEY_TILE = 256
ATTN_QUERY_TILE = 512
ATTN_LOOKAHEAD = 2
ATTN_SCORE_SLOTS = 4
GLA_TILE = 256
GLA_LEVELS = 8
SC_WINDOW = 32
GATHER_CHUNKS = 2
SC_INDEX_PAD = 128
ATTN_TILES_PER_STEP = 2
PERM_GROUP = 256
VMEM_LIMIT = 56 * 1024 * 1024

LOG2_E = 1.4426950408889634
NT_DIMS = (((1,), (1,)), ((), ()))
TN_DIMS = (((0,), (0,)), ((), ()))


def _cparams(sem):
    return pltpu.CompilerParams(dimension_semantics=sem, vmem_limit_bytes=VMEM_LIMIT)


def _rms(x, gain):
    return x * lax.rsqrt(jnp.mean(x * x, axis=-1, keepdims=True) + RMS_EPS) * gain


def _split_bf16(x):
    hi = x.astype(BF16)
    lo = (x - hi.astype(F32)).astype(BF16)
    return hi, lo


def _pre_kernel(xa_ref, xb_ref, an_ref, win_ref, qn_ref, wuqT_ref, kvn_ref, wk_ref, wvT_ref,
                wgf_ref, bgf_ref, wgb_ref, bgb_ref, rc_ref, rs1_ref, rs2_ref, cosT_ref, sinT_ref,
                qT_ref, k_ref, vT_ref, gq_ref, gk_ref, gv_ref, laf_ref, lab_ref, gr_ref, *, n_first):
    x = jnp.where(pl.program_id(0) < n_first, xa_ref[...], xb_ref[...])
    h = _rms(x, an_ref[...]).astype(BF16)
    proj = jnp.dot(h, win_ref[...], preferred_element_type=F32)

    cqn = _rms(proj[:, OFF_CQ:OFF_CQ + MLA_Q_LORA], qn_ref[...]).astype(BF16)
    ckvn = _rms(proj[:, OFF_CKV:OFF_CKV + MLA_KV_LORA], kvn_ref[...]).astype(BF16)

    scale = (MLA_NOPE + MLA_ROPE) ** -0.5 * LOG2_E
    qT = lax.dot_general(wuqT_ref[...], cqn, NT_DIMS, preferred_element_type=F32) * scale
    n_nope = MLA_HEADS * MLA_NOPE
    n_half = MLA_HEADS * HALF_ROPE
    x1 = qT[n_nope:n_nope + n_half]
    x2 = qT[n_nope + n_half:n_nope + 2 * n_half]
    c = cosT_ref[...]
    s = sinT_ref[...]
    x1r = x1 * c - x2 * s
    x2r = x1 * s + x2 * c
    zpad = jnp.zeros((QK_PAD - MLA_NOPE - MLA_ROPE, qT.shape[1]), BF16)
    for hd in range(MLA_HEADS):
        qT_ref[hd, 0:MLA_NOPE, :] = qT[hd * MLA_NOPE:(hd + 1) * MLA_NOPE].astype(BF16)
        qT_ref[hd, MLA_NOPE:MLA_NOPE + HALF_ROPE, :] = x1r[hd * HALF_ROPE:(hd + 1) * HALF_ROPE].astype(BF16)
        qT_ref[hd, MLA_NOPE + HALF_ROPE:MLA_NOPE + MLA_ROPE, :] = (
            x2r[hd * HALF_ROPE:(hd + 1) * HALF_ROPE].astype(BF16))
        qT_ref[hd, MLA_NOPE + MLA_ROPE:QK_PAD, :] = zpad

    kr = proj[:, OFF_KR:OFF_KR + LANES]
    kr = (kr * rc_ref[...]
          + pltpu.roll(kr, LANES - HALF_ROPE, axis=1) * rs1_ref[...]
          + pltpu.roll(kr, HALF_ROPE, axis=1) * rs2_ref[...])
    kfull = jnp.dot(ckvn, wk_ref[...], preferred_element_type=F32)
    for hd in range(MLA_HEADS):
        k_ref[hd] = (kfull[:, hd * QK_PAD:(hd + 1) * QK_PAD] + kr).astype(BF16)

    vT = lax.dot_general(wvT_ref[...], ckvn, NT_DIMS, preferred_element_type=F32)
    for hd in range(MLA_HEADS):
        vT_ref[hd] = vT[hd * MLA_V:(hd + 1) * MLA_V].astype(BF16)

    gq_ref[...] = proj[:, OFF_GQ:OFF_GQ + GLA_KDIM] * (GLA_DK ** -0.5)
    gk_ref[...] = proj[:, OFF_GK:OFF_GK + GLA_KDIM]
    gv_ref[...] = proj[:, OFF_GV:OFF_GV + GLA_WIDTH]
    gr_ref[...] = proj[:, OFF_GR:OFF_GR + GLA_WIDTH]
    lr = proj[:, OFF_LR:OFF_LR + LANES].astype(BF16)

    def log_decay(w_ref, b_ref):
        z = jnp.dot(lr, w_ref[...], preferred_element_type=F32) + b_ref[...]
        return (jnp.minimum(z, 0.0) - jnp.log1p(jnp.exp(-jnp.abs(z)))) * (LOG2_E / GLA_GATE_NORM)

    laf_ref[...] = log_decay(wgf_ref, bgf_ref)
    lab_ref[...] = log_decay(wgb_ref, bgb_ref)


def _two_source_specs(n_first, tm, width):
    return [pl.BlockSpec((None, tm, width), lambda b, i: (jnp.minimum(b, n_first - 1), i, 0)),
            pl.BlockSpec((None, tm, width), lambda b, i: (jnp.maximum(b - n_first, 0), i, 0))]


def _pre_call(xa, xb, prm, rope, tm):
    n_first, S, D = xa.shape
    B = n_first + xb.shape[0]
    nS = S // tm
    H = MLA_HEADS

    def full(a):
        nd = a.ndim
        return pl.BlockSpec(a.shape, lambda b, i, _nd=nd: (0,) * _nd)

    tok = lambda w: pl.BlockSpec((None, tm, w), lambda b, i: (b, i, 0))
    in_specs = [
        *_two_source_specs(n_first, tm, D), full(prm["attn_norm"]), full(prm["w_in"]), full(prm["q_norm"]),
        full(prm["w_uqT"]), full(prm["kv_norm"]), full(prm["w_k"]), full(prm["w_vT"]),
        full(prm["w_gf"]), full(prm["b_gf"]), full(prm["w_gb"]), full(prm["b_gb"]),
        pl.BlockSpec((tm, LANES), lambda b, i: (i, 0)),
        pl.BlockSpec((tm, LANES), lambda b, i: (i, 0)),
        pl.BlockSpec((tm, LANES), lambda b, i: (i, 0)),
        pl.BlockSpec((H * HALF_ROPE, tm), lambda b, i: (0, i)),
        pl.BlockSpec((H * HALF_ROPE, tm), lambda b, i: (0, i)),
    ]
    out_shape = [
        jax.ShapeDtypeStruct((B, H, QK_PAD, S), BF16),
        jax.ShapeDtypeStruct((B, H, S, QK_PAD), BF16),
        jax.ShapeDtypeStruct((B, H, nS, MLA_V, tm), BF16),
        jax.ShapeDtypeStruct((B, S, GLA_KDIM), F32),
        jax.ShapeDtypeStruct((B, S, GLA_KDIM), F32),
        jax.ShapeDtypeStruct((B, S, GLA_WIDTH), F32),
        jax.ShapeDtypeStruct((B, S, GLA_KDIM), F32),
        jax.ShapeDtypeStruct((B, S, GLA_KDIM), F32),
        jax.ShapeDtypeStruct((B, S, GLA_WIDTH), F32),
    ]
    out_specs = [
        pl.BlockSpec((None, H, QK_PAD, tm), lambda b, i: (b, 0, 0, i)),
        pl.BlockSpec((None, H, tm, QK_PAD), lambda b, i: (b, 0, i, 0)),
        pl.BlockSpec((None, H, None, MLA_V, tm), lambda b, i: (b, 0, i, 0, 0)),
        tok(GLA_KDIM), tok(GLA_KDIM), tok(GLA_WIDTH), tok(GLA_KDIM), tok(GLA_KDIM), tok(GLA_WIDTH),
    ]
    return pl.pallas_call(
        functools.partial(_pre_kernel, n_first=n_first), grid=(B, nS),
        in_specs=in_specs, out_specs=out_specs, out_shape=out_shape,
        compiler_params=_cparams(("parallel", "parallel")), name="pre",
    )(xa, xb, prm["attn_norm"], prm["w_in"], prm["q_norm"], prm["w_uqT"], prm["kv_norm"], prm["w_k"],
      prm["w_vT"], prm["w_gf"], prm["b_gf"], prm["w_gb"], prm["b_gb"],
      rope["c"], rope["s1"], rope["s2"], rope["cosT"], rope["sinT"])


def _attn_kernel(qT_ref, k_ref, vT_ref, o_ref, s_ref, *, n_kblk, tiles):
    tq = qT_ref.shape[1] // tiles
    tk = k_ref.shape[0] // n_kblk
    per_tile = vT_ref.shape[2] // tk
    ones = jnp.ones((V_ONES_ROWS, tk), BF16)

    def values(j, p):
        v_blk = vT_ref[j // per_tile, :, (j % per_tile) * tk:(j % per_tile + 1) * tk]
        vext = jnp.concatenate([v_blk, ones], axis=0)
        return jnp.dot(vext, p, preferred_element_type=F32)

    for t in range(tiles):
        qT = qT_ref[:, t * tq:(t + 1) * tq]

        def scores(n):
            j = n % n_kblk
            sT = jnp.dot(k_ref[j * tk:(j + 1) * tk, :], qT, preferred_element_type=F32)
            s_ref[n % ATTN_SCORE_SLOTS] = sT
            return jnp.max(sT, axis=0, keepdims=True)

        first = t * n_kblk
        m = jnp.full((1, tq), -jnp.inf, F32)
        acc = jnp.zeros((MLA_V + V_ONES_ROWS, tq), F32)
        blk_max = {first + j: scores(first + j) for j in range(ATTN_LOOKAHEAD)}
        for n in range(first, first + n_kblk):
            if n + ATTN_LOOKAHEAD < first + n_kblk:
                blk_max[n + ATTN_LOOKAHEAD] = scores(n + ATTN_LOOKAHEAD)
            m_new = jnp.maximum(m, blk_max.pop(n))
            p = jnp.exp2(s_ref[n % ATTN_SCORE_SLOTS] - m_new).astype(BF16)
            acc = jnp.exp2(m - m_new) * acc + values(n % n_kblk, p)
            m = m_new
        o_ref[:, t * tq:(t + 1) * tq] = (acc[0:MLA_V] / acc[MLA_V:MLA_V + 1]).astype(o_ref.dtype)


def _attn_call(qT, k, vT, tq, tk):
    B, H, _, S = qT.shape
    n_vblk, v_tile = vT.shape[2], vT.shape[4]
    n_kblk = S // tk
    tiles = ATTN_TILES_PER_STEP if S % (tq * ATTN_TILES_PER_STEP) == 0 else 1
    assert n_kblk >= ATTN_LOOKAHEAD and S % (tq * tiles) == 0 and v_tile % tk == 0
    return pl.pallas_call(
        functools.partial(_attn_kernel, n_kblk=n_kblk, tiles=tiles),
        grid=(B, H, S // (tq * tiles)),
        in_specs=[
            pl.BlockSpec((None, None, QK_PAD, tq * tiles), lambda b, h, i: (b, h, 0, i)),
            pl.BlockSpec((None, None, S, QK_PAD), lambda b, h, i: (b, h, 0, 0)),
            pl.BlockSpec((None, None, n_vblk, MLA_V, v_tile), lambda b, h, i: (b, h, 0, 0, 0)),
        ],
        out_specs=pl.BlockSpec((None, None, MLA_V, tq * tiles), lambda b, h, i: (b, h, 0, i)),
        out_shape=jax.ShapeDtypeStruct((B, H, MLA_V, S), BF16),
        scratch_shapes=[pltpu.VMEM((ATTN_SCORE_SLOTS, tk, tq), F32)],
        compiler_params=_cparams(("parallel", "parallel", "parallel")), name="attn",
    )(qT, k, vT)


def _gla_consts():
    L = GLA_TILE
    i = np.arange(L)[:, None]
    j = np.arange(L)[None, :]
    x = i ^ j
    lidx = np.where(x > 0, np.floor(np.log2(np.maximum(x, 1))), -1).astype(np.int32)
    lidx_f = np.where(i > j, lidx, -1).astype(np.int32)
    lidx_b = np.where(i < j, lidx, -1).astype(np.int32)
    hd = np.arange(GLA_KDIM)[:, None] // GLA_DK
    hv = np.arange(GLA_WIDTH)[None, :] // GLA_DV
    bexp = (hd == hv).astype(np.float32)
    return jnp.asarray(lidx_f), jnp.asarray(lidx_b), jnp.asarray(bexp, BF16)


def _gla_direction(q, k, v, la, lidx, bexp_ref, ssum_ref, state_ref, o_ref, forward):
    L = GLA_TILE
    la_hi, la_lo = _split_bf16(la)
    vb = v.astype(BF16)
    lane_head = lax.broadcasted_iota(jnp.int32, (1, GLA_KDIM), 1) // GLA_DK
    head_masks = [(lane_head == h).astype(BF16) for h in range(GLA_HEADS)]
    row = lax.broadcasted_iota(jnp.int32, la.shape, 0)

    def widen(c, t, m):
        upper_half = (row & m) != 0
        sibling_total = jnp.where(upper_half, pltpu.roll(t, m, axis=0), pltpu.roll(t, L - m, axis=0))
        return c + jnp.where(upper_half, sibling_total, 0.0), t + sibling_total

    c, t = (la if forward else jnp.zeros_like(la)), la
    for lvl in range(GLA_LEVELS):
        if lvl > 0:
            c, t = widen(c, t, 1 << (lvl - 1))
        if forward:
            eq, ek = c, t - c
        else:
            eq, ek = t - c, c
        ql = (q * jnp.exp2(eq)).astype(BF16)
        kl = (k * jnp.exp2(ek)).astype(BF16)
        sel = lidx == lvl
        for h in range(GLA_HEADS):
            sc = lax.dot_general(ql * head_masks[h], kl, NT_DIMS, preferred_element_type=F32)
            ssum_ref[h] = jnp.where(sel, sc, ssum_ref[h] if lvl > 0 else 0.0)

    c, t = widen(c, t, L // 2)
    if forward:
        eq, ek = c, t - c
    else:
        eq, ek = t - c, c
    q_in = (q * jnp.exp2(eq)).astype(BF16)
    k_out = (k * jnp.exp2(ek)).astype(BF16)
    o = jnp.dot(q_in, state_ref[...].astype(BF16), preferred_element_type=F32)
    if forward:
        o = o + jnp.dot((q * k).astype(BF16), bexp_ref[...], preferred_element_type=F32) * v
    for h in range(GLA_HEADS):
        oh = jnp.dot(ssum_ref[h].astype(BF16), vb[:, h * GLA_DV:(h + 1) * GLA_DV],
                     preferred_element_type=F32)
        o_ref[:, h * GLA_DV:(h + 1) * GLA_DV] = o[:, h * GLA_DV:(h + 1) * GLA_DV] + oh

    ones = jnp.ones((L, LANES), BF16)
    tot_col = (lax.dot_general(la_hi, ones, TN_DIMS, preferred_element_type=F32)
               + lax.dot_general(la_lo, ones, TN_DIMS, preferred_element_type=F32))
    dec = jnp.exp2(tot_col)
    upd = lax.dot_general(k_out, vb, TN_DIMS, preferred_element_type=F32)
    for h in range(GLA_HEADS):
        cols = slice(h * GLA_DV, (h + 1) * GLA_DV)
        state_ref[:, cols] = dec * state_ref[:, cols] + upd[:, cols] * bexp_ref[:, cols].astype(F32)


def _gla_kernel(qf_ref, kf_ref, vf_ref, laf_ref, qb_ref, kb_ref, vb_ref, lab_ref,
                lidxf_ref, lidxb_ref, bexp_ref, of_ref, ob_ref, sf_ref, sb_ref, ssum_ref):
    @pl.when(pl.program_id(1) == 0)
    def _():
        sf_ref[...] = jnp.zeros_like(sf_ref)
        sb_ref[...] = jnp.zeros_like(sb_ref)

    _gla_direction(qf_ref[...], kf_ref[...], vf_ref[...], laf_ref[...], lidxf_ref[...],
                   bexp_ref, ssum_ref, sf_ref, of_ref, True)
    _gla_direction(qb_ref[...], kb_ref[...], vb_ref[...], lab_ref[...], lidxb_ref[...],
                   bexp_ref, ssum_ref, sb_ref, ob_ref, False)


def _gla_call(gq, gk, gv, laf, lab):
    B, S, _ = gq.shape
    L = GLA_TILE
    n = S // L
    lidx_f, lidx_b, bexp = _gla_consts()
    fwd = lambda w: pl.BlockSpec((None, L, w), lambda b, i: (b, i, 0))
    bwd = lambda w: pl.BlockSpec((None, L, w), lambda b, i: (b, n - 1 - i, 0))
    const = lambda a: pl.BlockSpec(a.shape, lambda b, i, _nd=a.ndim: (0,) * _nd)
    return pl.pallas_call(
        _gla_kernel, grid=(B, n),
        in_specs=[fwd(GLA_KDIM), fwd(GLA_KDIM), fwd(GLA_WIDTH), fwd(GLA_KDIM),
                  bwd(GLA_KDIM), bwd(GLA_KDIM), bwd(GLA_WIDTH), bwd(GLA_KDIM),
                  const(lidx_f), const(lidx_b), const(bexp)],
        out_specs=[fwd(GLA_WIDTH), bwd(GLA_WIDTH)],
        out_shape=[jax.ShapeDtypeStruct((B, S, GLA_WIDTH), F32)] * 2,
        scratch_shapes=[pltpu.VMEM((GLA_KDIM, GLA_WIDTH), F32), pltpu.VMEM((GLA_KDIM, GLA_WIDTH), F32),
                        pltpu.VMEM((GLA_HEADS, L, L), F32)],
        compiler_params=_cparams(("parallel", "arbitrary")), name="gla",
    )(gq, gk, gv, laf, gq, gk, gv, lab, lidx_f, lidx_b, bexp)


def _post_kernel(x_ref, oT_ref, of_ref, ob_ref, gr_ref, gon_ref, woa_ref, wob_ref, fn_ref,
                 rwT_ref, rb_ref, upper_ref,
                 xmid_ref, h2_ref, meta_ref, gates_ref, counts_ref, carry_ref):
    first = (pl.program_id(0) == 0) & (pl.program_id(1) == 0)

    @pl.when(first)
    def _():
        carry_ref[...] = jnp.zeros_like(carry_ref)

    o = of_ref[...] + ob_ref[...]
    gr = gr_ref[...]
    parts = []
    for h in range(GLA_HEADS):
        cols = slice(h * GLA_DV, (h + 1) * GLA_DV)
        parts.append(_rms(o[:, cols], gon_ref[...]) * jax.nn.silu(gr[:, cols]))
    gla = jnp.concatenate(parts, axis=1).astype(BF16)

    x_mid = (x_ref[...]
             + lax.dot_general(oT_ref[...], woa_ref[...], TN_DIMS, preferred_element_type=F32)
             + jnp.dot(gla, wob_ref[...], preferred_element_type=F32))
    xmid_ref[...] = x_mid
    h2 = _rms(x_mid, fn_ref[...])
    h2_ref[...] = h2

    h_hi, h_lo = _split_bf16(h2)
    w_hi, w_lo = _split_bf16(rwT_ref[...])
    by_h_hi = lax.dot_general(jnp.concatenate([w_hi, w_lo], axis=0), h_hi, NT_DIMS,
                              preferred_element_type=F32)
    logits = (by_h_hi[:N_EXPERTS] + by_h_hi[N_EXPERTS:]
              + lax.dot_general(w_hi, h_lo, NT_DIMS, preferred_element_type=F32)
              + rb_ref[...])
    tm = logits.shape[1]
    eidx = lax.broadcasted_iota(jnp.int32, (N_EXPERTS, tm), 0).astype(F32)
    vals, idxs, sels = [], [], []
    cur = logits
    for _ in range(TOP_K):
        mk = jnp.max(cur, axis=0, keepdims=True)
        ik = jnp.min(jnp.where(cur == mk, eidx, float(N_EXPERTS)), axis=0, keepdims=True)
        sel = eidx == ik
        vals.append(mk)
        idxs.append(ik)
        sels.append(sel)
        cur = jnp.where(sel, -jnp.inf, cur)
    exps = [jnp.exp(vk - vals[0]) for vk in vals]
    denom = exps[0] + exps[1] + exps[2] + exps[3]
    gates = [e / denom for e in exps]

    cnt = (sels[0] | sels[1] | sels[2] | sels[3])
    before = jnp.dot(cnt.astype(BF16), upper_ref[...], preferred_element_type=F32) + carry_ref[:, 0:1]
    ranks = [jnp.sum(jnp.where(sel, before, 0.0), axis=0, keepdims=True) for sel in sels]
    carry_ref[...] = carry_ref[...] + jnp.sum(cnt.astype(F32), axis=1, keepdims=True)
    counts_ref[...] = carry_ref[...]

    meta_ref[...] = jnp.concatenate(idxs + ranks, axis=0).astype(jnp.int32)
    gates_ref[...] = jnp.concatenate(gates + [jnp.zeros((TOP_K, tm), F32)], axis=0)


def _post_call(x, batch0, oT, o_f, o_b, gr, prm, tm):
    B, S, D = x.shape
    nS = S // tm
    upper = jnp.asarray(np.triu(np.ones((tm, tm), np.float32), k=1), BF16)
    full = lambda a: pl.BlockSpec(a.shape, lambda b, i, _nd=a.ndim: (0,) * _nd)
    own = lambda w: pl.BlockSpec((None, tm, w), lambda b, i: (b, i, 0))
    tok = lambda w: pl.BlockSpec((None, tm, w), lambda b, i: (batch0 + b, i, 0))
    colblk = lambda r: pl.BlockSpec((None, r, tm), lambda b, i: (batch0 + b, 0, i))
    flat = lambda r: pl.BlockSpec((r, tm), lambda b, i: (0, b * nS + i))
    return pl.pallas_call(
        _post_kernel, grid=(B, nS),
        in_specs=[own(D), colblk(MLA_WIDTH), tok(GLA_WIDTH), tok(GLA_WIDTH), tok(GLA_WIDTH),
                  full(prm["gla_out_norm"]), full(prm["w_out_a"]), full(prm["w_out_b"]),
                  full(prm["ffn_norm"]), full(prm["router_wT"]), full(prm["router_b"]), full(upper)],
        out_specs=[own(D), own(D), flat(2 * TOP_K), flat(2 * TOP_K),
                   pl.BlockSpec((N_EXPERTS, LANES), lambda b, i: (0, 0))],
        out_shape=[jax.ShapeDtypeStruct((B, S, D), F32), jax.ShapeDtypeStruct((B, S, D), F32),
                   jax.ShapeDtypeStruct((2 * TOP_K, B * S), jnp.int32),
                   jax.ShapeDtypeStruct((2 * TOP_K, B * S), F32),
                   jax.ShapeDtypeStruct((N_EXPERTS, LANES), F32)],
        scratch_shapes=[pltpu.VMEM((N_EXPERTS, LANES), F32)],
        compiler_params=_cparams(("arbitrary", "arbitrary")), name="post",
    )(x, oT, o_f, o_b, gr, prm["gla_out_norm"], prm["w_out_a"], prm["w_out_b"], prm["ffn_norm"],
      prm["router_wT"], prm["router_b"], upper)


def _sc_mesh():
    return plsc.VectorSubcoreMesh(core_axis_name="core", subcore_axis_name="subcore")


def _sc_scatter_rows(x, slots, n_rows):
    T, D = x.shape
    K = slots.shape[0]

    @pl.kernel(out_type=jax.ShapeDtypeStruct((n_rows, D), x.dtype), mesh=_sc_mesh(), scratch_types=[])
    def scatter(x_hbm, slots_hbm, out_hbm):
        def window(x_vmem, slots_vmem):
            for kk in range(K):
                pltpu.sync_copy(x_vmem, out_hbm.at[slots_vmem.at[kk, 0, pl.ds(0, SC_WINDOW)]])

        pltpu.emit_pipeline(
            window, grid=(T // SC_WINDOW,),
            in_specs=[pl.BlockSpec((SC_WINDOW, D), lambda i: (i, 0)),
                      pl.BlockSpec((K, 1, SC_INDEX_PAD), lambda i: (0, i, 0))],
            out_specs=[], core_axis_name=("core", "subcore"),
            dimension_semantics=(pltpu.PARALLEL,))(x_hbm, slots_hbm)

    return scatter(x, slots)


def _sc_gather_rows(y, slots, T):
    D = y.shape[1]
    K = slots.shape[0]

    @pl.kernel(out_type=jax.ShapeDtypeStruct((K, T, D), y.dtype), mesh=_sc_mesh(), scratch_types=[])
    def gather(y_hbm, slots_hbm, out_hbm):
        def window(slots_vmem, out_vmem):
            pltpu.sync_copy(y_hbm.at[slots_vmem.at[0, 0, pl.ds(0, SC_WINDOW)]], out_vmem.at[0])

        pltpu.emit_pipeline(
            window, grid=(K, T // SC_WINDOW),
            in_specs=[pl.BlockSpec((1, 1, SC_INDEX_PAD), lambda kk, i: (kk, i, 0))],
            out_specs=[pl.BlockSpec((1, SC_WINDOW, D), lambda kk, i: (kk, i, 0))],
            core_axis_name=("core", "subcore"),
            dimension_semantics=(pltpu.PARALLEL, pltpu.PARALLEL))(slots_hbm, out_hbm)

    return gather(y, slots)


def _expert_weights_kernel(wgu_ref, wdn_ref, perm_ref, wg_ref, wl_ref, wd_ref):
    perm = perm_ref[...]
    half = PERM_GROUP // 2
    for c in range(2 * D_FF // PERM_GROUP):
        w = wgu_ref[:, c * PERM_GROUP:(c + 1) * PERM_GROUP].astype(BF16)
        sep = jnp.dot(w, perm, preferred_element_type=F32).astype(BF16)
        wg_ref[:, c * half:(c + 1) * half] = sep[:, :half]
        wl_ref[:, c * half:(c + 1) * half] = sep[:, half:]
    wd_ref[...] = wdn_ref[...].astype(BF16)


def _expert_weights_call(w_gu, w_dn):
    E, D, _ = w_gu.shape
    half = PERM_GROUP // 2
    src = np.concatenate([2 * np.arange(half), 2 * np.arange(half) + 1])
    perm = jnp.asarray(np.arange(PERM_GROUP)[:, None] == src[None, :], BF16)
    per_expert = lambda r, c: pl.BlockSpec((None, r, c), lambda e: (e, 0, 0))
    return pl.pallas_call(
        _expert_weights_kernel, grid=(E,),
        in_specs=[per_expert(D, 2 * D_FF), per_expert(D_FF, D),
                  pl.BlockSpec((PERM_GROUP, PERM_GROUP), lambda e: (0, 0))],
        out_specs=[per_expert(D, D_FF), per_expert(D, D_FF), per_expert(D_FF, D)],
        out_shape=[jax.ShapeDtypeStruct((E, D, D_FF), BF16), jax.ShapeDtypeStruct((E, D, D_FF), BF16),
                   jax.ShapeDtypeStruct((E, D_FF, D), BF16)],
        compiler_params=_cparams(("parallel",)), name="expert_weights",
    )(w_gu, w_dn, perm)


def _expert_kernel(be_ref, nv_ref, xs_ref, wg_ref, wl_ref, bg_ref, bl_ref, wd_ref, bd_ref, ys_ref):
    del be_ref
    valid = pl.program_id(0) < nv_ref[0]

    @pl.when(valid)
    def _():
        xb = xs_ref[...].astype(BF16)
        g = jnp.dot(xb, wg_ref[...], preferred_element_type=F32) + bg_ref[...]
        l = jnp.dot(xb, wl_ref[...], preferred_element_type=F32) + bl_ref[...]
        glu = jnp.minimum(g, SWIGLU_LIMIT)
        lin = jnp.clip(l, -SWIGLU_LIMIT, SWIGLU_LIMIT)
        act = glu * jax.nn.sigmoid(SWIGLU_ALPHA * glu) * (lin + 1.0)
        ys_ref[...] = jnp.dot(act.astype(BF16), wd_ref[...], preferred_element_type=F32) + bd_ref[...]

    @pl.when(jnp.logical_not(valid))
    def _():
        ys_ref[...] = jnp.zeros_like(ys_ref)


def _expert_call(block_expert, n_valid, xs, prm):
    n_rows, D = xs.shape
    n_blocks = n_rows // MOE_BLOCK
    rows = lambda i, be, nv: (jnp.minimum(i, nv[0] - 1), 0)
    wsel = lambda i, be, nv: (be[i], 0, 0)
    grid_spec = pltpu.PrefetchScalarGridSpec(
        num_scalar_prefetch=2, grid=(n_blocks,),
        in_specs=[pl.BlockSpec((MOE_BLOCK, D), rows),
                  pl.BlockSpec((None, D, D_FF), wsel), pl.BlockSpec((None, D, D_FF), wsel),
                  pl.BlockSpec((None, 1, D_FF), wsel), pl.BlockSpec((None, 1, D_FF), wsel),
                  pl.BlockSpec((None, D_FF, D), wsel), pl.BlockSpec((None, 1, D), wsel)],
        out_specs=pl.BlockSpec((MOE_BLOCK, D), lambda i, be, nv: (i, 0)))
    return pl.pallas_call(
        _expert_kernel, grid_spec=grid_spec,
        out_shape=jax.ShapeDtypeStruct((n_rows, D), F32),
        compiler_params=_cparams(("arbitrary",)), name="experts",
    )(block_expert, n_valid, xs, prm["w_glu"], prm["w_lin"], prm["b_glu"], prm["b_lin"],
      prm["w_dn"], prm["b_dn"])


def _combine_kernel(xmid_ref, gates_ref, fnorm_ref, y4_ref, *rest):
    out_ref = rest[-1]
    g = jnp.transpose(gates_ref[...])
    y = xmid_ref[...]
    for kk in range(TOP_K):
        y = y + y4_ref[kk] * g[:, kk:kk + 1]
    out_ref[...] = _rms(y, fnorm_ref[...])


def _combine_call(x_mid, gates, final_norm, y4, tc, tile0=0, partial=None):
    T, D = x_mid.shape
    n_tiles = y4.shape[1] // tc
    in_specs = [pl.BlockSpec((tc, D), lambda i: (tile0 + i, 0)),
                pl.BlockSpec((2 * TOP_K, tc), lambda i: (0, tile0 + i)),
                pl.BlockSpec(final_norm.shape, lambda i: (0, 0)),
                pl.BlockSpec((TOP_K, tc, D), lambda i: (0, i, 0))]
    args = [x_mid, gates, final_norm, y4]
    aliases = {}
    if partial is not None:
        in_specs.append(pl.BlockSpec(memory_space=pl.ANY))
        args.append(partial)
        aliases = {4: 0}
    return pl.pallas_call(
        _combine_kernel, grid=(n_tiles,), in_specs=in_specs,
        out_specs=pl.BlockSpec((tc, D), lambda i: (tile0 + i, 0)),
        out_shape=jax.ShapeDtypeStruct((T, D), F32), input_output_aliases=aliases,
        compiler_params=_cparams(("parallel",)), name="combine",
    )(*args)


def _prep_params(attn_norm, w_in, mla_q_norm, mla_w_uq, mla_kv_norm, mla_w_ukv, gla_w_gate_fwd,
                 gla_b_gate_fwd, gla_w_gate_bwd, gla_b_gate_bwd, gla_out_norm, w_out, ffn_norm,
                 router_w, router_b, w_gu, b_gu, w_dn, b_dn):
    D = D_MODEL
    o = np.cumsum((0, MLA_Q_LORA, MLA_KV_LORA, MLA_ROPE, GLA_KDIM, GLA_KDIM, GLA_WIDTH,
                   2 * GLA_GATE_RANK, GLA_WIDTH))
    seg = [w_in[:, o[n]:o[n + 1]] for n in range(8)]
    z = lambda n: jnp.zeros((D, n), w_in.dtype)
    w_in_p = jnp.concatenate(
        [seg[0], seg[1], z(MLA_NOPE), seg[2], z(LANES - MLA_NOPE - MLA_ROPE), seg[3], seg[4], seg[5],
         seg[6], z(LANES - 2 * GLA_GATE_RANK), seg[7]], axis=1).astype(BF16)

    wq = mla_w_uq.reshape(MLA_Q_LORA, MLA_HEADS, MLA_NOPE + MLA_ROPE)
    w_uq_p = jnp.concatenate(
        [wq[:, :, :MLA_NOPE].reshape(MLA_Q_LORA, -1),
         wq[:, :, MLA_NOPE:MLA_NOPE + HALF_ROPE].reshape(MLA_Q_LORA, -1),
         wq[:, :, MLA_NOPE + HALF_ROPE:].reshape(MLA_Q_LORA, -1)], axis=1)
    wkv = mla_w_ukv.reshape(MLA_KV_LORA, MLA_HEADS, MLA_NOPE + MLA_V)
    w_k = jnp.concatenate([wkv[:, :, :MLA_NOPE], jnp.zeros((MLA_KV_LORA, MLA_HEADS, QK_PAD - MLA_NOPE),
                                                            wkv.dtype)], axis=2)
    w_v = wkv[:, :, MLA_NOPE:].reshape(MLA_KV_LORA, -1)

    def gate_w(w, row0):
        full = jnp.zeros((LANES, GLA_KDIM), w.dtype)
        return full.at[row0:row0 + GLA_GATE_RANK].set(w).astype(BF16)

    E = N_EXPERTS
    w_glu, w_lin, w_dn_b = _expert_weights_call(w_gu, w_dn)
    return {
        "attn_norm": attn_norm.reshape(1, D), "w_in": w_in_p,
        "q_norm": mla_q_norm.reshape(1, -1), "w_uqT": w_uq_p.T.astype(BF16),
        "kv_norm": mla_kv_norm.reshape(1, -1),
        "w_k": w_k.reshape(MLA_KV_LORA, -1).astype(BF16), "w_vT": w_v.T.astype(BF16),
        "w_gf": gate_w(gla_w_gate_fwd, 0), "b_gf": gla_b_gate_fwd.reshape(1, -1),
        "w_gb": gate_w(gla_w_gate_bwd, GLA_GATE_RANK), "b_gb": gla_b_gate_bwd.reshape(1, -1),
        "gla_out_norm": gla_out_norm.reshape(1, -1),
        "w_out_a": w_out[:MLA_WIDTH].astype(BF16), "w_out_b": w_out[MLA_WIDTH:].astype(BF16),
        "ffn_norm": ffn_norm.reshape(1, D),
        "router_wT": router_w.T, "router_b": router_b.reshape(E, 1),
        "w_glu": w_glu, "w_lin": w_lin, "w_dn": w_dn_b,
        "b_glu": b_gu[:, 0::2].reshape(E, 1, D_FF), "b_lin": b_gu[:, 1::2].reshape(E, 1, D_FF),
        "b_dn": b_dn.reshape(E, 1, D),
    }


def _rope_tables(S):
    inv_freq = jnp.power(ROPE_THETA, -jnp.arange(0, MLA_ROPE, 2, dtype=F32) / MLA_ROPE)
    ang = jnp.arange(S, dtype=F32)[:, None] * inv_freq[None, :]
    cos, sin = jnp.cos(ang), jnp.sin(ang)
    z = lambda n: jnp.zeros((S, n), F32)
    tail = LANES - MLA_NOPE - MLA_ROPE
    return {
        "c": jnp.concatenate([z(MLA_NOPE), cos, cos, z(tail)], axis=1),
        "s1": jnp.concatenate([z(MLA_NOPE), -sin, z(HALF_ROPE), z(tail)], axis=1),
        "s2": jnp.concatenate([z(MLA_NOPE), z(HALF_ROPE), sin, z(tail)], axis=1),
        "cosT": jnp.tile(cos.T, (MLA_HEADS, 1)), "sinT": jnp.tile(sin.T, (MLA_HEADS, 1)),
    }


def _encoder(xa, xb, prm, final_norm):
    n_first, S, D = xa.shape
    B = n_first + xb.shape[0]
    rope = _rope_tables(S)
    qT, k, vT, gq, gk, gv, laf, lab, gr = _pre_call(xa, xb, prm, rope, min(TOKEN_TILE, S))
    oT = _attn_call(qT, k, vT, min(ATTN_QUERY_TILE, S), ATTN_KEY_TILE).reshape(B, MLA_WIDTH, S)
    o_f, o_b = _gla_call(gq, gk, gv, laf, lab)
    mixed = (oT, o_f, o_b, gr)
    return (_moe_block(xa, 0, mixed, prm, final_norm), _moe_block(xb, n_first, mixed, prm, final_norm))


def _moe_block(x, batch0, mixed, prm, final_norm):
    B, S, D = x.shape
    x_mid, h2, meta, gates, counts = _post_call(x, batch0, *mixed, prm, min(TOKEN_TILE, S))

    T = B * S
    n_rows = T * TOP_K + N_EXPERTS * MOE_BLOCK
    n_blocks = n_rows // MOE_BLOCK
    cnt = counts[:, 0].astype(jnp.int32)
    padded = ((cnt + MOE_BLOCK - 1) // MOE_BLOCK) * MOE_BLOCK
    pends = jnp.cumsum(padded)
    starts = (pends - padded).astype(jnp.int32)
    n_valid = (pends[-1] // MOE_BLOCK).astype(jnp.int32).reshape(1)
    blk = jnp.minimum(jnp.arange(n_blocks, dtype=jnp.int32), n_valid[0] - 1) * MOE_BLOCK
    block_expert = jnp.minimum(jnp.sum(pends[None, :] <= blk[:, None], axis=1), N_EXPERTS - 1).astype(jnp.int32)

    experts = jnp.arange(N_EXPERTS, dtype=jnp.int32)[:, None, None]
    slots = jnp.sum(jnp.where(meta[None, :TOP_K] == experts, starts[:, None, None], 0), axis=0) + meta[TOP_K:]
    slots = jnp.pad(slots.reshape(TOP_K, T // SC_WINDOW, SC_WINDOW),
                    ((0, 0), (0, 0), (0, SC_INDEX_PAD - SC_WINDOW)))

    xs = _sc_scatter_rows(h2.reshape(T, D), slots, n_rows)
    ys = _expert_call(block_expert, n_valid, xs, prm)
    tc = min(256, S)
    chunk = T // GATHER_CHUNKS
    windows = chunk // SC_WINDOW
    y = None
    for c in range(GATHER_CHUNKS):
        y4 = _sc_gather_rows(ys, slots[:, c * windows:(c + 1) * windows], chunk)
        y = _combine_call(x_mid.reshape(T, D), gates, final_norm.reshape(1, D), y4, tc,
                          tile0=c * (chunk // tc), partial=y)
    return y.reshape(x.shape)


def kernel(x_prompt, x_sample, attn_norm, w_in, mla_q_norm, mla_w_uq, mla_kv_norm, mla_w_ukv,
           gla_w_gate_fwd, gla_b_gate_fwd, gla_w_gate_bwd, gla_b_gate_bwd, gla_out_norm, w_out, ffn_norm,
           router_w, router_b, expert_w_gate_up, expert_b_gate_up, expert_w_down, expert_b_down,
           final_norm):
    layer = (attn_norm, w_in, mla_q_norm, mla_w_uq, mla_kv_norm, mla_w_ukv, gla_w_gate_fwd,
             gla_b_gate_fwd, gla_w_gate_bwd, gla_b_gate_bwd, gla_out_norm, w_out, ffn_norm, router_w,
             router_b, expert_w_gate_up, expert_b_gate_up, expert_w_down, expert_b_down)
    assert all(p.shape[0] == 1 for p in layer), "single layer expected"
    assert x_prompt.shape[1:] == x_sample.shape[1:]
    prm = _prep_params(*[p[0] for p in layer])
    return _encoder(x_prompt, x_sample, prm, final_norm)
```

```python
import functools

import jax
import jax.numpy as jnp
import numpy as np
from jax import lax
from jax.experimental import pallas as pl
from jax.experimental.pallas import tpu as pltpu
from jax.experimental.pallas import tpu_sc as plsc

F32 = jnp.float32
BF16 = jnp.bfloat16

D_MODEL = 1024
MLA_HEADS = 8
MLA_NOPE = 64
MLA_ROPE = 32
MLA_V = 64
MLA_Q_LORA = 384
MLA_KV_LORA = 256
ROPE_THETA = 10000.0
GLA_HEADS = 4
GLA_DK = 64
GLA_DV = 128
GLA_GATE_RANK = 16
GLA_GATE_NORM = 16.0
N_EXPERTS = 32
TOP_K = 4
D_FF = 1024
SWIGLU_LIMIT = 7.0
SWIGLU_ALPHA = 1.702
MOE_BLOCK = 512
RMS_EPS = 1e-6

MLA_WIDTH = MLA_HEADS * MLA_V
GLA_WIDTH = GLA_HEADS * GLA_DV
GLA_KDIM = GLA_HEADS * GLA_DK
HALF_ROPE = MLA_ROPE // 2
QK_PAD = 128
LANES = 128
V_ONES_ROWS = 16

OFF_CQ = 0
OFF_CKV = OFF_CQ + MLA_Q_LORA
OFF_KR = OFF_CKV + MLA_KV_LORA
OFF_GQ = OFF_KR + LANES
OFF_GK = OFF_GQ + GLA_KDIM
OFF_GV = OFF_GK + GLA_KDIM
OFF_LR = OFF_GV + GLA_WIDTH
OFF_GR = OFF_LR + LANES
PROJ_COLS = OFF_GR + GLA_WIDTH

TOKEN_TILE = 512
ATTN_KEY_TILE = 256
ATTN_QUERY_TILE = 512
ATTN_LOOKAHEAD = 2
ATTN_SCORE_SLOTS = 4
GLA_TILE = 256
GLA_LEVELS = 8
SC_WINDOW = 32
SC_INDEX_PAD = 128
ATTN_TILES_PER_STEP = 2
PERM_GROUP = 256
VMEM_LIMIT = 56 * 1024 * 1024

LOG2_E = 1.4426950408889634
NT_DIMS = (((1,), (1,)), ((), ()))
TN_DIMS = (((0,), (0,)), ((), ()))


def _cparams(sem):
    return pltpu.CompilerParams(dimension_semantics=sem, vmem_limit_bytes=VMEM_LIMIT)


def _rms(x, gain):
    return x * lax.rsqrt(jnp.mean(x * x, axis=-1, keepdims=True) + RMS_EPS) * gain


def _split_bf16(x):
    hi = x.astype(BF16)
    lo = (x - hi.astype(F32)).astype(BF16)
    return hi, lo


def _pre_kernel(xa_ref, xb_ref, an_ref, win_ref, qn_ref, wuqT_ref, kvn_ref, wk_ref, wvT_ref,
                wgf_ref, bgf_ref, wgb_ref, bgb_ref, rc_ref, rs1_ref, rs2_ref, cosT_ref, sinT_ref,
                qT_ref, k_ref, vT_ref, gq_ref, gk_ref, gv_ref, laf_ref, lab_ref, gr_ref, *, n_first):
    x = jnp.where(pl.program_id(0) < n_first, xa_ref[...], xb_ref[...])
    h = _rms(x, an_ref[...]).astype(BF16)
    proj = jnp.dot(h, win_ref[...], preferred_element_type=F32)

    cqn = _rms(proj[:, OFF_CQ:OFF_CQ + MLA_Q_LORA], qn_ref[...]).astype(BF16)
    ckvn = _rms(proj[:, OFF_CKV:OFF_CKV + MLA_KV_LORA], kvn_ref[...]).astype(BF16)

    scale = (MLA_NOPE + MLA_ROPE) ** -0.5 * LOG2_E
    qT = lax.dot_general(wuqT_ref[...], cqn, NT_DIMS, preferred_element_type=F32) * scale
    n_nope = MLA_HEADS * MLA_NOPE
    n_half = MLA_HEADS * HALF_ROPE
    x1 = qT[n_nope:n_nope + n_half]
    x2 = qT[n_nope + n_half:n_nope + 2 * n_half]
    c = cosT_ref[...]
    s = sinT_ref[...]
    x1r = x1 * c - x2 * s
    x2r = x1 * s + x2 * c
    zpad = jnp.zeros((QK_PAD - MLA_NOPE - MLA_ROPE, qT.shape[1]), BF16)
    for hd in range(MLA_HEADS):
        qT_ref[hd, 0:MLA_NOPE, :] = qT[hd * MLA_NOPE:(hd + 1) * MLA_NOPE].astype(BF16)
        qT_ref[hd, MLA_NOPE:MLA_NOPE + HALF_ROPE, :] = x1r[hd * HALF_ROPE:(hd + 1) * HALF_ROPE].astype(BF16)
        qT_ref[hd, MLA_NOPE + HALF_ROPE:MLA_NOPE + MLA_ROPE, :] = (
            x2r[hd * HALF_ROPE:(hd + 1) * HALF_ROPE].astype(BF16))
        qT_ref[hd, MLA_NOPE + MLA_ROPE:QK_PAD, :] = zpad

    kr = proj[:, OFF_KR:OFF_KR + LANES]
    kr = (kr * rc_ref[...]
          + pltpu.roll(kr, LANES - HALF_ROPE, axis=1) * rs1_ref[...]
          + pltpu.roll(kr, HALF_ROPE, axis=1) * rs2_ref[...])
    kfull = jnp.dot(ckvn, wk_ref[...], preferred_element_type=F32)
    for hd in range(MLA_HEADS):
        k_ref[hd] = (kfull[:, hd * QK_PAD:(hd + 1) * QK_PAD] + kr).astype(BF16)

    vT = lax.dot_general(wvT_ref[...], ckvn, NT_DIMS, preferred_element_type=F32)
    for hd in range(MLA_HEADS):
        vT_ref[hd] = vT[hd * MLA_V:(hd + 1) * MLA_V].astype(BF16)

    gq_ref[...] = proj[:, OFF_GQ:OFF_GQ + GLA_KDIM] * (GLA_DK ** -0.5)
    gk_ref[...] = proj[:, OFF_GK:OFF_GK + GLA_KDIM]
    gv_ref[...] = proj[:, OFF_GV:OFF_GV + GLA_WIDTH]
    gr_ref[...] = proj[:, OFF_GR:OFF_GR + GLA_WIDTH]
    lr = proj[:, OFF_LR:OFF_LR + LANES].astype(BF16)

    def log_decay(w_ref, b_ref):
        z = jnp.dot(lr, w_ref[...], preferred_element_type=F32) + b_ref[...]
        return (jnp.minimum(z, 0.0) - jnp.log1p(jnp.exp(-jnp.abs(z)))) * (LOG2_E / GLA_GATE_NORM)

    laf_ref[...] = log_decay(wgf_ref, bgf_ref)
    lab_ref[...] = log_decay(wgb_ref, bgb_ref)


def _two_source_specs(n_first, tm, width):
    return [pl.BlockSpec((None, tm, width), lambda b, i: (jnp.minimum(b, n_first - 1), i, 0)),
            pl.BlockSpec((None, tm, width), lambda b, i: (jnp.maximum(b - n_first, 0), i, 0))]


def _pre_call(xa, xb, prm, rope, tm):
    n_first, S, D = xa.shape
    B = n_first + xb.shape[0]
    nS = S // tm
    H = MLA_HEADS

    def full(a):
        nd = a.ndim
        return pl.BlockSpec(a.shape, lambda b, i, _nd=nd: (0,) * _nd)

    tok = lambda w: pl.BlockSpec((None, tm, w), lambda b, i: (b, i, 0))
    in_specs = [
        *_two_source_specs(n_first, tm, D), full(prm["attn_norm"]), full(prm["w_in"]), full(prm["q_norm"]),
        full(prm["w_uqT"]), full(prm["kv_norm"]), full(prm["w_k"]), full(prm["w_vT"]),
        full(prm["w_gf"]), full(prm["b_gf"]), full(prm["w_gb"]), full(prm["b_gb"]),
        pl.BlockSpec((tm, LANES), lambda b, i: (i, 0)),
        pl.BlockSpec((tm, LANES), lambda b, i: (i, 0)),
        pl.BlockSpec((tm, LANES), lambda b, i: (i, 0)),
        pl.BlockSpec((H * HALF_ROPE, tm), lambda b, i: (0, i)),
        pl.BlockSpec((H * HALF_ROPE, tm), lambda b, i: (0, i)),
    ]
    out_shape = [
        jax.ShapeDtypeStruct((B, H, QK_PAD, S), BF16),
        jax.ShapeDtypeStruct((B, H, S, QK_PAD), BF16),
        jax.ShapeDtypeStruct((B, H, nS, MLA_V, tm), BF16),
        jax.ShapeDtypeStruct((B, S, GLA_KDIM), F32),
        jax.ShapeDtypeStruct((B, S, GLA_KDIM), F32),
        jax.ShapeDtypeStruct((B, S, GLA_WIDTH), F32),
        jax.ShapeDtypeStruct((B, S, GLA_KDIM), F32),
        jax.ShapeDtypeStruct((B, S, GLA_KDIM), F32),
        jax.ShapeDtypeStruct((B, S, GLA_WIDTH), F32),
    ]
    out_specs = [
        pl.BlockSpec((None, H, QK_PAD, tm), lambda b, i: (b, 0, 0, i)),
        pl.BlockSpec((None, H, tm, QK_PAD), lambda b, i: (b, 0, i, 0)),
        pl.BlockSpec((None, H, None, MLA_V, tm), lambda b, i: (b, 0, i, 0, 0)),
        tok(GLA_KDIM), tok(GLA_KDIM), tok(GLA_WIDTH), tok(GLA_KDIM), tok(GLA_KDIM), tok(GLA_WIDTH),
    ]
    return pl.pallas_call(
        functools.partial(_pre_kernel, n_first=n_first), grid=(B, nS),
        in_specs=in_specs, out_specs=out_specs, out_shape=out_shape,
        compiler_params=_cparams(("parallel", "parallel")), name="pre",
    )(xa, xb, prm["attn_norm"], prm["w_in"], prm["q_norm"], prm["w_uqT"], prm["kv_norm"], prm["w_k"],
      prm["w_vT"], prm["w_gf"], prm["b_gf"], prm["w_gb"], prm["b_gb"],
      rope["c"], rope["s1"], rope["s2"], rope["cosT"], rope["sinT"])


def _attn_kernel(qT_ref, k_ref, vT_ref, o_ref, s_ref, *, n_kblk, tiles):
    tq = qT_ref.shape[1] // tiles
    tk = k_ref.shape[0] // n_kblk
    per_tile = vT_ref.shape[2] // tk
    ones = jnp.ones((V_ONES_ROWS, tk), BF16)

    def values(j, p):
        v_blk = vT_ref[j // per_tile, :, (j % per_tile) * tk:(j % per_tile + 1) * tk]
        vext = jnp.concatenate([v_blk, ones], axis=0)
        return jnp.dot(vext, p, preferred_element_type=F32)

    for t in range(tiles):
        qT = qT_ref[:, t * tq:(t + 1) * tq]

        def scores(n):
            j = n % n_kblk
            sT = jnp.dot(k_ref[j * tk:(j + 1) * tk, :], qT, preferred_element_type=F32)
            s_ref[n % ATTN_SCORE_SLOTS] = sT
            return jnp.max(sT, axis=0, keepdims=True)

        first = t * n_kblk
        m = jnp.full((1, tq), -jnp.inf, F32)
        acc = jnp.zeros((MLA_V + V_ONES_ROWS, tq), F32)
        blk_max = {first + j: scores(first + j) for j in range(ATTN_LOOKAHEAD)}
        for n in range(first, first + n_kblk):
            if n + ATTN_LOOKAHEAD < first + n_kblk:
                blk_max[n + ATTN_LOOKAHEAD] = scores(n + ATTN_LOOKAHEAD)
            m_new = jnp.maximum(m, blk_max.pop(n))
            p = jnp.exp2(s_ref[n % ATTN_SCORE_SLOTS] - m_new).astype(BF16)
            acc = jnp.exp2(m - m_new) * acc + values(n % n_kblk, p)
            m = m_new
        o_ref[:, t * tq:(t + 1) * tq] = (acc[0:MLA_V] / acc[MLA_V:MLA_V + 1]).astype(o_ref.dtype)


def _attn_call(qT, k, vT, tq, tk):
    B, H, _, S = qT.shape
    n_vblk, v_tile = vT.shape[2], vT.shape[4]
    n_kblk = S // tk
    tiles = ATTN_TILES_PER_STEP if S % (tq * ATTN_TILES_PER_STEP) == 0 else 1
    assert n_kblk >= ATTN_LOOKAHEAD and S % (tq * tiles) == 0 and v_tile % tk == 0
    return pl.pallas_call(
        functools.partial(_attn_kernel, n_kblk=n_kblk, tiles=tiles),
        grid=(B, H, S // (tq * tiles)),
        in_specs=[
            pl.BlockSpec((None, None, QK_PAD, tq * tiles), lambda b, h, i: (b, h, 0, i)),
            pl.BlockSpec((None, None, S, QK_PAD), lambda b, h, i: (b, h, 0, 0)),
            pl.BlockSpec((None, None, n_vblk, MLA_V, v_tile), lambda b, h, i: (b, h, 0, 0, 0)),
        ],
        out_specs=pl.BlockSpec((None, None, MLA_V, tq * tiles), lambda b, h, i: (b, h, 0, i)),
        out_shape=jax.ShapeDtypeStruct((B, H, MLA_V, S), BF16),
        scratch_shapes=[pltpu.VMEM((ATTN_SCORE_SLOTS, tk, tq), F32)],
        compiler_params=_cparams(("parallel", "parallel", "parallel")), name="attn",
    )(qT, k, vT)


def _gla_consts():
    L = GLA_TILE
    i = np.arange(L)[:, None]
    j = np.arange(L)[None, :]
    x = i ^ j
    lidx = np.where(x > 0, np.floor(np.log2(np.maximum(x, 1))), -1).astype(np.int32)
    lidx_f = np.where(i > j, lidx, -1).astype(np.int32)
    lidx_b = np.where(i < j, lidx, -1).astype(np.int32)
    hd = np.arange(GLA_KDIM)[:, None] // GLA_DK
    hv = np.arange(GLA_WIDTH)[None, :] // GLA_DV
    bexp = (hd == hv).astype(np.float32)
    return jnp.asarray(lidx_f), jnp.asarray(lidx_b), jnp.asarray(bexp, BF16)


def _gla_direction(q, k, v, la, lidx, bexp_ref, ssum_ref, state_ref, o_ref, forward):
    L = GLA_TILE
    la_hi, la_lo = _split_bf16(la)
    vb = v.astype(BF16)
    lane_head = lax.broadcasted_iota(jnp.int32, (1, GLA_KDIM), 1) // GLA_DK
    head_masks = [(lane_head == h).astype(BF16) for h in range(GLA_HEADS)]
    row = lax.broadcasted_iota(jnp.int32, la.shape, 0)

    def widen(c, t, m):
        upper_half = (row & m) != 0
        sibling_total = jnp.where(upper_half, pltpu.roll(t, m, axis=0), pltpu.roll(t, L - m, axis=0))
        return c + jnp.where(upper_half, sibling_total, 0.0), t + sibling_total

    c, t = (la if forward else jnp.zeros_like(la)), la
    for lvl in range(GLA_LEVELS):
        if lvl > 0:
            c, t = widen(c, t, 1 << (lvl - 1))
        if forward:
            eq, ek = c, t - c
        else:
            eq, ek = t - c, c
        ql = (q * jnp.exp2(eq)).astype(BF16)
        kl = (k * jnp.exp2(ek)).astype(BF16)
        sel = lidx == lvl
        for h in range(GLA_HEADS):
            sc = lax.dot_general(ql * head_masks[h], kl, NT_DIMS, preferred_element_type=F32)
            ssum_ref[h] = jnp.where(sel, sc, ssum_ref[h] if lvl > 0 else 0.0)

    c, t = widen(c, t, L // 2)
    if forward:
        eq, ek = c, t - c
    else:
        eq, ek = t - c, c
    q_in = (q * jnp.exp2(eq)).astype(BF16)
    k_out = (k * jnp.exp2(ek)).astype(BF16)
    o = jnp.dot(q_in, state_ref[...].astype(BF16), preferred_element_type=F32)
    if forward:
        o = o + jnp.dot((q * k).astype(BF16), bexp_ref[...], preferred_element_type=F32) * v
    for h in range(GLA_HEADS):
        oh = jnp.dot(ssum_ref[h].astype(BF16), vb[:, h * GLA_DV:(h + 1) * GLA_DV],
                     preferred_element_type=F32)
        o_ref[:, h * GLA_DV:(h + 1) * GLA_DV] = o[:, h * GLA_DV:(h + 1) * GLA_DV] + oh

    ones = jnp.ones((L, LANES), BF16)
    tot_col = (lax.dot_general(la_hi, ones, TN_DIMS, preferred_element_type=F32)
               + lax.dot_general(la_lo, ones, TN_DIMS, preferred_element_type=F32))
    dec = jnp.exp2(tot_col)
    upd = lax.dot_general(k_out, vb, TN_DIMS, preferred_element_type=F32)
    for h in range(GLA_HEADS):
        cols = slice(h * GLA_DV, (h + 1) * GLA_DV)
        state_ref[:, cols] = dec * state_ref[:, cols] + upd[:, cols] * bexp_ref[:, cols].astype(F32)


def _gla_kernel(qf_ref, kf_ref, vf_ref, laf_ref, qb_ref, kb_ref, vb_ref, lab_ref,
                lidxf_ref, lidxb_ref, bexp_ref, of_ref, ob_ref, sf_ref, sb_ref, ssum_ref):
    @pl.when(pl.program_id(1) == 0)
    def _():
        sf_ref[...] = jnp.zeros_like(sf_ref)
        sb_ref[...] = jnp.zeros_like(sb_ref)

    _gla_direction(qf_ref[...], kf_ref[...], vf_ref[...], laf_ref[...], lidxf_ref[...],
                   bexp_ref, ssum_ref, sf_ref, of_ref, True)
    _gla_direction(qb_ref[...], kb_ref[...], vb_ref[...], lab_ref[...], lidxb_ref[...],
                   bexp_ref, ssum_ref, sb_ref, ob_ref, False)


def _gla_call(gq, gk, gv, laf, lab):
    B, S, _ = gq.shape
    L = GLA_TILE
    n = S // L
    lidx_f, lidx_b, bexp = _gla_consts()
    fwd = lambda w: pl.BlockSpec((None, L, w), lambda b, i: (b, i, 0))
    bwd = lambda w: pl.BlockSpec((None, L, w), lambda b, i: (b, n - 1 - i, 0))
    const = lambda a: pl.BlockSpec(a.shape, lambda b, i, _nd=a.ndim: (0,) * _nd)
    return pl.pallas_call(
        _gla_kernel, grid=(B, n),
        in_specs=[fwd(GLA_KDIM), fwd(GLA_KDIM), fwd(GLA_WIDTH), fwd(GLA_KDIM),
                  bwd(GLA_KDIM), bwd(GLA_KDIM), bwd(GLA_WIDTH), bwd(GLA_KDIM),
                  const(lidx_f), const(lidx_b), const(bexp)],
        out_specs=[fwd(GLA_WIDTH), bwd(GLA_WIDTH)],
        out_shape=[jax.ShapeDtypeStruct((B, S, GLA_WIDTH), F32)] * 2,
        scratch_shapes=[pltpu.VMEM((GLA_KDIM, GLA_WIDTH), F32), pltpu.VMEM((GLA_KDIM, GLA_WIDTH), F32),
                        pltpu.VMEM((GLA_HEADS, L, L), F32)],
        compiler_params=_cparams(("parallel", "arbitrary")), name="gla",
    )(gq, gk, gv, laf, gq, gk, gv, lab, lidx_f, lidx_b, bexp)


def _post_kernel(x_ref, oT_ref, of_ref, ob_ref, gr_ref, gon_ref, woa_ref, wob_ref, fn_ref,
                 rwT_ref, rb_ref, upper_ref,
                 xmid_ref, h2_ref, meta_ref, gates_ref, counts_ref, carry_ref):
    first = (pl.program_id(0) == 0) & (pl.program_id(1) == 0)

    @pl.when(first)
    def _():
        carry_ref[...] = jnp.zeros_like(carry_ref)

    o = of_ref[...] + ob_ref[...]
    gr = gr_ref[...]
    parts = []
    for h in range(GLA_HEADS):
        cols = slice(h * GLA_DV, (h + 1) * GLA_DV)
        parts.append(_rms(o[:, cols], gon_ref[...]) * jax.nn.silu(gr[:, cols]))
    gla = jnp.concatenate(parts, axis=1).astype(BF16)

    x_mid = (x_ref[...]
             + lax.dot_general(oT_ref[...], woa_ref[...], TN_DIMS, preferred_element_type=F32)
             + jnp.dot(gla, wob_ref[...], preferred_element_type=F32))
    xmid_ref[...] = x_mid
    h2 = _rms(x_mid, fn_ref[...])
    h2_ref[...] = h2

    h_hi, h_lo = _split_bf16(h2)
    w_hi, w_lo = _split_bf16(rwT_ref[...])
    by_h_hi = lax.dot_general(jnp.concatenate([w_hi, w_lo], axis=0), h_hi, NT_DIMS,
                              preferred_element_type=F32)
    logits = (by_h_hi[:N_EXPERTS] + by_h_hi[N_EXPERTS:]
              + lax.dot_general(w_hi, h_lo, NT_DIMS, preferred_element_type=F32)
              + rb_ref[...])
    tm = logits.shape[1]
    eidx = lax.broadcasted_iota(jnp.int32, (N_EXPERTS, tm), 0).astype(F32)
    vals, idxs, sels = [], [], []
    cur = logits
    for _ in range(TOP_K):
        mk = jnp.max(cur, axis=0, keepdims=True)
        ik = jnp.min(jnp.where(cur == mk, eidx, float(N_EXPERTS)), axis=0, keepdims=True)
        sel = eidx == ik
        vals.append(mk)
        idxs.append(ik)
        sels.append(sel)
        cur = jnp.where(sel, -jnp.inf, cur)
    exps = [jnp.exp(vk - vals[0]) for vk in vals]
    denom = exps[0] + exps[1] + exps[2] + exps[3]
    gates = [e / denom for e in exps]

    cnt = (sels[0] | sels[1] | sels[2] | sels[3])
    before = jnp.dot(cnt.astype(BF16), upper_ref[...], preferred_element_type=F32) + carry_ref[:, 0:1]
    ranks = [jnp.sum(jnp.where(sel, before, 0.0), axis=0, keepdims=True) for sel in sels]
    carry_ref[...] = carry_ref[...] + jnp.sum(cnt.astype(F32), axis=1, keepdims=True)
    counts_ref[...] = carry_ref[...]

    meta_ref[...] = jnp.concatenate(idxs + ranks, axis=0).astype(jnp.int32)
    gates_ref[...] = jnp.concatenate(gates + [jnp.zeros((TOP_K, tm), F32)], axis=0)


def _post_call(x, batch0, oT, o_f, o_b, gr, prm, tm):
    B, S, D = x.shape
    nS = S // tm
    upper = jnp.asarray(np.triu(np.ones((tm, tm), np.float32), k=1), BF16)
    full = lambda a: pl.BlockSpec(a.shape, lambda b, i, _nd=a.ndim: (0,) * _nd)
    own = lambda w: pl.BlockSpec((None, tm, w), lambda b, i: (b, i, 0))
    tok = lambda w: pl.BlockSpec((None, tm, w), lambda b, i: (batch0 + b, i, 0))
    colblk = lambda r: pl.BlockSpec((None, r, tm), lambda b, i: (batch0 + b, 0, i))
    flat = lambda r: pl.BlockSpec((r, tm), lambda b, i: (0, b * nS + i))
    return pl.pallas_call(
        _post_kernel, grid=(B, nS),
        in_specs=[own(D), colblk(MLA_WIDTH), tok(GLA_WIDTH), tok(GLA_WIDTH), tok(GLA_WIDTH),
                  full(prm["gla_out_norm"]), full(prm["w_out_a"]), full(prm["w_out_b"]),
                  full(prm["ffn_norm"]), full(prm["router_wT"]), full(prm["router_b"]), full(upper)],
        out_specs=[own(D), own(D), flat(2 * TOP_K), flat(2 * TOP_K),
                   pl.BlockSpec((N_EXPERTS, LANES), lambda b, i: (0, 0))],
        out_shape=[jax.ShapeDtypeStruct((B, S, D), F32), jax.ShapeDtypeStruct((B, S, D), F32),
                   jax.ShapeDtypeStruct((2 * TOP_K, B * S), jnp.int32),
                   jax.ShapeDtypeStruct((2 * TOP_K, B * S), F32),
                   jax.ShapeDtypeStruct((N_EXPERTS, LANES), F32)],
        scratch_shapes=[pltpu.VMEM((N_EXPERTS, LANES), F32)],
        compiler_params=_cparams(("arbitrary", "arbitrary")), name="post",
    )(x, oT, o_f, o_b, gr, prm["gla_out_norm"], prm["w_out_a"], prm["w_out_b"], prm["ffn_norm"],
      prm["router_wT"], prm["router_b"], upper)


def _sc_mesh():
    return plsc.VectorSubcoreMesh(core_axis_name="core", subcore_axis_name="subcore")


def _sc_scatter_rows(x, slots, n_rows):
    T, D = x.shape
    K = slots.shape[0]

    @pl.kernel(out_type=jax.ShapeDtypeStruct((n_rows, D), x.dtype), mesh=_sc_mesh(), scratch_types=[])
    def scatter(x_hbm, slots_hbm, out_hbm):
        def window(x_vmem, slots_vmem):
            for kk in range(K):
                pltpu.sync_copy(x_vmem, out_hbm.at[slots_vmem.at[kk, 0, pl.ds(0, SC_WINDOW)]])

        pltpu.emit_pipeline(
            window, grid=(T // SC_WINDOW,),
            in_specs=[pl.BlockSpec((SC_WINDOW, D), lambda i: (i, 0)),
                      pl.BlockSpec((K, 1, SC_INDEX_PAD), lambda i: (0, i, 0))],
            out_specs=[], core_axis_name=("core", "subcore"),
            dimension_semantics=(pltpu.PARALLEL,))(x_hbm, slots_hbm)

    return scatter(x, slots)


def _sc_gather_rows(y, slots, T):
    D = y.shape[1]
    K = slots.shape[0]

    @pl.kernel(out_type=jax.ShapeDtypeStruct((K, T, D), y.dtype), mesh=_sc_mesh(), scratch_types=[])
    def gather(y_hbm, slots_hbm, out_hbm):
        def window(slots_vmem, out_vmem):
            pltpu.sync_copy(y_hbm.at[slots_vmem.at[0, 0, pl.ds(0, SC_WINDOW)]], out_vmem.at[0])

        pltpu.emit_pipeline(
            window, grid=(K, T // SC_WINDOW),
            in_specs=[pl.BlockSpec((1, 1, SC_INDEX_PAD), lambda kk, i: (kk, i, 0))],
            out_specs=[pl.BlockSpec((1, SC_WINDOW, D), lambda kk, i: (kk, i, 0))],
            core_axis_name=("core", "subcore"),
            dimension_semantics=(pltpu.PARALLEL, pltpu.PARALLEL))(slots_hbm, out_hbm)

    return gather(y, slots)


def _expert_weights_kernel(wgu_ref, wdn_ref, perm_ref, wg_ref, wl_ref, wd_ref):
    perm = perm_ref[...]
    half = PERM_GROUP // 2
    for c in range(2 * D_FF // PERM_GROUP):
        w = wgu_ref[:, c * PERM_GROUP:(c + 1) * PERM_GROUP].astype(BF16)
        sep = jnp.dot(w, perm, preferred_element_type=F32).astype(BF16)
        wg_ref[:, c * half:(c + 1) * half] = sep[:, :half]
        wl_ref[:, c * half:(c + 1) * half] = sep[:, half:]
    wd_ref[...] = wdn_ref[...].astype(BF16)


def _expert_weights_call(w_gu, w_dn):
    E, D, _ = w_gu.shape
    half = PERM_GROUP // 2
    src = np.concatenate([2 * np.arange(half), 2 * np.arange(half) + 1])
    perm = jnp.asarray(np.arange(PERM_GROUP)[:, None] == src[None, :], BF16)
    per_expert = lambda r, c: pl.BlockSpec((None, r, c), lambda e: (e, 0, 0))
    return pl.pallas_call(
        _expert_weights_kernel, grid=(E,),
        in_specs=[per_expert(D, 2 * D_FF), per_expert(D_FF, D),
                  pl.BlockSpec((PERM_GROUP, PERM_GROUP), lambda e: (0, 0))],
        out_specs=[per_expert(D, D_FF), per_expert(D, D_FF), per_expert(D_FF, D)],
        out_shape=[jax.ShapeDtypeStruct((E, D, D_FF), BF16), jax.ShapeDtypeStruct((E, D, D_FF), BF16),
                   jax.ShapeDtypeStruct((E, D_FF, D), BF16)],
        compiler_params=_cparams(("parallel",)), name="expert_weights",
    )(w_gu, w_dn, perm)


def _expert_kernel(be_ref, nv_ref, xs_ref, wg_ref, wl_ref, bg_ref, bl_ref, wd_ref, bd_ref, ys_ref):
    del be_ref
    valid = pl.program_id(0) < nv_ref[0]

    @pl.when(valid)
    def _():
        xb = xs_ref[...].astype(BF16)
        g = jnp.dot(xb, wg_ref[...], preferred_element_type=F32) + bg_ref[...]
        l = jnp.dot(xb, wl_ref[...], preferred_element_type=F32) + bl_ref[...]
        glu = jnp.minimum(g, SWIGLU_LIMIT)
        lin = jnp.clip(l, -SWIGLU_LIMIT, SWIGLU_LIMIT)
        act = glu * jax.nn.sigmoid(SWIGLU_ALPHA * glu) * (lin + 1.0)
        ys_ref[...] = jnp.dot(act.astype(BF16), wd_ref[...], preferred_element_type=F32) + bd_ref[...]

    @pl.when(jnp.logical_not(valid))
    def _():
        ys_ref[...] = jnp.zeros_like(ys_ref)


def _expert_call(block_expert, n_valid, xs, prm):
    n_rows, D = xs.shape
    n_blocks = n_rows // MOE_BLOCK
    rows = lambda i, be, nv: (jnp.minimum(i, nv[0] - 1), 0)
    wsel = lambda i, be, nv: (be[i], 0, 0)
    grid_spec = pltpu.PrefetchScalarGridSpec(
        num_scalar_prefetch=2, grid=(n_blocks,),
        in_specs=[pl.BlockSpec((MOE_BLOCK, D), rows),
                  pl.BlockSpec((None, D, D_FF), wsel), pl.BlockSpec((None, D, D_FF), wsel),
                  pl.BlockSpec((None, 1, D_FF), wsel), pl.BlockSpec((None, 1, D_FF), wsel),
                  pl.BlockSpec((None, D_FF, D), wsel), pl.BlockSpec((None, 1, D), wsel)],
        out_specs=pl.BlockSpec((MOE_BLOCK, D), lambda i, be, nv: (i, 0)))
    return pl.pallas_call(
        _expert_kernel, grid_spec=grid_spec,
        out_shape=jax.ShapeDtypeStruct((n_rows, D), F32),
        compiler_params=_cparams(("arbitrary",)), name="experts",
    )(block_expert, n_valid, xs, prm["w_glu"], prm["w_lin"], prm["b_glu"], prm["b_lin"],
      prm["w_dn"], prm["b_dn"])


def _combine_kernel(xmid_ref, gates_ref, fnorm_ref, y4_ref, out_ref):
    g = jnp.transpose(gates_ref[...])
    y = xmid_ref[...]
    for kk in range(TOP_K):
        y = y + y4_ref[kk] * g[:, kk:kk + 1]
    out_ref[...] = _rms(y, fnorm_ref[...])


def _combine_call(x_mid, gates, final_norm, y4, tc):
    T, D = x_mid.shape
    return pl.pallas_call(
        _combine_kernel, grid=(T // tc,),
        in_specs=[pl.BlockSpec((tc, D), lambda i: (i, 0)),
                  pl.BlockSpec((2 * TOP_K, tc), lambda i: (0, i)),
                  pl.BlockSpec(final_norm.shape, lambda i: (0, 0)),
                  pl.BlockSpec((TOP_K, tc, D), lambda i: (0, i, 0))],
        out_specs=pl.BlockSpec((tc, D), lambda i: (i, 0)),
        out_shape=jax.ShapeDtypeStruct((T, D), F32),
        compiler_params=_cparams(("parallel",)), name="combine",
    )(x_mid, gates, final_norm, y4)


def _prep_params(attn_norm, w_in, mla_q_norm, mla_w_uq, mla_kv_norm, mla_w_ukv, gla_w_gate_fwd,
                 gla_b_gate_fwd, gla_w_gate_bwd, gla_b_gate_bwd, gla_out_norm, w_out, ffn_norm,
                 router_w, router_b, w_gu, b_gu, w_dn, b_dn):
    D = D_MODEL
    o = np.cumsum((0, MLA_Q_LORA, MLA_KV_LORA, MLA_ROPE, GLA_KDIM, GLA_KDIM, GLA_WIDTH,
                   2 * GLA_GATE_RANK, GLA_WIDTH))
    seg = [w_in[:, o[n]:o[n + 1]] for n in range(8)]
    z = lambda n: jnp.zeros((D, n), w_in.dtype)
    w_in_p = jnp.concatenate(
        [seg[0], seg[1], z(MLA_NOPE), seg[2], z(LANES - MLA_NOPE - MLA_ROPE), seg[3], seg[4], seg[5],
         seg[6], z(LANES - 2 * GLA_GATE_RANK), seg[7]], axis=1).astype(BF16)

    wq = mla_w_uq.reshape(MLA_Q_LORA, MLA_HEADS, MLA_NOPE + MLA_ROPE)
    w_uq_p = jnp.concatenate(
        [wq[:, :, :MLA_NOPE].reshape(MLA_Q_LORA, -1),
         wq[:, :, MLA_NOPE:MLA_NOPE + HALF_ROPE].reshape(MLA_Q_LORA, -1),
         wq[:, :, MLA_NOPE + HALF_ROPE:].reshape(MLA_Q_LORA, -1)], axis=1)
    wkv = mla_w_ukv.reshape(MLA_KV_LORA, MLA_HEADS, MLA_NOPE + MLA_V)
    w_k = jnp.concatenate([wkv[:, :, :MLA_NOPE], jnp.zeros((MLA_KV_LORA, MLA_HEADS, QK_PAD - MLA_NOPE),
                                                            wkv.dtype)], axis=2)
    w_v = wkv[:, :, MLA_NOPE:].reshape(MLA_KV_LORA, -1)

    def gate_w(w, row0):
        full = jnp.zeros((LANES, GLA_KDIM), w.dtype)
        return full.at[row0:row0 + GLA_GATE_RANK].set(w).astype(BF16)

    E = N_EXPERTS
    w_glu, w_lin, w_dn_b = _expert_weights_call(w_gu, w_dn)
    return {
        "attn_norm": attn_norm.reshape(1, D), "w_in": w_in_p,
        "q_norm": mla_q_norm.reshape(1, -1), "w_uqT": w_uq_p.T.astype(BF16),
        "kv_norm": mla_kv_norm.reshape(1, -1),
        "w_k": w_k.reshape(MLA_KV_LORA, -1).astype(BF16), "w_vT": w_v.T.astype(BF16),
        "w_gf": gate_w(gla_w_gate_fwd, 0), "b_gf": gla_b_gate_fwd.reshape(1, -1),
        "w_gb": gate_w(gla_w_gate_bwd, GLA_GATE_RANK), "b_gb": gla_b_gate_bwd.reshape(1, -1),
        "gla_out_norm": gla_out_norm.reshape(1, -1),
        "w_out_a": w_out[:MLA_WIDTH].astype(BF16), "w_out_b": w_out[MLA_WIDTH:].astype(BF16),
        "ffn_norm": ffn_norm.reshape(1, D),
        "router_wT": router_w.T, "router_b": router_b.reshape(E, 1),
        "w_glu": w_glu, "w_lin": w_lin, "w_dn": w_dn_b,
        "b_glu": b_gu[:, 0::2].reshape(E, 1, D_FF), "b_lin": b_gu[:, 1::2].reshape(E, 1, D_FF),
        "b_dn": b_dn.reshape(E, 1, D),
    }


def _rope_tables(S):
    inv_freq = jnp.power(ROPE_THETA, -jnp.arange(0, MLA_ROPE, 2, dtype=F32) / MLA_ROPE)
    ang = jnp.arange(S, dtype=F32)[:, None] * inv_freq[None, :]
    cos, sin = jnp.cos(ang), jnp.sin(ang)
    z = lambda n: jnp.zeros((S, n), F32)
    tail = LANES - MLA_NOPE - MLA_ROPE
    return {
        "c": jnp.concatenate([z(MLA_NOPE), cos, cos, z(tail)], axis=1),
        "s1": jnp.concatenate([z(MLA_NOPE), -sin, z(HALF_ROPE), z(tail)], axis=1),
        "s2": jnp.concatenate([z(MLA_NOPE), z(HALF_ROPE), sin, z(tail)], axis=1),
        "cosT": jnp.tile(cos.T, (MLA_HEADS, 1)), "sinT": jnp.tile(sin.T, (MLA_HEADS, 1)),
    }


def _encoder(xa, xb, prm, final_norm):
    n_first, S, D = xa.shape
    B = n_first + xb.shape[0]
    rope = _rope_tables(S)
    qT, k, vT, gq, gk, gv, laf, lab, gr = _pre_call(xa, xb, prm, rope, min(TOKEN_TILE, S))
    oT = _attn_call(qT, k, vT, min(ATTN_QUERY_TILE, S), ATTN_KEY_TILE).reshape(B, MLA_WIDTH, S)
    o_f, o_b = _gla_call(gq, gk, gv, laf, lab)
    mixed = (oT, o_f, o_b, gr)
    return (_moe_block(xa, 0, mixed, prm, final_norm), _moe_block(xb, n_first, mixed, prm, final_norm))


def _moe_block(x, batch0, mixed, prm, final_norm):
    B, S, D = x.shape
    x_mid, h2, meta, gates, counts = _post_call(x, batch0, *mixed, prm, min(TOKEN_TILE, S))

    T = B * S
    n_rows = T * TOP_K + N_EXPERTS * MOE_BLOCK
    n_blocks = n_rows // MOE_BLOCK
    cnt = counts[:, 0].astype(jnp.int32)
    padded = ((cnt + MOE_BLOCK - 1) // MOE_BLOCK) * MOE_BLOCK
    pends = jnp.cumsum(padded)
    starts = (pends - padded).astype(jnp.int32)
    n_valid = (pends[-1] // MOE_BLOCK).astype(jnp.int32).reshape(1)
    blk = jnp.minimum(jnp.arange(n_blocks, dtype=jnp.int32), n_valid[0] - 1) * MOE_BLOCK
    block_expert = jnp.minimum(jnp.sum(pends[None, :] <= blk[:, None], axis=1), N_EXPERTS - 1).astype(jnp.int32)

    experts = jnp.arange(N_EXPERTS, dtype=jnp.int32)[:, None, None]
    slots = jnp.sum(jnp.where(meta[None, :TOP_K] == experts, starts[:, None, None], 0), axis=0) + meta[TOP_K:]
    slots = jnp.pad(slots.reshape(TOP_K, T // SC_WINDOW, SC_WINDOW),
                    ((0, 0), (0, 0), (0, SC_INDEX_PAD - SC_WINDOW)))

    xs = _sc_scatter_rows(h2.reshape(T, D), slots, n_rows)
    ys = _expert_call(block_expert, n_valid, xs, prm)
    y4 = _sc_gather_rows(ys, slots, T)
    y = _combine_call(x_mid.reshape(T, D), gates, final_norm.reshape(1, D), y4, min(TOKEN_TILE, S))
    return y.reshape(x.shape)


def kernel(x_prompt, x_sample, attn_norm, w_in, mla_q_norm, mla_w_uq, mla_kv_norm, mla_w_ukv,
           gla_w_gate_fwd, gla_b_gate_fwd, gla_w_gate_bwd, gla_b_gate_bwd, gla_out_norm, w_out, ffn_norm,
           router_w, router_b, expert_w_gate_up, expert_b_gate_up, expert_w_down, expert_b_down,
           final_norm):
    layer = (attn_norm, w_in, mla_q_norm, mla_w_uq, mla_kv_norm, mla_w_ukv, gla_w_gate_fwd,
             gla_b_gate_fwd, gla_w_gate_bwd, gla_b_gate_bwd, gla_out_norm, w_out, ffn_norm, router_w,
             router_b, expert_w_gate_up, expert_b_gate_up, expert_w_down, expert_b_down)
    assert all(p.shape[0] == 1 for p in layer), "single layer expected"
    assert x_prompt.shape[1:] == x_sample.shape[1:]
    prm = _prep_params(*[p[0] for p in layer])
    return _encoder(x_prompt, x_sample, prm, final_norm)
```
